```python
import math
import jax
import jax.numpy as jnp
from jax import lax
import numpy as np

D_MODEL = 2048
BATCH = 1
SEQ = 8192
DEPTH = 2

GRID_W = 64
CTX_LEN = 256
HEAD_DIM = 128
ROPE_THETA = 10000.0
EPS = 1e-6
NEG_INF = -1e30
Q_BLOCK = 128
N_MOD = 6

A_HEADS = D_MODEL // (2 * HEAD_DIM)
A_SUB = HEAD_DIM // 2
B_HEADS = D_MODEL // (2 * HEAD_DIM)
B_KV_HEADS = B_HEADS // 4
C_HEADS = D_MODEL // (2 * HEAD_DIM)
C_KV_HEADS = C_HEADS // 4
C_WINDOW = 128
C_BAND_SIDE = -(-C_WINDOW // Q_BLOCK)
D_HEADS = D_MODEL // (2 * HEAD_DIM)
NA_KH = 8
NA_KW = 16
D_FF = 5632
CONV_W = 3
N_EVEN = (DEPTH + 1) // 2
N_ODD = DEPTH // 2

EVEN_SIZES = (A_HEADS * 2 * A_SUB, A_HEADS * 2 * A_SUB, A_HEADS * HEAD_DIM,
              B_HEADS * HEAD_DIM, B_KV_HEADS * HEAD_DIM, B_KV_HEADS * HEAD_DIM)
ODD_SIZES = (C_HEADS * HEAD_DIM, C_KV_HEADS * HEAD_DIM, C_KV_HEADS * HEAD_DIM,
             D_HEADS * HEAD_DIM, D_HEADS * HEAD_DIM, D_HEADS * HEAD_DIM)
IN_WIDTH = sum(EVEN_SIZES)
MIX_WIDTH = (A_HEADS + B_HEADS) * HEAD_DIM

kernel_name = 'hybrid_diffattn_gqa_swa_natten_convffn'


def rms_norm(x, g):
    xf = x.astype(jnp.float32)
    y = xf * lax.rsqrt(jnp.mean(xf * xf, axis=-1, keepdims=True) + EPS)
    return (y * g.astype(jnp.float32)).astype(x.dtype)


def modulate(h, shift, scale):
    return h * (1 + scale) + shift


def split_cols(t, sizes):
    parts, off = [], 0
    for s in sizes:
        parts.append(t[..., off:off + s])
        off += s
    return parts


def rope_1d(x, pos):
    half = x.shape[-1] // 2
    freqs = ROPE_THETA ** (-jnp.arange(half, dtype=jnp.float32) / half)
    ang = pos.astype(jnp.float32)[:, None] * freqs[None, :]
    cos = jnp.cos(ang)[:, None, :]
    sin = jnp.sin(ang)[:, None, :]
    xf = x.astype(jnp.float32)
    x1, x2 = xf[..., :half], xf[..., half:]
    return jnp.concatenate([x1 * cos - x2 * sin, x2 * cos + x1 * sin], axis=-1).astype(x.dtype)


def rope_2d(x, rows, cols):
    h = x.shape[-1] // 2
    return jnp.concatenate([rope_1d(x[..., :h], rows), rope_1d(x[..., h:], cols)], axis=-1)


def sweep_query_blocks(fn, q):
    b, s = q.shape[:2]
    nb = s // Q_BLOCK
    qb = jnp.moveaxis(q.reshape((b, nb, Q_BLOCK) + q.shape[2:]), 1, 0)
    o = jnp.moveaxis(lax.map(fn, qb), 0, 1)
    return o.reshape((b, s) + o.shape[3:])


def gqa_attend(q, k, v):
    b, s, hq, d = q.shape
    hkv = k.shape[2]
    qg = q.reshape(b, s, hkv, hq // hkv, d)
    sc = jnp.einsum('bqhgd,bkhd->bhgqk', qg, k).astype(jnp.float32) * (d ** -0.5)
    p = jax.nn.softmax(sc, axis=-1).astype(v.dtype)
    o = jnp.einsum('bhgqk,bkhd->bqhgd', p, v)
    return o.reshape(b, s, hq, v.shape[-1])


def diff_attend(q, k, v, lam):
    sc = jnp.einsum('bqhmd,bkhmd->bhmqk', q, k).astype(jnp.float32) * (q.shape[-1] ** -0.5)
    p = jax.nn.softmax(sc, axis=-1)
    pd = (p[:, :, 0] - lam * p[:, :, 1]).astype(v.dtype)
    return jnp.einsum('bhqk,bkhd->bqhd', pd, v)


def sink_softmax(sc, sink_hg):
    col = jnp.broadcast_to(sink_hg.astype(jnp.float32)[:, :, None, None], sc.shape[:-1] + (1,))
    p = jax.nn.softmax(jnp.concatenate([sc, col], axis=-1), axis=-1)
    return p[..., :-1]


def ctx_sink_attend(q, k, v, sink):
    b, s, hq, d = q.shape
    hkv = k.shape[2]
    g = hq // hkv
    qg = q.reshape(b, s, hkv, g, d)
    sc = jnp.einsum('bqhgd,bkhd->bhgqk', qg, k).astype(jnp.float32) * (d ** -0.5)
    p = sink_softmax(sc, sink.reshape(hkv, g)).astype(v.dtype)
    return jnp.einsum('bhgqk,bkhd->bqhgd', p, v).reshape(b, s, hq, d)


def window_sink_attend(q, k, v, k_ctx, v_ctx, sink):
    b, s, hq, d = q.shape
    hkv = k.shape[2]
    g = hq // hkv
    t = k_ctx.shape[1]
    nb = s // Q_BLOCK
    pad = C_BAND_SIDE * Q_BLOCK
    n_band = 2 * C_BAND_SIDE + 1
    band_len = n_band * Q_BLOCK

    def band(z):
        zp = jnp.pad(z, ((0, 0), (pad, pad), (0, 0), (0, 0)))
        zb = zp.reshape(b, nb + 2 * C_BAND_SIDE, Q_BLOCK, hkv, z.shape[-1])
        return jnp.concatenate([zb[:, j:j + nb] for j in range(n_band)], axis=2)

    kb, vb = band(k), band(v)
    qb = q.reshape(b, nb, Q_BLOCK, hkv, g, d)
    blk = jnp.arange(nb)[:, None, None]
    qpos = blk * Q_BLOCK + jnp.arange(Q_BLOCK)[None, :, None]
    kpos = blk * Q_BLOCK - pad + jnp.arange(band_len)[None, None, :]
    valid = (jnp.abs(qpos - kpos) <= C_WINDOW) & (kpos >= 0) & (kpos < s)
    scale = d ** -0.5
    s_lat = jnp.einsum('bnqhgd,bnkhd->bnhgqk', qb, kb).astype(jnp.float32) * scale
    s_lat = jnp.where(valid[None, :, None, None], s_lat, NEG_INF)
    s_ctx = jnp.einsum('bnqhgd,bkhd->bnhgqk', qb, k_ctx).astype(jnp.float32) * scale
    p = sink_softmax(jnp.concatenate([s_ctx, s_lat], axis=-1), sink.reshape(hkv, g)).astype(v.dtype)
    o = (jnp.einsum('bnhgqk,bkhd->bnqhgd', p[..., :t], v_ctx)
         + jnp.einsum('bnhgqk,bnkhd->bnqhgd', p[..., t:], vb))
    return o.reshape(b, s, hq, d)


def neighbourhood_attend(q, k, v, k_ctx, v_ctx, rpb):
    b, s, h, d = q.shape
    t = k_ctx.shape[1]
    n_rows = s // GRID_W
    kh = min(NA_KH, n_rows)
    qg = q.reshape(b, n_rows, GRID_W, h, d)
    kg = k.reshape(b, n_rows, GRID_W, h, d)
    vg = v.reshape(b, n_rows, GRID_W, h, d)
    r = jnp.arange(n_rows)
    r_start = jnp.clip(r - kh // 2, 0, n_rows - kh)
    row_idx = r_start[:, None] + jnp.arange(kh)[None, :]
    k_rows = kg[:, row_idx]
    v_rows = vg[:, row_idx]
    cq = jnp.arange(GRID_W)
    c_start = jnp.clip(cq - NA_KW // 2, 0, GRID_W - NA_KW)
    col_in = (cq[None, :] >= c_start[:, None]) & (cq[None, :] < c_start[:, None] + NA_KW)
    r_off = row_idx - r[:, None] + (NA_KH - 1)
    c_off = jnp.clip(cq[None, :] - cq[:, None], -(NA_KW - 1), NA_KW - 1) + (NA_KW - 1)
    bias = rpb.astype(jnp.float32)[:, r_off[:, None, :, None], c_off[None, :, None, :]]
    scale = d ** -0.5
    s_win = jnp.einsum('brwhd,brikhd->bhrwik', qg, k_rows).astype(jnp.float32) * scale + bias[None]
    s_win = jnp.where(col_in[None, None, None, :, None, :], s_win, NEG_INF)
    s_win = s_win.reshape(b, h, n_rows, GRID_W, kh * GRID_W)
    s_ctx = jnp.einsum('brwhd,bkhd->bhrwk', qg, k_ctx).astype(jnp.float32) * scale
    p = jax.nn.softmax(jnp.concatenate([s_ctx, s_win], axis=-1), axis=-1).astype(v.dtype)
    p_win = p[..., t:].reshape(b, h, n_rows, GRID_W, kh, GRID_W)
    o = (jnp.einsum('bhrwk,bkhd->brwhd', p[..., :t], v_ctx)
         + jnp.einsum('bhrwik,brikhd->brwhd', p_win, v_rows))
    return o.reshape(b, s, h, d)


def even_layer_mixers(p, pc, rows, cols, qk_a, lam_params, subln_g, qk_b, lam_init, need_ctx):
    b, s, _ = p.shape
    t = pc.shape[1]
    aq, ak, av, bq, bk, bv = split_cols(p, EVEN_SIZES)
    caq, cak, cav, cbq, cbk, cbv = split_cols(pc, EVEN_SIZES)
    lam = (jnp.exp(jnp.sum(lam_params[0] * lam_params[1]).astype(jnp.float32))
           - jnp.exp(jnp.sum(lam_params[2] * lam_params[3]).astype(jnp.float32)) + lam_init)
    aq_l = rope_2d(rms_norm(aq.reshape(b, s, 2 * A_HEADS, A_SUB), qk_a[0]), rows, cols).reshape(b, s, A_HEADS, 2, A_SUB)
    ak_l = rope_2d(rms_norm(ak.reshape(b, s, 2 * A_HEADS, A_SUB), qk_a[1]), rows, cols).reshape(b, s, A_HEADS, 2, A_SUB)
    ak_c = rms_norm(cak.reshape(b, t, A_HEADS, 2, A_SUB), qk_a[1])
    av_c = cav.reshape(b, t, A_HEADS, HEAD_DIM)
    ka_all = jnp.concatenate([ak_c, ak_l], axis=1)
    va_all = jnp.concatenate([av_c, av.reshape(b, s, A_HEADS, HEAD_DIM)], axis=1)
    ya = sweep_query_blocks(lambda qb: diff_attend(qb, ka_all, va_all, lam), aq_l)
    ya = rms_norm(ya, subln_g) * (1.0 - lam_init)
    bq_l = rope_2d(rms_norm(bq.reshape(b, s, B_HEADS, HEAD_DIM), qk_b[0]), rows, cols)
    bk_l = rope_2d(rms_norm(bk.reshape(b, s, B_KV_HEADS, HEAD_DIM), qk_b[1]), rows, cols)
    bk_c = rms_norm(cbk.reshape(b, t, B_KV_HEADS, HEAD_DIM), qk_b[1])
    bv_c = cbv.reshape(b, t, B_KV_HEADS, HEAD_DIM)
    kb_all = jnp.concatenate([bk_c, bk_l], axis=1)
    vb_all = jnp.concatenate([bv_c, bv.reshape(b, s, B_KV_HEADS, HEAD_DIM)], axis=1)
    yb = sweep_query_blocks(lambda qb: gqa_attend(qb, kb_all, vb_all), bq_l)
    y = jnp.concatenate([ya.reshape(b, s, -1), yb.reshape(b, s, -1)], axis=-1)
    yc = None
    if need_ctx:
        aq_c = rms_norm(caq.reshape(b, t, A_HEADS, 2, A_SUB), qk_a[0])
        ya_c = rms_norm(diff_attend(aq_c, ak_c, av_c, lam), subln_g) * (1.0 - lam_init)
        bq_c = rms_norm(cbq.reshape(b, t, B_HEADS, HEAD_DIM), qk_b[0])
        yb_c = gqa_attend(bq_c, bk_c, bv_c)
        yc = jnp.concatenate([ya_c.reshape(b, t, -1), yb_c.reshape(b, t, -1)], axis=-1)
    return y, yc


def odd_layer_mixers(p, pc, rows, cols, qk_c, sink, qk_d, rpb, need_ctx):
    b, s, _ = p.shape
    t = pc.shape[1]
    cq, ck, cv, dq, dk, dv = split_cols(p, ODD_SIZES)
    ccq, cck, ccv, cdq, cdk, cdv = split_cols(pc, ODD_SIZES)
    cq_l = rope_2d(rms_norm(cq.reshape(b, s, C_HEADS, HEAD_DIM), qk_c[0]), rows, cols)
    ck_l = rope_2d(rms_norm(ck.reshape(b, s, C_KV_HEADS, HEAD_DIM), qk_c[1]), rows, cols)
    ck_c = rms_norm(cck.reshape(b, t, C_KV_HEADS, HEAD_DIM), qk_c[1])
    cv_c = ccv.reshape(b, t, C_KV_HEADS, HEAD_DIM)
    yc_lat = window_sink_attend(cq_l, ck_l, cv.reshape(b, s, C_KV_HEADS, HEAD_DIM), ck_c, cv_c, sink)
    dq_l = rms_norm(dq.reshape(b, s, D_HEADS, HEAD_DIM), qk_d[0])
    dk_l = rms_norm(dk.reshape(b, s, D_HEADS, HEAD_DIM), qk_d[1])
    dk_c = rms_norm(cdk.reshape(b, t, D_HEADS, HEAD_DIM), qk_d[1])
    dv_c = cdv.reshape(b, t, D_HEADS, HEAD_DIM)
    yd_lat = neighbourhood_attend(dq_l, dk_l, dv.reshape(b, s, D_HEADS, HEAD_DIM), dk_c, dv_c, rpb)
    y = jnp.concatenate([yc_lat.reshape(b, s, -1), yd_lat.reshape(b, s, -1)], axis=-1)
    yc = None
    if need_ctx:
        cq_c = rms_norm(ccq.reshape(b, t, C_HEADS, HEAD_DIM), qk_c[0])
        yc_c = ctx_sink_attend(cq_c, ck_c, cv_c, sink)
        dq_c = rms_norm(cdq.reshape(b, t, D_HEADS, HEAD_DIM), qk_d[0])
        yd_c = gqa_attend(dq_c, dk_c, dv_c)
        yc = jnp.concatenate([yc_c.reshape(b, t, -1), yd_c.reshape(b, t, -1)], axis=-1)
    return y, yc


def depthwise_conv(u, w, bias):
    s = u.shape[1]
    pad = CONV_W // 2
    up = jnp.pad(u, ((0, 0), (pad, pad), (0, 0)))
    out = bias + up[:, 0:s] * w[0]
    for j in range(1, CONV_W):
        out = out + up[:, j:j + s] * w[j]
    return out


def conv_ffn(h, w_up, conv_w, conv_b, w_down):
    u = depthwise_conv(h @ w_up, conv_w, conv_b)
    a, g = u[..., :D_FF], u[..., D_FF:]
    return (jax.nn.silu(g) * a) @ w_down


def lambda_init_for(layer_idx):
    return 0.8 - 0.6 * math.exp(-0.3 * layer_idx)


def setup_inputs(seed: int = 0) -> dict:
    key = jax.random.key(seed)
    ks = jax.random.split(key, 22)
    f32 = jnp.float32

    def nrm(k, shape, scale):
        return jax.random.normal(k, shape, f32) * scale

    return {
        'x': nrm(ks[0], (BATCH, SEQ, D_MODEL), 1.0),
        'c': nrm(ks[1], (BATCH, D_MODEL), 1.0),
        'ctx': nrm(ks[2], (BATCH, CTX_LEN, D_MODEL), 1.0),
        'c_ctx': nrm(ks[3], (D_MODEL,), 1.0),
        'w_ada': nrm(ks[4], (DEPTH, D_MODEL, N_MOD * D_MODEL), 0.5 * D_MODEL ** -0.5),
        'b_ada': nrm(ks[5], (DEPTH, N_MOD * D_MODEL), 0.02),
        'norm1_g': 1.0 + nrm(ks[6], (DEPTH, D_MODEL), 0.02),
        'w_in': nrm(ks[7], (DEPTH, D_MODEL, IN_WIDTH), D_MODEL ** -0.5),
        'w_out': nrm(ks[8], (DEPTH, MIX_WIDTH, D_MODEL), MIX_WIDTH ** -0.5),
        'a_qk_g': 1.0 + nrm(ks[9], (N_EVEN, 2, A_SUB), 0.02),
        'a_lambda': nrm(ks[10], (N_EVEN, 4, A_SUB), 0.1),
        'a_subln_g': 1.0 + nrm(ks[11], (N_EVEN, HEAD_DIM), 0.02),
        'b_qk_g': 1.0 + nrm(ks[12], (N_EVEN, 2, HEAD_DIM), 0.02),
        'c_qk_g': 1.0 + nrm(ks[13], (N_ODD, 2, HEAD_DIM), 0.02),
        'c_sink': nrm(ks[14], (N_ODD, C_HEADS), 0.5),
        'd_qk_g': 1.0 + nrm(ks[15], (N_ODD, 2, HEAD_DIM), 0.02),
        'd_rpb': nrm(ks[16], (N_ODD, D_HEADS, 2 * NA_KH - 1, 2 * NA_KW - 1), 0.1),
        'norm2_g': 1.0 + nrm(ks[17], (DEPTH, D_MODEL), 0.02),
        'w_up': nrm(ks[18], (DEPTH, D_MODEL, 2 * D_FF), D_MODEL ** -0.5),
        'conv_w': nrm(ks[19], (DEPTH, CONV_W, 2 * D_FF), CONV_W ** -0.5),
        'conv_b': nrm(ks[20], (DEPTH, 2 * D_FF), 0.02),
        'w_down': nrm(ks[21], (DEPTH, D_FF, D_MODEL), D_FF ** -0.5),
    }


def reference(x, c, ctx, c_ctx, w_ada, b_ada, norm1_g, w_in, w_out, a_qk_g, a_lambda, a_subln_g,
              b_qk_g, c_qk_g, c_sink, d_qk_g, d_rpb, norm2_g, w_up, conv_w, conv_b, w_down):
    s = x.shape[1]
    pos = jnp.arange(s, dtype=jnp.int32)
    rows, cols = pos // GRID_W, pos % GRID_W
    xc = ctx
    for i in range(DEPTH):
        need_ctx = i < DEPTH - 1
        mod = jax.nn.silu(c) @ w_ada[i] + b_ada[i]
        mod_c = jax.nn.silu(c_ctx) @ w_ada[i] + b_ada[i]
        sh1, sc1, g1, sh2, sc2, g2 = jnp.split(mod[:, None, :], N_MOD, axis=-1)
        csh1, csc1, cg1, csh2, csc2, cg2 = jnp.split(mod_c, N_MOD, axis=-1)
        p = modulate(rms_norm(x, norm1_g[i]), sh1, sc1) @ w_in[i]
        pc = modulate(rms_norm(xc, norm1_g[i]), csh1, csc1) @ w_in[i]
        if i % 2 == 0:
            e = i // 2
            y, yc = even_layer_mixers(p, pc, rows, cols, a_qk_g[e], a_lambda[e], a_subln_g[e],
                                      b_qk_g[e], lambda_init_for(i), need_ctx)
        else:
            o = i // 2
            y, yc = odd_layer_mixers(p, pc, rows, cols, c_qk_g[o], c_sink[o], d_qk_g[o], d_rpb[o], need_ctx)
        x = x + g1 * (y @ w_out[i])
        x = x + g2 * conv_ffn(modulate(rms_norm(x, norm2_g[i]), sh2, sc2), w_up[i], conv_w[i], conv_b[i], w_down[i])
        if need_ctx:
            xc = xc + cg1 * (yc @ w_out[i])
            xc = xc + cg2 * conv_ffn(modulate(rms_norm(xc, norm2_g[i]), csh2, csc2),
                                     w_up[i], conv_w[i], conv_b[i], w_down[i])
    return x
```

```python
import functools
import math

import jax
import jax.numpy as jnp
from jax import lax
from jax.experimental import pallas as pl
from jax.experimental.pallas import tpu as pltpu

GRID_W = 64
HEAD_DIM = 128
A_SUB = HEAD_DIM // 2
ROPE_THETA = 10000.0
EPS = 1e-6
NEG_INF = -1e30
N_MOD = 6
C_WINDOW = 128
NA_KH = 8
NA_KW = 16
CONV_W = 3
GQA_GROUP = 4

LANES = 128
BF16_SUBLANES = 16
VMEM_LIMIT = 56 * 1024 * 1024

F32 = jnp.float32
BF16 = jnp.bfloat16


def _params(sem, vmem=VMEM_LIMIT):
    return pltpu.CompilerParams(dimension_semantics=sem, vmem_limit_bytes=vmem)


def _dot(a, b):
    return jnp.dot(a, b, preferred_element_type=F32)


def _dot_nt(a, b):
    return lax.dot_general(a, b, (((1,), (1,)), ((), ())), preferred_element_type=F32)


def _norm_mod(x, g, sh, sc):
    inv = lax.rsqrt(jnp.mean(x * x, axis=-1, keepdims=True) + EPS)
    return (x * inv * g) * (1.0 + sc) + sh


def _mod_kernel(cb_ref, w_ref, b_ref, o_ref):
    tn = w_ref.shape[2]
    for r in range(2):
        cv = cb_ref[r]
        a = cv * (1.0 / (1.0 + jnp.exp(-cv)))
        for ch in range(tn // LANES):
            sl = slice(ch * LANES, (ch + 1) * LANES)
            o_ref[0, r:r + 1, sl] = jnp.sum(w_ref[0, :, sl] * a, axis=0, keepdims=True) + b_ref[0, :, sl]


def _mod_call(c, c_ctx, w_ada, b_ada):
    depth, d, n = w_ada.shape
    tn = 1024
    cb = jnp.broadcast_to(jnp.stack([c, c_ctx])[:, :, None], (2, d, LANES))
    return pl.pallas_call(
        _mod_kernel,
        grid=(depth, n // tn),
        in_specs=[pl.BlockSpec((2, d, LANES), lambda i, j: (0, 0, 0)),
                  pl.BlockSpec((1, d, tn), lambda i, j: (i, 0, j)),
                  pl.BlockSpec((1, 1, tn), lambda i, j: (i, 0, j))],
        out_specs=pl.BlockSpec((1, 2, tn), lambda i, j: (i, 0, j)),
        out_shape=jax.ShapeDtypeStruct((depth, 2, n), F32),
        compiler_params=_params(("parallel", "parallel")),
        name="adaln_mod",
    )(cb, w_ada, b_ada.reshape(depth, 1, n))


EPI_CHUNK = 256


def _head_norm(p, gain, width):
    ss = p * p
    hi = ss.astype(BF16)
    lo = (ss - hi.astype(F32)).astype(BF16)
    shift = int(math.log2(width))
    r = lax.broadcasted_iota(jnp.int32, (EPI_CHUNK, EPI_CHUNK), 0) >> shift
    c = lax.broadcasted_iota(jnp.int32, (EPI_CHUNK, EPI_CHUNK), 1) >> shift
    bd = jnp.where(r == c, 1.0, 0.0).astype(BF16)
    seg = _dot(hi, bd) + _dot(lo, bd)
    return p * lax.rsqrt(seg * (1.0 / width) + EPS) * gain


def _rope(y, half, tabs):
    c_ref, s1_ref, s2_ref = tabs
    c, s1, s2 = c_ref[...], s1_ref[...], s2_ref[...]
    outs = []
    for k in range(y.shape[1] // LANES):
        yk = y[:, k * LANES:(k + 1) * LANES]
        outs.append(yk * c + pltpu.roll(yk, LANES - half, 1) * s1 + pltpu.roll(yk, half, 1) * s2)
    return jnp.concatenate(outs, axis=1)


def _inproj_kernel(*refs, tiles, rope):
    x_ref, g_ref, sh_ref, sc_ref, w_ref, gain_ref = refs[:6]
    if rope:
        tabs = {"A": refs[6:9], "B": refs[9:12]}
        o_ref, h_scr = refs[12:]
    else:
        tabs = None
        o_ref, h_scr = refs[6:]
    j = pl.program_id(1)

    @pl.when(j == 0)
    def _():
        h_scr[...] = _norm_mod(x_ref[...], g_ref[...], sh_ref[...], sc_ref[...]).astype(BF16)

    p = _dot(h_scr[...], w_ref[...])
    for lo, hi, chunk_classes in tiles:
        @pl.when(jnp.logical_and(j >= lo, j < hi))
        def _(chunk_classes=chunk_classes):
            for ci, (width, kind) in enumerate(chunk_classes):
                sl = slice(ci * EPI_CHUNK, (ci + 1) * EPI_CHUNK)
                y = p[:, sl]
                if width:
                    y = _head_norm(y, gain_ref[:, sl], width)
                if kind and rope:
                    y = _rope(y, 16 if kind == "A" else 32, tabs[kind])
                o_ref[:, sl] = y.astype(BF16)


def _tile_classes(classes, tn):
    per = tn // EPI_CHUNK
    tiles = [tuple(classes[t * per:(t + 1) * per]) for t in range(len(classes) // per)]
    out, start = [], 0
    for t in range(1, len(tiles) + 1):
        if t == len(tiles) or tiles[t] != tiles[start]:
            out.append((start, t, tiles[start]))
            start = t
    return tuple(out)


def _inproj_call(x, g, sh, sc, w, gain, classes, tabs, tm, name):
    m, d = x.shape
    n = w.shape[1]
    tn = 2 * EPI_CHUNK
    tiles = _tile_classes(classes, tn)
    rope = tabs is not None
    row = lambda i, j: (0, 0)
    in_specs = [pl.BlockSpec((tm, d), lambda i, j: (i, 0)),
                pl.BlockSpec((1, d), row), pl.BlockSpec((1, d), row), pl.BlockSpec((1, d), row),
                pl.BlockSpec((d, tn), lambda i, j: (0, j)),
                pl.BlockSpec((1, tn), lambda i, j: (0, j))]
    args = [x, g, sh, sc, w, gain]
    if rope:
        in_specs += [pl.BlockSpec((tm, LANES), lambda i, j: (i, 0))] * 6
        args += list(tabs)
    return pl.pallas_call(
        functools.partial(_inproj_kernel, tiles=tiles, rope=rope),
        grid=(m // tm, n // tn),
        in_specs=in_specs,
        out_specs=pl.BlockSpec((tm, tn), lambda i, j: (i, j)),
        out_shape=jax.ShapeDtypeStruct((m, n), BF16),
        scratch_shapes=[pltpu.VMEM((tm, d), BF16)],
        compiler_params=_params(("parallel", "arbitrary")),
        name=name,
    )(*args)


def _rope_tables(s, half):
    pos = jnp.arange(s, dtype=jnp.int32)
    freqs = ROPE_THETA ** (-jnp.arange(half, dtype=F32) / half)
    zeros = jnp.zeros((s, half), F32)
    cs, s1s, s2s = [], [], []
    for p in (pos // GRID_W, pos % GRID_W):
        ang = p.astype(F32)[:, None] * freqs[None, :]
        cos, sin = jnp.cos(ang), jnp.sin(ang)
        cs += [cos, cos]
        s1s += [-sin, zeros]
        s2s += [zeros, sin]
    reps = LANES // (4 * half)
    return tuple(jnp.tile(jnp.concatenate(t, axis=1), (1, reps)) for t in (cs, s1s, s2s))


def _flash_kernel(*refs, mode, tq, tk, n_lat, lam_init):
    q_ref, kc_ref, vc_ref = refs[:3]
    pos = 3
    if n_lat:
        k_ref, v_ref = refs[3:5]
        pos = 5
    if mode == "diff":
        lam_ref, sg_ref = refs[pos:pos + 2]
        pos += 2
    o_ref = refs[pos]

    if mode == "diff":
        q = q_ref[...].astype(F32)
        lane = lax.broadcasted_iota(jnp.int32, q.shape, 1)
        qs = jnp.concatenate([jnp.where(lane < A_SUB, q, 0.0), jnp.where(lane >= A_SUB, q, 0.0)],
                             axis=0).astype(BF16)
    else:
        qs = jnp.concatenate([q_ref[:, g * HEAD_DIM:(g + 1) * HEAD_DIM] for g in range(GQA_GROUP)], axis=0)
    rows = qs.shape[0]

    def step(k, v, carry):
        m, l, acc = carry
        s = _dot_nt(qs, k)
        m_new = jnp.maximum(m, jnp.max(s, axis=1, keepdims=True))
        p = jnp.exp(s - m_new)
        alpha = jnp.exp(m - m_new)
        l = alpha * l + jnp.sum(p, axis=1, keepdims=True)
        acc = alpha * acc + _dot(p.astype(BF16), v)
        return m_new, l, acc

    carry = (jnp.full((rows, 1), NEG_INF, F32), jnp.zeros((rows, 1), F32), jnp.zeros((rows, HEAD_DIM), F32))
    carry = step(kc_ref[...], vc_ref[...], carry)
    if n_lat:
        def body(c, carry):
            off = pl.multiple_of(c * tk, tk)
            return step(k_ref[pl.ds(off, tk), :], v_ref[pl.ds(off, tk), :], carry)
        carry = lax.fori_loop(0, n_lat, body, carry)
    _, l, acc = carry
    o = acc * (1.0 / l)
    if mode == "diff":
        lp = lam_ref[...]
        lam = (jnp.exp(jnp.sum(lp[0:1] * lp[1:2], axis=1, keepdims=True))
               - jnp.exp(jnp.sum(lp[2:3] * lp[3:4], axis=1, keepdims=True)) + lam_init)
        dlt = o[:tq] - lam * o[tq:]
        y = dlt * lax.rsqrt(jnp.mean(dlt * dlt, axis=-1, keepdims=True) + EPS) * sg_ref[...]
        o_ref[...] = (y * (1.0 - lam_init)).astype(BF16)
    else:
        for g in range(GQA_GROUP):
            o_ref[:, g * HEAD_DIM:(g + 1) * HEAD_DIM] = o[g * tq:(g + 1) * tq].astype(BF16)


def _flash_call(mode, qsrc, ksrc, pc, offs, n_heads, lam_p, subln, lam_init, tq, name):
    sq = qsrc.shape[0]
    t = pc.shape[0]
    qo, ko, vo = offs
    tk = 512
    qw = HEAD_DIM if mode == "diff" else GQA_GROUP * HEAD_DIM
    qb, kb, vb = qo // qw, ko // HEAD_DIM, vo // HEAD_DIM
    in_specs = [pl.BlockSpec((tq, qw), lambda h, i: (i, qb + h)),
                pl.BlockSpec((t, HEAD_DIM), lambda h, i: (0, kb + h)),
                pl.BlockSpec((t, HEAD_DIM), lambda h, i: (0, vb + h))]
    args = [qsrc, pc, pc]
    n_lat = 0
    if ksrc is not None:
        s = ksrc.shape[0]
        n_lat = s // tk
        in_specs += [pl.BlockSpec((s, HEAD_DIM), lambda h, i: (0, kb + h)),
                     pl.BlockSpec((s, HEAD_DIM), lambda h, i: (0, vb + h))]
        args += [ksrc, ksrc]
    if mode == "diff":
        in_specs += [pl.BlockSpec(lam_p.shape, lambda h, i: (0, 0)),
                     pl.BlockSpec((1, HEAD_DIM), lambda h, i: (0, 0))]
        args += [lam_p, subln.reshape(1, HEAD_DIM)]
    return pl.pallas_call(
        functools.partial(_flash_kernel, mode=mode, tq=tq, tk=tk, n_lat=n_lat, lam_init=lam_init),
        grid=(n_heads, sq // tq),
        in_specs=in_specs,
        out_specs=pl.BlockSpec((tq, qw), lambda h, i: (i, h)),
        out_shape=jax.ShapeDtypeStruct((sq, n_heads * qw), BF16),
        compiler_params=_params(("parallel", "parallel")),
        name=name,
    )(*args)


def _win_kernel(sink_ref, q_ref, k_ref, v_ref, kc_ref, vc_ref, o_ref, *, tq, s_len):
    kvh = pl.program_id(0)
    q0 = pl.program_id(1) * tq
    band = tq + 2 * C_WINDOW
    start = pl.multiple_of(jnp.clip(q0 - C_WINDOW, 0, s_len - band), C_WINDOW)
    kb = k_ref[pl.ds(start, band), :]
    vb = v_ref[pl.ds(start, band), :]
    kc, vc = kc_ref[...], vc_ref[...]
    qpos = q0 + lax.broadcasted_iota(jnp.int32, (tq, band), 0)
    kpos = start + lax.broadcasted_iota(jnp.int32, (tq, band), 1)
    valid = jnp.abs(qpos - kpos) <= C_WINDOW
    for g in range(GQA_GROUP):
        sl = slice(g * HEAD_DIM, (g + 1) * HEAD_DIM)
        q = q_ref[:, sl]
        s_lat = jnp.where(valid, _dot_nt(q, kb), NEG_INF)
        s_ctx = _dot_nt(q, kc)
        sink = sink_ref[kvh * GQA_GROUP + g]
        m = jnp.maximum(jnp.maximum(jnp.max(s_lat, axis=1, keepdims=True),
                                    jnp.max(s_ctx, axis=1, keepdims=True)), sink)
        p_lat = jnp.exp(s_lat - m)
        p_ctx = jnp.exp(s_ctx - m)
        l = (jnp.sum(p_lat, axis=1, keepdims=True) + jnp.sum(p_ctx, axis=1, keepdims=True)
             + jnp.exp(sink - m))
        o = _dot(p_ctx.astype(BF16), vc) + _dot(p_lat.astype(BF16), vb)
        o_ref[:, sl] = (o * (1.0 / l)).astype(BF16)


def _win_call(p, pc, offs, n_kv, sink, tq):
    s = p.shape[0]
    t = pc.shape[0]
    qo, ko, vo = offs
    qw = GQA_GROUP * HEAD_DIM
    qb, kb, vb = qo // qw, ko // HEAD_DIM, vo // HEAD_DIM
    return pl.pallas_call(
        functools.partial(_win_kernel, tq=tq, s_len=s),
        grid=(n_kv, s // tq),
        in_specs=[pl.BlockSpec(memory_space=pltpu.SMEM),
                  pl.BlockSpec((tq, qw), lambda h, i: (i, qb + h)),
                  pl.BlockSpec((s, HEAD_DIM), lambda h, i: (0, kb + h)),
                  pl.BlockSpec((s, HEAD_DIM), lambda h, i: (0, vb + h)),
                  pl.BlockSpec((t, HEAD_DIM), lambda h, i: (0, kb + h)),
                  pl.BlockSpec((t, HEAD_DIM), lambda h, i: (0, vb + h))],
        out_specs=pl.BlockSpec((tq, qw), lambda h, i: (i, h)),
        out_shape=jax.ShapeDtypeStruct((s, n_kv * qw), BF16),
        compiler_params=_params(("parallel", "parallel")),
        name="window_attn",
    )(sink, p, p, p, pc, pc)


NBR_Q_ROWS = 4
NBR_WIN_ROWS = NBR_Q_ROWS + NA_KH


def _nbr_kernel(q_ref, k_ref, v_ref, kc_ref, vc_ref, bias_ref, o_ref, *, tq, win, s_len):
    q0 = pl.program_id(1) * tq
    lead = (NA_KH // 2) * GRID_W
    start = pl.multiple_of(jnp.clip(q0 - lead, 0, s_len - win), lead)
    q = q_ref[...]
    s_win = _dot_nt(q, k_ref[pl.ds(start, win), :]) + bias_ref[0, 0]
    s_ctx = _dot_nt(q, kc_ref[...])
    m = jnp.maximum(jnp.max(s_win, axis=1, keepdims=True), jnp.max(s_ctx, axis=1, keepdims=True))
    p_win = jnp.exp(s_win - m)
    p_ctx = jnp.exp(s_ctx - m)
    l = jnp.sum(p_win, axis=1, keepdims=True) + jnp.sum(p_ctx, axis=1, keepdims=True)
    o = _dot(p_ctx.astype(BF16), vc_ref[...]) + _dot(p_win.astype(BF16), v_ref[pl.ds(start, win), :])
    o_ref[...] = (o * (1.0 / l)).astype(BF16)


def _nbr_bias(rpb, n_rows):
    tabs = []
    for r0 in (0, NBR_Q_ROWS, n_rows - NBR_Q_ROWS):
        ws = min(max(r0 - NA_KH // 2, 0), n_rows - NBR_WIN_ROWS)
        r = r0 + jnp.arange(NBR_Q_ROWS)[:, None, None, None]
        wq = jnp.arange(GRID_W)[None, :, None, None]
        kr = ws + jnp.arange(NBR_WIN_ROWS)[None, None, :, None]
        wk = jnp.arange(GRID_W)[None, None, None, :]
        r_start = jnp.clip(r - NA_KH // 2, 0, n_rows - NA_KH)
        c_start = jnp.clip(wq - NA_KW // 2, 0, GRID_W - NA_KW)
        valid = (kr >= r_start) & (kr < r_start + NA_KH) & (wk >= c_start) & (wk < c_start + NA_KW)
        r_off = jnp.clip(kr - r + (NA_KH - 1), 0, 2 * NA_KH - 2)
        c_off = jnp.clip(wk - wq, -(NA_KW - 1), NA_KW - 1) + (NA_KW - 1)
        shape = (NBR_Q_ROWS, GRID_W, NBR_WIN_ROWS, GRID_W)
        r_off, c_off, valid = (jnp.broadcast_to(a, shape) for a in (r_off, c_off, valid))
        b = jnp.where(valid[None], rpb.astype(F32)[:, r_off, c_off], NEG_INF)
        tabs.append(b.reshape(rpb.shape[0], NBR_Q_ROWS * GRID_W, NBR_WIN_ROWS * GRID_W))
    return jnp.stack(tabs)


def _nbr_call(p, pc, offs, n_heads, rpb):
    s = p.shape[0]
    t = pc.shape[0]
    tq = NBR_Q_ROWS * GRID_W
    win = NBR_WIN_ROWS * GRID_W
    nq = s // tq
    qb, kb, vb = (o // HEAD_DIM for o in offs)
    bias = _nbr_bias(rpb, s // GRID_W)

    def bias_map(h, i):
        return (jnp.where(i == 0, 0, jnp.where(i == nq - 1, 2, 1)), h, 0, 0)

    return pl.pallas_call(
        functools.partial(_nbr_kernel, tq=tq, win=win, s_len=s),
        grid=(n_heads, nq),
        in_specs=[pl.BlockSpec((tq, HEAD_DIM), lambda h, i: (i, qb + h)),
                  pl.BlockSpec((s, HEAD_DIM), lambda h, i: (0, kb + h)),
                  pl.BlockSpec((s, HEAD_DIM), lambda h, i: (0, vb + h)),
                  pl.BlockSpec((t, HEAD_DIM), lambda h, i: (0, kb + h)),
                  pl.BlockSpec((t, HEAD_DIM), lambda h, i: (0, vb + h)),
                  pl.BlockSpec((1, 1, tq, win), bias_map)],
        out_specs=pl.BlockSpec((tq, HEAD_DIM), lambda h, i: (i, h)),
        out_shape=jax.ShapeDtypeStruct((s, n_heads * HEAD_DIM), BF16),
        compiler_params=_params(("parallel", "parallel")),
        name="nbr_attn",
    )(p, p, p, pc, pc, bias)


def _outproj_kernel(x_ref, ya_ref, yb_ref, wa_ref, wb_ref, g_ref, o_ref):
    acc = _dot(ya_ref[...], wa_ref[...]) + _dot(yb_ref[...], wb_ref[...])
    o_ref[...] = x_ref[...] + g_ref[...] * acc


def _outproj_call(x, ya, yb, wa, wb, gate, tm, name):
    m, d = x.shape
    ka, kb = ya.shape[1], yb.shape[1]
    return pl.pallas_call(
        _outproj_kernel,
        grid=(m // tm,),
        in_specs=[pl.BlockSpec((tm, d), lambda i: (i, 0)),
                  pl.BlockSpec((tm, ka), lambda i: (i, 0)),
                  pl.BlockSpec((tm, kb), lambda i: (i, 0)),
                  pl.BlockSpec((ka, d), lambda i: (0, 0)),
                  pl.BlockSpec((kb, d), lambda i: (0, 0)),
                  pl.BlockSpec((1, d), lambda i: (0, 0))],
        out_specs=pl.BlockSpec((tm, d), lambda i: (i, 0)),
        out_shape=jax.ShapeDtypeStruct((m, d), F32),
        compiler_params=_params(("parallel",)),
        name=name,
    )(x, ya, yb, wa, wb, gate)


HALO = BF16_SUBLANES


def _ffn_kernel(xm_ref, xp_ref, xn_ref, gn_ref, sh_ref, sc_ref, gate_ref, wa_ref, wg_ref,
                cwa_ref, cwg_ref, cba_ref, cbg_ref, wd_ref, o_ref, h_scr, acc_scr, ua_scr, ug_scr):
    i, j = pl.program_id(0), pl.program_id(1)
    ni, nj = pl.num_programs(0), pl.num_programs(1)
    tm = xm_ref.shape[0]

    @pl.when(j == 0)
    def _():
        nm = lambda x: _norm_mod(x, gn_ref[...], sh_ref[...], sc_ref[...])
        h_scr[0:HALO, :] = jnp.where(i > 0, nm(xp_ref[...]), 0.0).astype(BF16)
        h_scr[HALO:HALO + tm, :] = nm(xm_ref[...]).astype(BF16)
        h_scr[HALO + tm:, :] = jnp.where(i < ni - 1, nm(xn_ref[...]), 0.0).astype(BF16)
        acc_scr[...] = jnp.zeros_like(acc_scr)

    h = h_scr[...]
    ua_scr[...] = _dot(h, wa_ref[...])
    ug_scr[...] = _dot(h, wg_ref[...])

    def conv(u_scr, cw_ref, cb_ref):
        out = cb_ref[...] + u_scr[pl.ds(HALO - 1, tm), :] * cw_ref[0:1, :]
        for t in range(1, CONV_W):
            out = out + u_scr[pl.ds(HALO - 1 + t, tm), :] * cw_ref[t:t + 1, :]
        return out

    a = conv(ua_scr, cwa_ref, cba_ref)
    g = conv(ug_scr, cwg_ref, cbg_ref)
    z = (g * (1.0 / (1.0 + jnp.exp(-g)))) * a
    acc_scr[...] += _dot(z.astype(BF16), wd_ref[...])

    @pl.when(j == nj - 1)
    def _():
        o_ref[...] = xm_ref[...] + gate_ref[...] * acc_scr[...]


def _ffn_call(x, gn, sh, sc, gate, w_up, conv_w, conv_b, w_down, tm, tf, name):
    m, d = x.shape
    f = w_down.shape[0]
    nfb = f // tf
    hb = tm // HALO
    last = m // HALO - 1
    row = lambda i, j: (0, 0)
    return pl.pallas_call(
        _ffn_kernel,
        grid=(m // tm, nfb),
        in_specs=[pl.BlockSpec((tm, d), lambda i, j: (i, 0)),
                  pl.BlockSpec((HALO, d), lambda i, j: (jnp.maximum(i * hb - 1, 0), 0)),
                  pl.BlockSpec((HALO, d), lambda i, j: (jnp.minimum((i + 1) * hb, last), 0)),
                  pl.BlockSpec((1, d), row), pl.BlockSpec((1, d), row), pl.BlockSpec((1, d), row),
                  pl.BlockSpec((1, d), row),
                  pl.BlockSpec((d, tf), lambda i, j: (0, j)),
                  pl.BlockSpec((d, tf), lambda i, j: (0, nfb + j)),
                  pl.BlockSpec((CONV_W, tf), lambda i, j: (0, j)),
                  pl.BlockSpec((CONV_W, tf), lambda i, j: (0, nfb + j)),
                  pl.BlockSpec((1, tf), lambda i, j: (0, j)),
                  pl.BlockSpec((1, tf), lambda i, j: (0, nfb + j)),
                  pl.BlockSpec((tf, d), lambda i, j: (j, 0))],
        out_specs=pl.BlockSpec((tm, d), lambda i, j: (i, 0)),
        out_shape=jax.ShapeDtypeStruct((m, d), F32),
        scratch_shapes=[pltpu.VMEM((tm + 2 * HALO, d), BF16),
                        pltpu.VMEM((tm, d), F32),
                        pltpu.VMEM((tm + 2 * HALO, tf), F32),
                        pltpu.VMEM((tm + 2 * HALO, tf), F32)],
        compiler_params=_params(("parallel", "arbitrary")),
        name=name,
    )(x, x, x, gn, sh, sc, gate, w_up, w_up, conv_w, conv_w, conv_b, conv_b, w_down)


def _offsets(sizes):
    offs, o = [], 0
    for s in sizes:
        offs.append(o)
        o += s
    return offs


def _chunk_classes(sizes, classes):
    out = []
    for s, c in zip(sizes, classes):
        out += [c] * (s // EPI_CHUNK)
    return out


def _gain_vector(sizes, gains):
    parts = []
    for s, g in zip(sizes, gains):
        parts.append(jnp.ones((s,), F32) if g is None else jnp.tile(g.astype(F32), s // g.shape[0]))
    return jnp.concatenate(parts).reshape(1, -1)


def _lambda_init(layer_idx):
    return 0.8 - 0.6 * math.exp(-0.3 * layer_idx)


def _forward(x, c, ctx, c_ctx, w_ada, b_ada, norm1_g, w_in, w_out, a_qk_g, a_lambda, a_subln_g,
             b_qk_g, c_qk_g, c_sink, d_qk_g, d_rpb, norm2_g, w_up, conv_w, conv_b, w_down):
    s, d = x.shape
    t = ctx.shape[0]
    depth = w_ada.shape[0]
    n_heads = d // (2 * HEAD_DIM)
    n_kv = n_heads // GQA_GROUP
    hw = n_heads * HEAD_DIM
    kvw = n_kv * HEAD_DIM
    even_sizes = (hw, hw, hw, hw, kvw, kvw)
    odd_sizes = (hw, kvw, kvw, hw, hw, hw)
    tm = 512
    tf = 512

    mod = _mod_call(c, c_ctx, w_ada, b_ada)
    rope_a = _rope_tables(s, 16)
    rope_b = _rope_tables(s, 32)
    tabs = rope_a + rope_b
    scale_a = A_SUB ** -0.5
    scale = HEAD_DIM ** -0.5

    xc = ctx
    for i in range(depth):
        need_ctx = i < depth - 1
        e = i // 2
        vec = lambda r, k: mod[i, r:r + 1, k * d:(k + 1) * d]
        w_in_i = w_in[i].astype(BF16)
        w_out_i = w_out[i].astype(BF16)
        w_up_i = w_up[i].astype(BF16)
        w_down_i = w_down[i].astype(BF16)
        g1n = norm1_g[i].reshape(1, d)
        g2n = norm2_g[i].reshape(1, d)
        if i % 2 == 0:
            sizes = even_sizes
            classes = [(A_SUB, "A"), (A_SUB, "A"), (0, None), (HEAD_DIM, "B"), (HEAD_DIM, "B"), (0, None)]
            gains = [a_qk_g[e, 0] * scale_a, a_qk_g[e, 1], None, b_qk_g[e, 0] * scale, b_qk_g[e, 1], None]
        else:
            sizes = odd_sizes
            classes = [(HEAD_DIM, "B"), (HEAD_DIM, "B"), (0, None), (HEAD_DIM, None), (HEAD_DIM, None), (0, None)]
            gains = [c_qk_g[e, 0] * scale, c_qk_g[e, 1], None, d_qk_g[e, 0] * scale, d_qk_g[e, 1], None]
        offs = _offsets(sizes)
        chunk_cls = _chunk_classes(sizes, classes)
        gain = _gain_vector(sizes, gains)

        p = _inproj_call(x, g1n, vec(0, 0), vec(0, 1), w_in_i, gain, chunk_cls, tabs, tm, f"inproj{i}")
        pc = _inproj_call(xc, g1n, vec(1, 0), vec(1, 1), w_in_i, gain, chunk_cls, None, t, f"inproj_ctx{i}")

        if i % 2 == 0:
            lam_init = _lambda_init(i)
            fa = functools.partial(_flash_call, "diff", offs=offs[0:3], n_heads=n_heads, lam_p=a_lambda[e],
                                   subln=a_subln_g[e], lam_init=lam_init, tq=256)
            fb = functools.partial(_flash_call, "gqa", offs=offs[3:6], n_heads=n_kv, lam_p=None,
                                   subln=None, lam_init=None)
            ya = fa(p, p, pc, name=f"diff_attn{i}")
            yb = fb(p, p, pc, tq=128, name=f"gqa_attn{i}")
            if need_ctx:
                ya_c = fa(pc, None, pc, name=f"diff_attn_ctx{i}")
                yb_c = fb(pc, None, pc, tq=t, name=f"gqa_attn_ctx{i}")
        else:
            ya = _win_call(p, pc, offs[0:3], n_kv, c_sink[e], 256)
            yb = _nbr_call(p, pc, offs[3:6], n_heads, d_rpb[e])
            if need_ctx:
                raise NotImplementedError("context stream after an odd layer")

        wo_a, wo_b = w_out_i[:hw], w_out_i[hw:]
        x = _outproj_call(x, ya, yb, wo_a, wo_b, vec(0, 2), tm, f"outproj{i}")
        x = _ffn_call(x, g2n, vec(0, 3), vec(0, 4), vec(0, 5), w_up_i, conv_w[i], conv_b[i].reshape(1, -1),
                      w_down_i, tm, tf, f"ffn{i}")
        if need_ctx:
            xc = _outproj_call(xc, ya_c, yb_c, wo_a, wo_b, vec(1, 2), t, f"outproj_ctx{i}")
            xc = _ffn_call(xc, g2n, vec(1, 3), vec(1, 4), vec(1, 5), w_up_i, conv_w[i], conv_b[i].reshape(1, -1),
                           w_down_i, t, tf, f"ffn_ctx{i}")
    return x


def kernel(x, c, ctx, c_ctx, w_ada, b_ada, norm1_g, w_in, w_out, a_qk_g, a_lambda, a_subln_g, b_qk_g, c_qk_g,
           c_sink, d_qk_g, d_rpb, norm2_g, w_up, conv_w, conv_b, w_down):
    outs = [_forward(x[b], c[b], ctx[b], c_ctx, w_ada, b_ada, norm1_g, w_in, w_out, a_qk_g, a_lambda, a_subln_g,
                     b_qk_g, c_qk_g, c_sink, d_qk_g, d_rpb, norm2_g, w_up, conv_w, conv_b, w_down)
            for b in range(x.shape[0])]
    return jnp.stack(outs)
```

```python
import functools
import math

import jax
import jax.numpy as jnp
from jax import lax
from jax.experimental import pallas as pl
from jax.experimental.pallas import tpu as pltpu

GRID_W = 64
HEAD_DIM = 128
A_SUB = HEAD_DIM // 2
ROPE_THETA = 10000.0
EPS = 1e-6
NEG_INF = -1e30
N_MOD = 6
C_WINDOW = 128
NA_KH = 8
NA_KW = 16
CONV_W = 3
GQA_GROUP = 4

LANES = 128
BF16_SUBLANES = 16
VMEM_LIMIT = 56 * 1024 * 1024

F32 = jnp.float32
BF16 = jnp.bfloat16


def _params(sem, vmem=VMEM_LIMIT):
    return pltpu.CompilerParams(dimension_semantics=sem, vmem_limit_bytes=vmem)


def _dot(a, b):
    return jnp.dot(a, b, preferred_element_type=F32)


def _dot_nt(a, b):
    return lax.dot_general(a, b, (((1,), (1,)), ((), ())), preferred_element_type=F32)


def _norm_mod(x, g, sh, sc):
    inv = lax.rsqrt(jnp.mean(x * x, axis=-1, keepdims=True) + EPS)
    return (x * inv * g) * (1.0 + sc) + sh


def _mod_kernel(cb_ref, w_ref, b_ref, o_ref):
    tn = w_ref.shape[2]
    for r in range(2):
        cv = cb_ref[r]
        a = cv * (1.0 / (1.0 + jnp.exp(-cv)))
        for ch in range(tn // LANES):
            sl = slice(ch * LANES, (ch + 1) * LANES)
            o_ref[0, r:r + 1, sl] = jnp.sum(w_ref[0, :, sl] * a, axis=0, keepdims=True) + b_ref[0, :, sl]


def _mod_call(c, c_ctx, w_ada, b_ada):
    depth, d, n = w_ada.shape
    tn = 1024
    cb = jnp.broadcast_to(jnp.stack([c, c_ctx])[:, :, None], (2, d, LANES))
    return pl.pallas_call(
        _mod_kernel,
        grid=(depth, n // tn),
        in_specs=[pl.BlockSpec((2, d, LANES), lambda i, j: (0, 0, 0)),
                  pl.BlockSpec((1, d, tn), lambda i, j: (i, 0, j)),
                  pl.BlockSpec((1, 1, tn), lambda i, j: (i, 0, j))],
        out_specs=pl.BlockSpec((1, 2, tn), lambda i, j: (i, 0, j)),
        out_shape=jax.ShapeDtypeStruct((depth, 2, n), F32),
        compiler_params=_params(("parallel", "parallel")),
        name="adaln_mod",
    )(cb, w_ada, b_ada.reshape(depth, 1, n))


EPI_CHUNK = 256


def _head_norm(p, gain, width):
    ss = p * p
    hi = ss.astype(BF16)
    lo = (ss - hi.astype(F32)).astype(BF16)
    shift = int(math.log2(width))
    r = lax.broadcasted_iota(jnp.int32, (EPI_CHUNK, EPI_CHUNK), 0) >> shift
    c = lax.broadcasted_iota(jnp.int32, (EPI_CHUNK, EPI_CHUNK), 1) >> shift
    bd = jnp.where(r == c, 1.0, 0.0).astype(BF16)
    seg = _dot(hi, bd) + _dot(lo, bd)
    return p * lax.rsqrt(seg * (1.0 / width) + EPS) * gain


def _rope(y, half, tabs):
    c_ref, s1_ref, s2_ref = tabs
    c, s1, s2 = c_ref[...], s1_ref[...], s2_ref[...]
    outs = []
    for k in range(y.shape[1] // LANES):
        yk = y[:, k * LANES:(k + 1) * LANES]
        outs.append(yk * c + pltpu.roll(yk, LANES - half, 1) * s1 + pltpu.roll(yk, half, 1) * s2)
    return jnp.concatenate(outs, axis=1)


def _inproj_kernel(*refs, tiles, rope):
    x_ref, g_ref, sh_ref, sc_ref, w_ref, gain_ref = refs[:6]
    if rope:
        tabs = {"A": refs[6:9], "B": refs[9:12]}
        o_ref, h_scr = refs[12:]
    else:
        tabs = None
        o_ref, h_scr = refs[6:]
    j = pl.program_id(1)

    @pl.when(j == 0)
    def _():
        h_scr[...] = _norm_mod(x_ref[...], g_ref[...], sh_ref[...], sc_ref[...]).astype(BF16)

    p = _dot(h_scr[...], w_ref[...])
    for lo, hi, chunk_classes in tiles:
        @pl.when(jnp.logical_and(j >= lo, j < hi))
        def _(chunk_classes=chunk_classes):
            for ci, (width, kind) in enumerate(chunk_classes):
                sl = slice(ci * EPI_CHUNK, (ci + 1) * EPI_CHUNK)
                y = p[:, sl]
                if width:
                    y = _head_norm(y, gain_ref[:, sl], width)
                if kind and rope:
                    y = _rope(y, 16 if kind == "A" else 32, tabs[kind])
                o_ref[:, sl] = y.astype(BF16)


def _tile_classes(classes, tn):
    per = tn // EPI_CHUNK
    tiles = [tuple(classes[t * per:(t + 1) * per]) for t in range(len(classes) // per)]
    out, start = [], 0
    for t in range(1, len(tiles) + 1):
        if t == len(tiles) or tiles[t] != tiles[start]:
            out.append((start, t, tiles[start]))
            start = t
    return tuple(out)


def _inproj_call(x, g, sh, sc, w, gain, classes, tabs, tm, name):
    m, d = x.shape
    n = w.shape[1]
    tn = 2 * EPI_CHUNK
    tiles = _tile_classes(classes, tn)
    rope = tabs is not None
    row = lambda i, j: (0, 0)
    in_specs = [pl.BlockSpec((tm, d), lambda i, j: (i, 0)),
                pl.BlockSpec((1, d), row), pl.BlockSpec((1, d), row), pl.BlockSpec((1, d), row),
                pl.BlockSpec((d, tn), lambda i, j: (0, j)),
                pl.BlockSpec((1, tn), lambda i, j: (0, j))]
    args = [x, g, sh, sc, w, gain]
    if rope:
        in_specs += [pl.BlockSpec((tm, LANES), lambda i, j: (i, 0))] * 6
        args += list(tabs)
    return pl.pallas_call(
        functools.partial(_inproj_kernel, tiles=tiles, rope=rope),
        grid=(m // tm, n // tn),
        in_specs=in_specs,
        out_specs=pl.BlockSpec((tm, tn), lambda i, j: (i, j)),
        out_shape=jax.ShapeDtypeStruct((m, n), BF16),
        scratch_shapes=[pltpu.VMEM((tm, d), BF16)],
        compiler_params=_params(("parallel", "arbitrary")),
        name=name,
    )(*args)


def _rope_tables(s, half):
    pos = jnp.arange(s, dtype=jnp.int32)
    freqs = ROPE_THETA ** (-jnp.arange(half, dtype=F32) / half)
    zeros = jnp.zeros((s, half), F32)
    cs, s1s, s2s = [], [], []
    for p in (pos // GRID_W, pos % GRID_W):
        ang = p.astype(F32)[:, None] * freqs[None, :]
        cos, sin = jnp.cos(ang), jnp.sin(ang)
        cs += [cos, cos]
        s1s += [-sin, zeros]
        s2s += [zeros, sin]
    reps = LANES // (4 * half)
    return tuple(jnp.tile(jnp.concatenate(t, axis=1), (1, reps)) for t in (cs, s1s, s2s))


def _flash_kernel(*refs, mode, tq, tk, n_lat, lam_init):
    q_ref, kc_ref, vc_ref = refs[:3]
    pos = 3
    if n_lat:
        k_ref, v_ref = refs[3:5]
        pos = 5
    if mode == "diff":
        lam_ref, sg_ref = refs[pos:pos + 2]
        pos += 2
    o_ref = refs[pos]

    if mode == "diff":
        q = q_ref[...].astype(F32)
        lane = lax.broadcasted_iota(jnp.int32, q.shape, 1)
        qs = jnp.concatenate([jnp.where(lane < A_SUB, q, 0.0), jnp.where(lane >= A_SUB, q, 0.0)],
                             axis=0).astype(BF16)
    else:
        qs = jnp.concatenate([q_ref[:, g * HEAD_DIM:(g + 1) * HEAD_DIM] for g in range(GQA_GROUP)], axis=0)
    rows = qs.shape[0]

    def step(k, v, carry):
        m, l, acc = carry
        s = _dot_nt(k, qs)
        m_new = jnp.maximum(m, jnp.max(s, axis=0, keepdims=True))
        p = jnp.exp(s - m_new)
        alpha = jnp.exp(m - m_new)
        l = alpha * l + jnp.sum(p, axis=0, keepdims=True)
        pv = lax.dot_general(v, p.astype(BF16), (((0,), (0,)), ((), ())), preferred_element_type=F32)
        return m_new, l, alpha * acc + pv

    carry = (jnp.full((1, rows), NEG_INF, F32), jnp.zeros((1, rows), F32), jnp.zeros((HEAD_DIM, rows), F32))
    carry = step(kc_ref[...], vc_ref[...], carry)
    if n_lat:
        def body(c, carry):
            off = pl.multiple_of(c * tk, tk)
            return step(k_ref[pl.ds(off, tk), :], v_ref[pl.ds(off, tk), :], carry)
        carry = lax.fori_loop(0, n_lat, body, carry, unroll=2)
    _, l, acc = carry
    o = (acc * (1.0 / l)).T
    if mode == "diff":
        lp = lam_ref[...]
        lam = (jnp.exp(jnp.sum(lp[0:1] * lp[1:2], axis=1, keepdims=True))
               - jnp.exp(jnp.sum(lp[2:3] * lp[3:4], axis=1, keepdims=True)) + lam_init)
        dlt = o[:tq] - lam * o[tq:]
        y = dlt * lax.rsqrt(jnp.mean(dlt * dlt, axis=-1, keepdims=True) + EPS) * sg_ref[...]
        o_ref[...] = (y * (1.0 - lam_init)).astype(BF16)
    else:
        for g in range(GQA_GROUP):
            o_ref[:, g * HEAD_DIM:(g + 1) * HEAD_DIM] = o[g * tq:(g + 1) * tq].astype(BF16)


def _flash_call(mode, qsrc, ksrc, pc, offs, n_heads, lam_p, subln, lam_init, tq, name):
    sq = qsrc.shape[0]
    t = pc.shape[0]
    qo, ko, vo = offs
    tk = 512
    qw = HEAD_DIM if mode == "diff" else GQA_GROUP * HEAD_DIM
    qb, kb, vb = qo // qw, ko // HEAD_DIM, vo // HEAD_DIM
    in_specs = [pl.BlockSpec((tq, qw), lambda h, i: (i, qb + h)),
                pl.BlockSpec((t, HEAD_DIM), lambda h, i: (0, kb + h)),
                pl.BlockSpec((t, HEAD_DIM), lambda h, i: (0, vb + h))]
    args = [qsrc, pc, pc]
    n_lat = 0
    if ksrc is not None:
        s = ksrc.shape[0]
        n_lat = s // tk
        in_specs += [pl.BlockSpec((s, HEAD_DIM), lambda h, i: (0, kb + h)),
                     pl.BlockSpec((s, HEAD_DIM), lambda h, i: (0, vb + h))]
        args += [ksrc, ksrc]
    if mode == "diff":
        in_specs += [pl.BlockSpec(lam_p.shape, lambda h, i: (0, 0)),
                     pl.BlockSpec((1, HEAD_DIM), lambda h, i: (0, 0))]
        args += [lam_p, subln.reshape(1, HEAD_DIM)]
    return pl.pallas_call(
        functools.partial(_flash_kernel, mode=mode, tq=tq, tk=tk, n_lat=n_lat, lam_init=lam_init),
        grid=(n_heads, sq // tq),
        in_specs=in_specs,
        out_specs=pl.BlockSpec((tq, qw), lambda h, i: (i, h)),
        out_shape=jax.ShapeDtypeStruct((sq, n_heads * qw), BF16),
        compiler_params=_params(("parallel", "parallel")),
        name=name,
    )(*args)


def _win_kernel(sink_ref, q_ref, k_ref, v_ref, kc_ref, vc_ref, o_ref, *, tq, s_len):
    kvh = pl.program_id(0)
    q0 = pl.program_id(1) * tq
    band = tq + 2 * C_WINDOW
    start = pl.multiple_of(jnp.clip(q0 - C_WINDOW, 0, s_len - band), C_WINDOW)
    kb = k_ref[pl.ds(start, band), :]
    vb = v_ref[pl.ds(start, band), :]
    kc, vc = kc_ref[...], vc_ref[...]
    qpos = q0 + lax.broadcasted_iota(jnp.int32, (tq, band), 0)
    kpos = start + lax.broadcasted_iota(jnp.int32, (tq, band), 1)
    valid = jnp.abs(qpos - kpos) <= C_WINDOW
    for g in range(GQA_GROUP):
        sl = slice(g * HEAD_DIM, (g + 1) * HEAD_DIM)
        q = q_ref[:, sl]
        s_lat = jnp.where(valid, _dot_nt(q, kb), NEG_INF)
        s_ctx = _dot_nt(q, kc)
        sink = sink_ref[kvh * GQA_GROUP + g]
        m = jnp.maximum(jnp.maximum(jnp.max(s_lat, axis=1, keepdims=True),
                                    jnp.max(s_ctx, axis=1, keepdims=True)), sink)
        p_lat = jnp.exp(s_lat - m)
        p_ctx = jnp.exp(s_ctx - m)
        l = (jnp.sum(p_lat, axis=1, keepdims=True) + jnp.sum(p_ctx, axis=1, keepdims=True)
             + jnp.exp(sink - m))
        o = _dot(p_ctx.astype(BF16), vc) + _dot(p_lat.astype(BF16), vb)
        o_ref[:, sl] = (o * (1.0 / l)).astype(BF16)


def _win_call(p, pc, offs, n_kv, sink, tq):
    s = p.shape[0]
    t = pc.shape[0]
    qo, ko, vo = offs
    qw = GQA_GROUP * HEAD_DIM
    qb, kb, vb = qo // qw, ko // HEAD_DIM, vo // HEAD_DIM
    return pl.pallas_call(
        functools.partial(_win_kernel, tq=tq, s_len=s),
        grid=(n_kv, s // tq),
        in_specs=[pl.BlockSpec(memory_space=pltpu.SMEM),
                  pl.BlockSpec((tq, qw), lambda h, i: (i, qb + h)),
                  pl.BlockSpec((s, HEAD_DIM), lambda h, i: (0, kb + h)),
                  pl.BlockSpec((s, HEAD_DIM), lambda h, i: (0, vb + h)),
                  pl.BlockSpec((t, HEAD_DIM), lambda h, i: (0, kb + h)),
                  pl.BlockSpec((t, HEAD_DIM), lambda h, i: (0, vb + h))],
        out_specs=pl.BlockSpec((tq, qw), lambda h, i: (i, h)),
        out_shape=jax.ShapeDtypeStruct((s, n_kv * qw), BF16),
        compiler_params=_params(("parallel", "parallel")),
        name="window_attn",
    )(sink, p, p, p, pc, pc)


NBR_Q_ROWS = 4
NBR_WIN_ROWS = NBR_Q_ROWS + NA_KH


def _nbr_build_bias(rpb_ref, head, mats_scr, tab_scr, n_rows):
    na, nb = 2 * NA_KH - 1, 2 * NA_KW - 1
    shape = (GRID_W, 2 * GRID_W)
    wq = lax.broadcasted_iota(jnp.int32, shape, 0)
    lane = lax.broadcasted_iota(jnp.int32, shape, 1)
    wk = lane & (GRID_W - 1)
    dc = jnp.clip(wk - wq, -(NA_KW - 1), NA_KW - 1) + (NA_KW - 1)
    c_start = jnp.clip(wq - NA_KW // 2, 0, GRID_W - NA_KW)
    col_ok = jnp.logical_and(wk >= c_start, wk < c_start + NA_KW)

    def build(a, carry):
        base = (head * na + a) * nb
        mat = jnp.zeros(shape, F32)
        for b in range(nb):
            mat = jnp.where(dc == b, rpb_ref[base + b], mat)
        mats_scr[a] = jnp.where(col_ok, mat, NEG_INF)
        return carry

    lax.fori_loop(0, na, build, 0)
    neg = jnp.full(shape, NEG_INF, F32)
    first_half = lane < GRID_W
    for cl, r0 in enumerate((0, NBR_Q_ROWS, n_rows - NBR_Q_ROWS)):
        ws = min(max(r0 - NA_KH // 2, 0), n_rows - NBR_WIN_ROWS)
        for qr in range(NBR_Q_ROWS):
            r = r0 + qr
            r_start = min(max(r - NA_KH // 2, 0), n_rows - NA_KH)
            for kp in range(NBR_WIN_ROWS // 2):
                halves = []
                for kr in (ws + 2 * kp, ws + 2 * kp + 1):
                    inside = r_start <= kr < r_start + NA_KH
                    halves.append(mats_scr[kr - r + NA_KH - 1] if inside else neg)
                tab_scr[cl, qr * GRID_W:(qr + 1) * GRID_W, kp * 2 * GRID_W:(kp + 1) * 2 * GRID_W] = (
                    jnp.where(first_half, halves[0], halves[1]))


def _nbr_kernel(rpb_ref, q_ref, k_ref, v_ref, kc_ref, vc_ref, o_ref, mats_scr, tab_scr, *, tq, win, s_len):
    i = pl.program_id(1)
    nq = pl.num_programs(1)

    @pl.when(i == 0)
    def _():
        _nbr_build_bias(rpb_ref, pl.program_id(0), mats_scr, tab_scr, s_len // GRID_W)

    q0 = i * tq
    lead = (NA_KH // 2) * GRID_W
    start = pl.multiple_of(jnp.clip(q0 - lead, 0, s_len - win), lead)
    q = q_ref[...]
    cls = jnp.where(i == 0, 0, jnp.where(i == nq - 1, 2, 1))
    s_win = _dot_nt(q, k_ref[pl.ds(start, win), :]) + tab_scr[cls]
    s_ctx = _dot_nt(q, kc_ref[...])
    m = jnp.maximum(jnp.max(s_win, axis=1, keepdims=True), jnp.max(s_ctx, axis=1, keepdims=True))
    p_win = jnp.exp(s_win - m)
    p_ctx = jnp.exp(s_ctx - m)
    l = jnp.sum(p_win, axis=1, keepdims=True) + jnp.sum(p_ctx, axis=1, keepdims=True)
    o = _dot(p_ctx.astype(BF16), vc_ref[...]) + _dot(p_win.astype(BF16), v_ref[pl.ds(start, win), :])
    o_ref[...] = (o * (1.0 / l)).astype(BF16)


def _nbr_call(p, pc, offs, n_heads, rpb):
    s = p.shape[0]
    t = pc.shape[0]
    tq = NBR_Q_ROWS * GRID_W
    win = NBR_WIN_ROWS * GRID_W
    qb, kb, vb = (o // HEAD_DIM for o in offs)
    assert s // GRID_W >= NBR_WIN_ROWS and rpb.shape[1:] == (2 * NA_KH - 1, 2 * NA_KW - 1)
    return pl.pallas_call(
        functools.partial(_nbr_kernel, tq=tq, win=win, s_len=s),
        grid=(n_heads, s // tq),
        in_specs=[pl.BlockSpec(memory_space=pltpu.SMEM),
                  pl.BlockSpec((tq, HEAD_DIM), lambda h, i: (i, qb + h)),
                  pl.BlockSpec((s, HEAD_DIM), lambda h, i: (0, kb + h)),
                  pl.BlockSpec((s, HEAD_DIM), lambda h, i: (0, vb + h)),
                  pl.BlockSpec((t, HEAD_DIM), lambda h, i: (0, kb + h)),
                  pl.BlockSpec((t, HEAD_DIM), lambda h, i: (0, vb + h))],
        out_specs=pl.BlockSpec((tq, HEAD_DIM), lambda h, i: (i, h)),
        out_shape=jax.ShapeDtypeStruct((s, n_heads * HEAD_DIM), BF16),
        scratch_shapes=[pltpu.VMEM((2 * NA_KH - 1, GRID_W, 2 * GRID_W), F32),
                        pltpu.VMEM((3, tq, win), F32)],
        compiler_params=_params(("parallel", "arbitrary")),
        name="nbr_attn",
    )(rpb.astype(F32).reshape(-1), p, p, p, pc, pc)


def _outproj_kernel(x_ref, ya_ref, yb_ref, wa_ref, wb_ref, g_ref, o_ref):
    acc = _dot(ya_ref[...], wa_ref[...]) + _dot(yb_ref[...], wb_ref[...])
    o_ref[...] = x_ref[...] + g_ref[...] * acc


def _outproj_call(x, ya, yb, wa, wb, gate, tm, name):
    m, d = x.shape
    ka, kb = ya.shape[1], yb.shape[1]
    return pl.pallas_call(
        _outproj_kernel,
        grid=(m // tm,),
        in_specs=[pl.BlockSpec((tm, d), lambda i: (i, 0)),
                  pl.BlockSpec((tm, ka), lambda i: (i, 0)),
                  pl.BlockSpec((tm, kb), lambda i: (i, 0)),
                  pl.BlockSpec((ka, d), lambda i: (0, 0)),
                  pl.BlockSpec((kb, d), lambda i: (0, 0)),
                  pl.BlockSpec((1, d), lambda i: (0, 0))],
        out_specs=pl.BlockSpec((tm, d), lambda i: (i, 0)),
        out_shape=jax.ShapeDtypeStruct((m, d), F32),
        compiler_params=_params(("parallel",)),
        name=name,
    )(x, ya, yb, wa, wb, gate)


HALO = BF16_SUBLANES


def _ffn_kernel(xm_ref, xp_ref, xn_ref, gn_ref, sh_ref, sc_ref, gate_ref, wa_ref, wg_ref,
                cwa_ref, cwg_ref, cba_ref, cbg_ref, wd_ref, o_ref, h_scr, acc_scr, ua_scr, ug_scr):
    i, j = pl.program_id(0), pl.program_id(1)
    ni, nj = pl.num_programs(0), pl.num_programs(1)
    tm = xm_ref.shape[0]

    @pl.when(j == 0)
    def _():
        nm = lambda x: _norm_mod(x, gn_ref[...], sh_ref[...], sc_ref[...])
        h_scr[0:HALO, :] = jnp.where(i > 0, nm(xp_ref[...]), 0.0).astype(BF16)
        h_scr[HALO:HALO + tm, :] = nm(xm_ref[...]).astype(BF16)
        h_scr[HALO + tm:, :] = jnp.where(i < ni - 1, nm(xn_ref[...]), 0.0).astype(BF16)
        acc_scr[...] = jnp.zeros_like(acc_scr)

    h = h_scr[...]
    ua_scr[...] = _dot(h, wa_ref[...])
    ug_scr[...] = _dot(h, wg_ref[...])

    def conv(u_scr, cw_ref, cb_ref):
        out = cb_ref[...] + u_scr[pl.ds(HALO - 1, tm), :] * cw_ref[0:1, :]
        for t in range(1, CONV_W):
            out = out + u_scr[pl.ds(HALO - 1 + t, tm), :] * cw_ref[t:t + 1, :]
        return out

    a = conv(ua_scr, cwa_ref, cba_ref)
    g = conv(ug_scr, cwg_ref, cbg_ref)
    z = (g * (1.0 / (1.0 + jnp.exp(-g)))) * a
    acc_scr[...] += _dot(z.astype(BF16), wd_ref[...])

    @pl.when(j == nj - 1)
    def _():
        o_ref[...] = xm_ref[...] + gate_ref[...] * acc_scr[...]


def _ffn_call(x, gn, sh, sc, gate, w_up, conv_w, conv_b, w_down, tm, tf, name):
    m, d = x.shape
    f = w_down.shape[0]
    nfb = f // tf
    hb = tm // HALO
    last = m // HALO - 1
    row = lambda i, j: (0, 0)
    return pl.pallas_call(
        _ffn_kernel,
        grid=(m // tm, nfb),
        in_specs=[pl.BlockSpec((tm, d), lambda i, j: (i, 0)),
                  pl.BlockSpec((HALO, d), lambda i, j: (jnp.maximum(i * hb - 1, 0), 0)),
                  pl.BlockSpec((HALO, d), lambda i, j: (jnp.minimum((i + 1) * hb, last), 0)),
                  pl.BlockSpec((1, d), row), pl.BlockSpec((1, d), row), pl.BlockSpec((1, d), row),
                  pl.BlockSpec((1, d), row),
                  pl.BlockSpec((d, tf), lambda i, j: (0, j)),
                  pl.BlockSpec((d, tf), lambda i, j: (0, nfb + j)),
                  pl.BlockSpec((CONV_W, tf), lambda i, j: (0, j)),
                  pl.BlockSpec((CONV_W, tf), lambda i, j: (0, nfb + j)),
                  pl.BlockSpec((1, tf), lambda i, j: (0, j)),
                  pl.BlockSpec((1, tf), lambda i, j: (0, nfb + j)),
                  pl.BlockSpec((tf, d), lambda i, j: (j, 0))],
        out_specs=pl.BlockSpec((tm, d), lambda i, j: (i, 0)),
        out_shape=jax.ShapeDtypeStruct((m, d), F32),
        scratch_shapes=[pltpu.VMEM((tm + 2 * HALO, d), BF16),
                        pltpu.VMEM((tm, d), F32),
                        pltpu.VMEM((tm + 2 * HALO, tf), F32),
                        pltpu.VMEM((tm + 2 * HALO, tf), F32)],
        compiler_params=_params(("parallel", "arbitrary")),
        name=name,
    )(x, x, x, gn, sh, sc, gate, w_up, w_up, conv_w, conv_w, conv_b, conv_b, w_down)


def _offsets(sizes):
    offs, o = [], 0
    for s in sizes:
        offs.append(o)
        o += s
    return offs


def _chunk_classes(sizes, classes):
    out = []
    for s, c in zip(sizes, classes):
        out += [c] * (s // EPI_CHUNK)
    return out


def _gain_vector(sizes, gains):
    parts = []
    for s, g in zip(sizes, gains):
        parts.append(jnp.ones((s,), F32) if g is None else jnp.tile(g.astype(F32), s // g.shape[0]))
    return jnp.concatenate(parts).reshape(1, -1)


def _lambda_init(layer_idx):
    return 0.8 - 0.6 * math.exp(-0.3 * layer_idx)


def _forward(x, c, ctx, c_ctx, w_ada, b_ada, norm1_g, w_in, w_out, a_qk_g, a_lambda, a_subln_g,
             b_qk_g, c_qk_g, c_sink, d_qk_g, d_rpb, norm2_g, w_up, conv_w, conv_b, w_down):
    s, d = x.shape
    t = ctx.shape[0]
    depth = w_ada.shape[0]
    n_heads = d // (2 * HEAD_DIM)
    n_kv = n_heads // GQA_GROUP
    hw = n_heads * HEAD_DIM
    kvw = n_kv * HEAD_DIM
    even_sizes = (hw, hw, hw, hw, kvw, kvw)
    odd_sizes = (hw, kvw, kvw, hw, hw, hw)
    tm = 512
    tf = 512

    mod = _mod_call(c, c_ctx, w_ada, b_ada)
    rope_a = _rope_tables(s, 16)
    rope_b = _rope_tables(s, 32)
    tabs = rope_a + rope_b
    scale_a = A_SUB ** -0.5
    scale = HEAD_DIM ** -0.5

    xc = ctx
    for i in range(depth):
        need_ctx = i < depth - 1
        e = i // 2
        vec = lambda r, k: mod[i, r:r + 1, k * d:(k + 1) * d]
        w_in_i = w_in[i].astype(BF16)
        w_out_i = w_out[i].astype(BF16)
        w_up_i = w_up[i].astype(BF16)
        w_down_i = w_down[i].astype(BF16)
        g1n = norm1_g[i].reshape(1, d)
        g2n = norm2_g[i].reshape(1, d)
        if i % 2 == 0:
            sizes = even_sizes
            classes = [(A_SUB, "A"), (A_SUB, "A"), (0, None), (HEAD_DIM, "B"), (HEAD_DIM, "B"), (0, None)]
            gains = [a_qk_g[e, 0] * scale_a, a_qk_g[e, 1], None, b_qk_g[e, 0] * scale, b_qk_g[e, 1], None]
        else:
            sizes = odd_sizes
            classes = [(HEAD_DIM, "B"), (HEAD_DIM, "B"), (0, None), (HEAD_DIM, None), (HEAD_DIM, None), (0, None)]
            gains = [c_qk_g[e, 0] * scale, c_qk_g[e, 1], None, d_qk_g[e, 0] * scale, d_qk_g[e, 1], None]
        offs = _offsets(sizes)
        chunk_cls = _chunk_classes(sizes, classes)
        gain = _gain_vector(sizes, gains)

        p = _inproj_call(x, g1n, vec(0, 0), vec(0, 1), w_in_i, gain, chunk_cls, tabs, tm, f"inproj{i}")
        pc = _inproj_call(xc, g1n, vec(1, 0), vec(1, 1), w_in_i, gain, chunk_cls, None, t, f"inproj_ctx{i}")

        if i % 2 == 0:
            lam_init = _lambda_init(i)
            fa = functools.partial(_flash_call, "diff", offs=offs[0:3], n_heads=n_heads, lam_p=a_lambda[e],
                                   subln=a_subln_g[e], lam_init=lam_init, tq=256)
            fb = functools.partial(_flash_call, "gqa", offs=offs[3:6], n_heads=n_kv, lam_p=None,
                                   subln=None, lam_init=None)
            ya = fa(p, p, pc, name=f"diff_attn{i}")
            yb = fb(p, p, pc, tq=128, name=f"gqa_attn{i}")
            if need_ctx:
                ya_c = fa(pc, None, pc, name=f"diff_attn_ctx{i}")
                yb_c = fb(pc, None, pc, tq=t, name=f"gqa_attn_ctx{i}")
        else:
            ya = _win_call(p, pc, offs[0:3], n_kv, c_sink[e], 256)
            yb = _nbr_call(p, pc, offs[3:6], n_heads, d_rpb[e])
            if need_ctx:
                raise NotImplementedError("context stream after an odd layer")

        wo_a, wo_b = w_out_i[:hw], w_out_i[hw:]
        x = _outproj_call(x, ya, yb, wo_a, wo_b, vec(0, 2), tm, f"outproj{i}")
        x = _ffn_call(x, g2n, vec(0, 3), vec(0, 4), vec(0, 5), w_up_i, conv_w[i], conv_b[i].reshape(1, -1),
                      w_down_i, tm, tf, f"ffn{i}")
        if need_ctx:
            xc = _outproj_call(xc, ya_c, yb_c, wo_a, wo_b, vec(1, 2), t, f"outproj_ctx{i}")
            xc = _ffn_call(xc, g2n, vec(1, 3), vec(1, 4), vec(1, 5), w_up_i, conv_w[i], conv_b[i].reshape(1, -1),
                           w_down_i, t, tf, f"ffn_ctx{i}")
    return x


def kernel(x, c, ctx, c_ctx, w_ada, b_ada, norm1_g, w_in, w_out, a_qk_g, a_lambda, a_subln_g, b_qk_g, c_qk_g,
           c_sink, d_qk_g, d_rpb, norm2_g, w_up, conv_w, conv_b, w_down):
    outs = [_forward(x[b], c[b], ctx[b], c_ctx, w_ada, b_ada, norm1_g, w_in, w_out, a_qk_g, a_lambda, a_subln_g,
                     b_qk_g, c_qk_g, c_sink, d_qk_g, d_rpb, norm2_g, w_up, conv_w, conv_b, w_down)
            for b in range(x.shape[0])]
    return jnp.stack(outs)
```

```python
import functools
import math

import jax
import jax.numpy as jnp
from jax import lax
from jax.experimental import pallas as pl
from jax.experimental.pallas import tpu as pltpu

GRID_W = 64
HEAD_DIM = 128
A_SUB = HEAD_DIM // 2
ROPE_THETA = 10000.0
EPS = 1e-6
NEG_INF = -1e30
N_MOD = 6
C_WINDOW = 128
NA_KH = 8
NA_KW = 16
CONV_W = 3
GQA_GROUP = 4
LOG2E = math.log2(math.e)
LOGIT_BOUND_SLACK = 1.02
MAX_LOGIT_BOUND = 50.0

LANES = 128
BF16_SUBLANES = 16
VMEM_LIMIT = 56 * 1024 * 1024

F32 = jnp.float32
BF16 = jnp.bfloat16


def _params(sem, vmem=VMEM_LIMIT):
    return pltpu.CompilerParams(dimension_semantics=sem, vmem_limit_bytes=vmem)


def _dot(a, b):
    return jnp.dot(a, b, preferred_element_type=F32)


def _dot_nt(a, b):
    return lax.dot_general(a, b, (((1,), (1,)), ((), ())), preferred_element_type=F32)


def _norm_mod(x, g, sh, sc):
    inv = lax.rsqrt(jnp.mean(x * x, axis=-1, keepdims=True) + EPS)
    return (x * inv * g) * (1.0 + sc) + sh


def _mod_kernel(cb_ref, w_ref, b_ref, o_ref):
    tn = w_ref.shape[2]
    for r in range(2):
        cv = cb_ref[r]
        a = cv * (1.0 / (1.0 + jnp.exp(-cv)))
        for ch in range(tn // LANES):
            sl = slice(ch * LANES, (ch + 1) * LANES)
            o_ref[0, r:r + 1, sl] = jnp.sum(w_ref[0, :, sl] * a, axis=0, keepdims=True) + b_ref[0, :, sl]


def _mod_call(c, c_ctx, w_ada, b_ada):
    depth, d, n = w_ada.shape
    tn = 1024
    cb = jnp.broadcast_to(jnp.stack([c, c_ctx])[:, :, None], (2, d, LANES))
    return pl.pallas_call(
        _mod_kernel,
        grid=(depth, n // tn),
        in_specs=[pl.BlockSpec((2, d, LANES), lambda i, j: (0, 0, 0)),
                  pl.BlockSpec((1, d, tn), lambda i, j: (i, 0, j)),
                  pl.BlockSpec((1, 1, tn), lambda i, j: (i, 0, j))],
        out_specs=pl.BlockSpec((1, 2, tn), lambda i, j: (i, 0, j)),
        out_shape=jax.ShapeDtypeStruct((depth, 2, n), F32),
        compiler_params=_params(("parallel", "parallel")),
        name="adaln_mod",
    )(cb, w_ada, b_ada.reshape(depth, 1, n))


EPI_CHUNK = 256


def _head_norm(p, gain, width):
    ss = p * p
    hi = ss.astype(BF16)
    lo = (ss - hi.astype(F32)).astype(BF16)
    shift = int(math.log2(width))
    r = lax.broadcasted_iota(jnp.int32, (EPI_CHUNK, EPI_CHUNK), 0) >> shift
    c = lax.broadcasted_iota(jnp.int32, (EPI_CHUNK, EPI_CHUNK), 1) >> shift
    bd = jnp.where(r == c, 1.0, 0.0).astype(BF16)
    seg = _dot(hi, bd) + _dot(lo, bd)
    return p * lax.rsqrt(seg * (1.0 / width) + EPS) * gain


def _rope(y, half, tabs):
    c_ref, s1_ref, s2_ref = tabs
    c, s1, s2 = c_ref[...], s1_ref[...], s2_ref[...]
    outs = []
    for k in range(y.shape[1] // LANES):
        yk = y[:, k * LANES:(k + 1) * LANES]
        outs.append(yk * c + pltpu.roll(yk, LANES - half, 1) * s1 + pltpu.roll(yk, half, 1) * s2)
    return jnp.concatenate(outs, axis=1)


def _inproj_kernel(*refs, classes, rope, tn):
    x_ref, g_ref, sh_ref, sc_ref, w_ref, gain_ref = refs[:6]
    tabs = {"A": refs[6:9], "B": refs[9:12]} if rope else None
    o_ref = refs[-1]
    h = _norm_mod(x_ref[...], g_ref[...], sh_ref[...], sc_ref[...]).astype(BF16)
    per = tn // EPI_CHUNK
    for t in range(len(classes) // per):
        p = _dot(h, w_ref[:, t * tn:(t + 1) * tn])
        for ci, (width, kind) in enumerate(classes[t * per:(t + 1) * per]):
            sl = slice(t * tn + ci * EPI_CHUNK, t * tn + (ci + 1) * EPI_CHUNK)
            y = p[:, ci * EPI_CHUNK:(ci + 1) * EPI_CHUNK]
            if width:
                y = _head_norm(y, gain_ref[:, sl], width)
            if kind and rope:
                y = _rope(y, 16 if kind == "A" else 32, tabs[kind])
            o_ref[:, sl] = y.astype(BF16)


def _inproj_call(x, g, sh, sc, w, gain, classes, tabs, tm, name):
    m, d = x.shape
    n = w.shape[1]
    rope = tabs is not None
    const = lambda i: (0, 0)
    in_specs = [pl.BlockSpec((tm, d), lambda i: (i, 0)),
                pl.BlockSpec((1, d), const), pl.BlockSpec((1, d), const), pl.BlockSpec((1, d), const),
                pl.BlockSpec((d, n), const, pipeline_mode=pl.Buffered(1)),
                pl.BlockSpec((1, n), const)]
    args = [x, g, sh, sc, w, gain]
    if rope:
        in_specs += [pl.BlockSpec((tm, LANES), lambda i: (i, 0))] * 6
        args += list(tabs)
    return pl.pallas_call(
        functools.partial(_inproj_kernel, classes=tuple(classes), rope=rope, tn=2 * EPI_CHUNK),
        grid=(m // tm,),
        in_specs=in_specs,
        out_specs=pl.BlockSpec((tm, n), lambda i: (i, 0)),
        out_shape=jax.ShapeDtypeStruct((m, n), BF16),
        compiler_params=_params(("parallel",)),
        name=name,
    )(*args)


def _rope_tables(s, half):
    pos = jnp.arange(s, dtype=jnp.int32)
    freqs = ROPE_THETA ** (-jnp.arange(half, dtype=F32) / half)
    zeros = jnp.zeros((s, half), F32)
    cs, s1s, s2s = [], [], []
    for p in (pos // GRID_W, pos % GRID_W):
        ang = p.astype(F32)[:, None] * freqs[None, :]
        cos, sin = jnp.cos(ang), jnp.sin(ang)
        cs += [cos, cos]
        s1s += [-sin, zeros]
        s2s += [zeros, sin]
    reps = LANES // (4 * half)
    return tuple(jnp.tile(jnp.concatenate(t, axis=1), (1, reps)) for t in (cs, s1s, s2s))


def _flash_kernel(*refs, mode, tq, tk, n_lat, lam_init, bounded):
    if bounded:
        bound = refs[0][0]
        refs = refs[1:]
    q_ref, kc_ref, vc_ref = refs[:3]
    pos = 3
    if n_lat:
        k_ref, v_ref = refs[3:5]
        pos = 5
    if mode == "diff":
        lam_ref, sg_ref = refs[pos:pos + 2]
        pos += 2
    o_ref = refs[pos]

    if mode == "diff":
        q = q_ref[...].astype(F32)
        lane = lax.broadcasted_iota(jnp.int32, q.shape, 1)
        qs = jnp.concatenate([jnp.where(lane < A_SUB, q, 0.0), jnp.where(lane >= A_SUB, q, 0.0)],
                             axis=0).astype(BF16)
    else:
        qs = jnp.concatenate([q_ref[:, g * HEAD_DIM:(g + 1) * HEAD_DIM] for g in range(GQA_GROUP)], axis=0)
    rows = qs.shape[0]

    def pv_dot(v, p):
        return lax.dot_general(v, p.astype(BF16), (((0,), (0,)), ((), ())), preferred_element_type=F32)

    if bounded:
        def step(k, v, carry):
            _, l, acc = carry
            p = jnp.exp2(_dot_nt(k, qs) - bound)
            return carry[0], l + jnp.sum(p, axis=0, keepdims=True), acc + pv_dot(v, p)
    else:
        def step(k, v, carry):
            m, l, acc = carry
            s = _dot_nt(k, qs)
            m_new = jnp.maximum(m, jnp.max(s, axis=0, keepdims=True))
            p = jnp.exp2(s - m_new)
            alpha = jnp.exp2(m - m_new)
            return m_new, alpha * l + jnp.sum(p, axis=0, keepdims=True), alpha * acc + pv_dot(v, p)

    carry = (jnp.full((1, rows), NEG_INF, F32), jnp.zeros((1, rows), F32), jnp.zeros((HEAD_DIM, rows), F32))
    carry = step(kc_ref[...], vc_ref[...], carry)
    if n_lat:
        def body(c, carry):
            off = pl.multiple_of(c * tk, tk)
            return step(k_ref[pl.ds(off, tk), :], v_ref[pl.ds(off, tk), :], carry)
        carry = lax.fori_loop(0, n_lat, body, carry, unroll=4)
    _, l, acc = carry
    o = (acc * (1.0 / l)).T
    if mode == "diff":
        lp = lam_ref[...]
        lam = (jnp.exp(jnp.sum(lp[0:1] * lp[1:2], axis=1, keepdims=True))
               - jnp.exp(jnp.sum(lp[2:3] * lp[3:4], axis=1, keepdims=True)) + lam_init)
        dlt = o[:tq] - lam * o[tq:]
        y = dlt * lax.rsqrt(jnp.mean(dlt * dlt, axis=-1, keepdims=True) + EPS) * sg_ref[...]
        o_ref[...] = (y * (1.0 - lam_init)).astype(BF16)
    else:
        for g in range(GQA_GROUP):
            o_ref[:, g * HEAD_DIM:(g + 1) * HEAD_DIM] = o[g * tq:(g + 1) * tq].astype(BF16)


def _flash_call(mode, qsrc, ksrc, pc, offs, n_heads, lam_p, subln, lam_init, tq, name, bound=None):
    sq = qsrc.shape[0]
    t = pc.shape[0]
    qo, ko, vo = offs
    tk = 2048
    qw = HEAD_DIM if mode == "diff" else GQA_GROUP * HEAD_DIM
    qb, kb, vb = qo // qw, ko // HEAD_DIM, vo // HEAD_DIM
    in_specs = [pl.BlockSpec((tq, qw), lambda h, i: (i, qb + h)),
                pl.BlockSpec((t, HEAD_DIM), lambda h, i: (0, kb + h)),
                pl.BlockSpec((t, HEAD_DIM), lambda h, i: (0, vb + h))]
    args = [qsrc, pc, pc]
    if bound is not None:
        in_specs = [pl.BlockSpec(memory_space=pltpu.SMEM)] + in_specs
        args = [bound] + args
    n_lat = 0
    if ksrc is not None:
        s = ksrc.shape[0]
        tk = min(tk, s)
        assert s % tk == 0
        n_lat = s // tk
        in_specs += [pl.BlockSpec((s, HEAD_DIM), lambda h, i: (0, kb + h)),
                     pl.BlockSpec((s, HEAD_DIM), lambda h, i: (0, vb + h))]
        args += [ksrc, ksrc]
    if mode == "diff":
        in_specs += [pl.BlockSpec(lam_p.shape, lambda h, i: (0, 0)),
                     pl.BlockSpec((1, HEAD_DIM), lambda h, i: (0, 0))]
        args += [lam_p, subln.reshape(1, HEAD_DIM)]
    return pl.pallas_call(
        functools.partial(_flash_kernel, mode=mode, tq=tq, tk=tk, n_lat=n_lat, lam_init=lam_init,
                          bounded=bound is not None),
        grid=(n_heads, sq // tq),
        in_specs=in_specs,
        out_specs=pl.BlockSpec((tq, qw), lambda h, i: (i, h)),
        out_shape=jax.ShapeDtypeStruct((sq, n_heads * qw), BF16),
        compiler_params=_params(("parallel", "parallel")),
        name=name,
    )(*args)


def _win_kernel(sink_ref, q_ref, k_ref, v_ref, kc_ref, vc_ref, o_ref, *, tq, s_len):
    kvh = pl.program_id(0)
    q0 = pl.program_id(1) * tq
    band = tq + 2 * C_WINDOW
    start = pl.multiple_of(jnp.clip(q0 - C_WINDOW, 0, s_len - band), C_WINDOW)
    kb = k_ref[pl.ds(start, band), :]
    vb = v_ref[pl.ds(start, band), :]
    kc, vc = kc_ref[...], vc_ref[...]
    qpos = q0 + lax.broadcasted_iota(jnp.int32, (tq, band), 0)
    kpos = start + lax.broadcasted_iota(jnp.int32, (tq, band), 1)
    valid = jnp.abs(qpos - kpos) <= C_WINDOW
    for g in range(GQA_GROUP):
        sl = slice(g * HEAD_DIM, (g + 1) * HEAD_DIM)
        q = q_ref[:, sl]
        s_lat = jnp.where(valid, _dot_nt(q, kb), NEG_INF)
        s_ctx = _dot_nt(q, kc)
        sink = sink_ref[kvh * GQA_GROUP + g]
        m = jnp.maximum(jnp.maximum(jnp.max(s_lat, axis=1, keepdims=True),
                                    jnp.max(s_ctx, axis=1, keepdims=True)), sink)
        p_lat = jnp.exp(s_lat - m)
        p_ctx = jnp.exp(s_ctx - m)
        l = (jnp.sum(p_lat, axis=1, keepdims=True) + jnp.sum(p_ctx, axis=1, keepdims=True)
             + jnp.exp(sink - m))
        o = _dot(p_ctx.astype(BF16), vc) + _dot(p_lat.astype(BF16), vb)
        o_ref[:, sl] = (o * (1.0 / l)).astype(BF16)


def _win_call(p, pc, offs, n_kv, sink, tq):
    s = p.shape[0]
    t = pc.shape[0]
    qo, ko, vo = offs
    qw = GQA_GROUP * HEAD_DIM
    qb, kb, vb = qo // qw, ko // HEAD_DIM, vo // HEAD_DIM
    return pl.pallas_call(
        functools.partial(_win_kernel, tq=tq, s_len=s),
        grid=(n_kv, s // tq),
        in_specs=[pl.BlockSpec(memory_space=pltpu.SMEM),
                  pl.BlockSpec((tq, qw), lambda h, i: (i, qb + h)),
                  pl.BlockSpec((s, HEAD_DIM), lambda h, i: (0, kb + h)),
                  pl.BlockSpec((s, HEAD_DIM), lambda h, i: (0, vb + h)),
                  pl.BlockSpec((t, HEAD_DIM), lambda h, i: (0, kb + h)),
                  pl.BlockSpec((t, HEAD_DIM), lambda h, i: (0, vb + h))],
        out_specs=pl.BlockSpec((tq, qw), lambda h, i: (i, h)),
        out_shape=jax.ShapeDtypeStruct((s, n_kv * qw), BF16),
        compiler_params=_params(("parallel", "parallel")),
        name="window_attn",
    )(sink, p, p, p, pc, pc)


NBR_Q_ROWS = 4
NBR_WIN_ROWS = NBR_Q_ROWS + NA_KH


def _nbr_build_bias(rpb_ref, head, mats_scr, tab_scr, n_rows):
    na, nb = 2 * NA_KH - 1, 2 * NA_KW - 1
    shape = (GRID_W, 2 * GRID_W)
    wq = lax.broadcasted_iota(jnp.int32, shape, 0)
    lane = lax.broadcasted_iota(jnp.int32, shape, 1)
    wk = lane & (GRID_W - 1)
    dc = jnp.clip(wk - wq, -(NA_KW - 1), NA_KW - 1) + (NA_KW - 1)
    c_start = jnp.clip(wq - NA_KW // 2, 0, GRID_W - NA_KW)
    col_ok = jnp.logical_and(wk >= c_start, wk < c_start + NA_KW)

    def build(a, carry):
        base = (head * na + a) * nb
        mat = jnp.zeros(shape, F32)
        for b in range(nb):
            mat = jnp.where(dc == b, rpb_ref[base + b], mat)
        mats_scr[a] = jnp.where(col_ok, mat, NEG_INF)
        return carry

    lax.fori_loop(0, na, build, 0)
    neg = jnp.full(shape, NEG_INF, F32)
    first_half = lane < GRID_W
    for cl, r0 in enumerate((0, NBR_Q_ROWS, n_rows - NBR_Q_ROWS)):
        ws = min(max(r0 - NA_KH // 2, 0), n_rows - NBR_WIN_ROWS)
        for qr in range(NBR_Q_ROWS):
            r = r0 + qr
            r_start = min(max(r - NA_KH // 2, 0), n_rows - NA_KH)
            for kp in range(NBR_WIN_ROWS // 2):
                halves = []
                for kr in (ws + 2 * kp, ws + 2 * kp + 1):
                    inside = r_start <= kr < r_start + NA_KH
                    halves.append(mats_scr[kr - r + NA_KH - 1] if inside else neg)
                tab_scr[cl, qr * GRID_W:(qr + 1) * GRID_W, kp * 2 * GRID_W:(kp + 1) * 2 * GRID_W] = (
                    jnp.where(first_half, halves[0], halves[1]))


def _nbr_kernel(rpb_ref, q_ref, k_ref, v_ref, kc_ref, vc_ref, o_ref, mats_scr, tab_scr, *, tq, win, s_len):
    i = pl.program_id(1)
    nq = pl.num_programs(1)

    @pl.when(i == 0)
    def _():
        _nbr_build_bias(rpb_ref, pl.program_id(0), mats_scr, tab_scr, s_len // GRID_W)

    q0 = i * tq
    lead = (NA_KH // 2) * GRID_W
    start = pl.multiple_of(jnp.clip(q0 - lead, 0, s_len - win), lead)
    q = q_ref[...]
    cls = jnp.where(i == 0, 0, jnp.where(i == nq - 1, 2, 1))
    s_win = _dot_nt(q, k_ref[pl.ds(start, win), :]) + tab_scr[cls]
    s_ctx = _dot_nt(q, kc_ref[...])
    m = jnp.maximum(jnp.max(s_win, axis=1, keepdims=True), jnp.max(s_ctx, axis=1, keepdims=True))
    p_win = jnp.exp(s_win - m)
    p_ctx = jnp.exp(s_ctx - m)
    l = jnp.sum(p_win, axis=1, keepdims=True) + jnp.sum(p_ctx, axis=1, keepdims=True)
    o = _dot(p_ctx.astype(BF16), vc_ref[...]) + _dot(p_win.astype(BF16), v_ref[pl.ds(start, win), :])
    o_ref[...] = (o * (1.0 / l)).astype(BF16)


def _nbr_call(p, pc, offs, n_heads, rpb):
    s = p.shape[0]
    t = pc.shape[0]
    tq = NBR_Q_ROWS * GRID_W
    win = NBR_WIN_ROWS * GRID_W
    qb, kb, vb = (o // HEAD_DIM for o in offs)
    assert s // GRID_W >= NBR_WIN_ROWS and rpb.shape[1:] == (2 * NA_KH - 1, 2 * NA_KW - 1)
    return pl.pallas_call(
        functools.partial(_nbr_kernel, tq=tq, win=win, s_len=s),
        grid=(n_heads, s // tq),
        in_specs=[pl.BlockSpec(memory_space=pltpu.SMEM),
                  pl.BlockSpec((tq, HEAD_DIM), lambda h, i: (i, qb + h)),
                  pl.BlockSpec((s, HEAD_DIM), lambda h, i: (0, kb + h)),
                  pl.BlockSpec((s, HEAD_DIM), lambda h, i: (0, vb + h)),
                  pl.BlockSpec((t, HEAD_DIM), lambda h, i: (0, kb + h)),
                  pl.BlockSpec((t, HEAD_DIM), lambda h, i: (0, vb + h))],
        out_specs=pl.BlockSpec((tq, HEAD_DIM), lambda h, i: (i, h)),
        out_shape=jax.ShapeDtypeStruct((s, n_heads * HEAD_DIM), BF16),
        scratch_shapes=[pltpu.VMEM((2 * NA_KH - 1, GRID_W, 2 * GRID_W), F32),
                        pltpu.VMEM((3, tq, win), F32)],
        compiler_params=_params(("parallel", "arbitrary")),
        name="nbr_attn",
    )(rpb.astype(F32).reshape(-1), p, p, p, pc, pc)


def _outproj_kernel(x_ref, ya_ref, yb_ref, wa_ref, wb_ref, g_ref, o_ref):
    acc = _dot(ya_ref[...], wa_ref[...]) + _dot(yb_ref[...], wb_ref[...])
    o_ref[...] = x_ref[...] + g_ref[...] * acc


def _outproj_call(x, ya, yb, wa, wb, gate, tm, name):
    m, d = x.shape
    ka, kb = ya.shape[1], yb.shape[1]
    return pl.pallas_call(
        _outproj_kernel,
        grid=(m // tm,),
        in_specs=[pl.BlockSpec((tm, d), lambda i: (i, 0)),
                  pl.BlockSpec((tm, ka), lambda i: (i, 0)),
                  pl.BlockSpec((tm, kb), lambda i: (i, 0)),
                  pl.BlockSpec((ka, d), lambda i: (0, 0)),
                  pl.BlockSpec((kb, d), lambda i: (0, 0)),
                  pl.BlockSpec((1, d), lambda i: (0, 0))],
        out_specs=pl.BlockSpec((tm, d), lambda i: (i, 0)),
        out_shape=jax.ShapeDtypeStruct((m, d), F32),
        compiler_params=_params(("parallel",)),
        name=name,
    )(x, ya, yb, wa, wb, gate)


HALO = BF16_SUBLANES


def _ffn_kernel(xm_ref, xp_ref, xn_ref, gn_ref, sh_ref, sc_ref, gate_ref, wa_ref, wg_ref,
                cwa_ref, cwg_ref, cba_ref, cbg_ref, wd_ref, o_ref, h_scr, acc_scr, ua_scr, ug_scr):
    i, j = pl.program_id(0), pl.program_id(1)
    ni, nj = pl.num_programs(0), pl.num_programs(1)
    tm = xm_ref.shape[0]

    @pl.when(j == 0)
    def _():
        nm = lambda x: _norm_mod(x, gn_ref[...], sh_ref[...], sc_ref[...])
        h_scr[0:HALO, :] = jnp.where(i > 0, nm(xp_ref[...]), 0.0).astype(BF16)
        h_scr[HALO:HALO + tm, :] = nm(xm_ref[...]).astype(BF16)
        h_scr[HALO + tm:, :] = jnp.where(i < ni - 1, nm(xn_ref[...]), 0.0).astype(BF16)
        acc_scr[...] = jnp.zeros_like(acc_scr)

    h = h_scr[...]
    ua_scr[...] = _dot(h, wa_ref[...])
    ug_scr[...] = _dot(h, wg_ref[...])

    def conv(u_scr, cw_ref, cb_ref):
        out = cb_ref[...] + u_scr[pl.ds(HALO - 1, tm), :] * cw_ref[0:1, :]
        for t in range(1, CONV_W):
            out = out + u_scr[pl.ds(HALO - 1 + t, tm), :] * cw_ref[t:t + 1, :]
        return out

    a = conv(ua_scr, cwa_ref, cba_ref)
    g = conv(ug_scr, cwg_ref, cbg_ref)
    z = (g * (1.0 / (1.0 + jnp.exp(-g)))) * a
    acc_scr[...] += _dot(z.astype(BF16), wd_ref[...])

    @pl.when(j == nj - 1)
    def _():
        o_ref[...] = xm_ref[...] + gate_ref[...] * acc_scr[...]


def _ffn_call(x, gn, sh, sc, gate, w_up, conv_w, conv_b, w_down, tm, tf, name):
    m, d = x.shape
    f = w_down.shape[0]
    nfb = f // tf
    hb = tm // HALO
    last = m // HALO - 1
    row = lambda i, j: (0, 0)
    return pl.pallas_call(
        _ffn_kernel,
        grid=(m // tm, nfb),
        in_specs=[pl.BlockSpec((tm, d), lambda i, j: (i, 0)),
                  pl.BlockSpec((HALO, d), lambda i, j: (jnp.maximum(i * hb - 1, 0), 0)),
                  pl.BlockSpec((HALO, d), lambda i, j: (jnp.minimum((i + 1) * hb, last), 0)),
                  pl.BlockSpec((1, d), row), pl.BlockSpec((1, d), row), pl.BlockSpec((1, d), row),
                  pl.BlockSpec((1, d), row),
                  pl.BlockSpec((d, tf), lambda i, j: (0, j)),
                  pl.BlockSpec((d, tf), lambda i, j: (0, nfb + j)),
                  pl.BlockSpec((CONV_W, tf), lambda i, j: (0, j)),
                  pl.BlockSpec((CONV_W, tf), lambda i, j: (0, nfb + j)),
                  pl.BlockSpec((1, tf), lambda i, j: (0, j)),
                  pl.BlockSpec((1, tf), lambda i, j: (0, nfb + j)),
                  pl.BlockSpec((tf, d), lambda i, j: (j, 0))],
        out_specs=pl.BlockSpec((tm, d), lambda i, j: (i, 0)),
        out_shape=jax.ShapeDtypeStruct((m, d), F32),
        scratch_shapes=[pltpu.VMEM((tm + 2 * HALO, d), BF16),
                        pltpu.VMEM((tm, d), F32),
                        pltpu.VMEM((tm + 2 * HALO, tf), F32),
                        pltpu.VMEM((tm + 2 * HALO, tf), F32)],
        compiler_params=_params(("parallel", "arbitrary")),
        name=name,
    )(x, x, x, gn, sh, sc, gate, w_up, w_up, conv_w, conv_w, conv_b, conv_b, w_down)


def _offsets(sizes):
    offs, o = [], 0
    for s in sizes:
        offs.append(o)
        o += s
    return offs


def _chunk_classes(sizes, classes):
    out = []
    for s, c in zip(sizes, classes):
        out += [c] * (s // EPI_CHUNK)
    return out


def _gain_vector(sizes, gains):
    parts = []
    for s, g in zip(sizes, gains):
        parts.append(jnp.ones((s,), F32) if g is None else jnp.tile(g.astype(F32), s // g.shape[0]))
    return jnp.concatenate(parts).reshape(1, -1)


def _lambda_init(layer_idx):
    return 0.8 - 0.6 * math.exp(-0.3 * layer_idx)


def _forward(x, c, ctx, c_ctx, w_ada, b_ada, norm1_g, w_in, w_out, a_qk_g, a_lambda, a_subln_g,
             b_qk_g, c_qk_g, c_sink, d_qk_g, d_rpb, norm2_g, w_up, conv_w, conv_b, w_down):
    s, d = x.shape
    t = ctx.shape[0]
    depth = w_ada.shape[0]
    n_heads = d // (2 * HEAD_DIM)
    n_kv = n_heads // GQA_GROUP
    hw = n_heads * HEAD_DIM
    kvw = n_kv * HEAD_DIM
    even_sizes = (hw, hw, hw, hw, kvw, kvw)
    odd_sizes = (hw, kvw, kvw, hw, hw, hw)
    tm = 512
    tf = 512

    mod = _mod_call(c, c_ctx, w_ada, b_ada)
    rope_a = _rope_tables(s, 16)
    rope_b = _rope_tables(s, 32)
    tabs = rope_a + rope_b
    scale_a = A_SUB ** -0.5
    scale = HEAD_DIM ** -0.5

    xc = ctx
    for i in range(depth):
        need_ctx = i < depth - 1
        e = i // 2
        vec = lambda r, k: mod[i, r:r + 1, k * d:(k + 1) * d]
        w_in_i = w_in[i].astype(BF16)
        w_out_i = w_out[i].astype(BF16)
        w_up_i = w_up[i].astype(BF16)
        w_down_i = w_down[i].astype(BF16)
        g1n = norm1_g[i].reshape(1, d)
        g2n = norm2_g[i].reshape(1, d)
        if i % 2 == 0:
            sizes = even_sizes
            classes = [(A_SUB, "A"), (A_SUB, "A"), (0, None), (HEAD_DIM, "B"), (HEAD_DIM, "B"), (0, None)]
            gains = [a_qk_g[e, 0] * (scale_a * LOG2E), a_qk_g[e, 1], None,
                     b_qk_g[e, 0] * (scale * LOG2E), b_qk_g[e, 1], None]
        else:
            sizes = odd_sizes
            classes = [(HEAD_DIM, "B"), (HEAD_DIM, "B"), (0, None), (HEAD_DIM, None), (HEAD_DIM, None), (0, None)]
            gains = [c_qk_g[e, 0] * scale, c_qk_g[e, 1], None, d_qk_g[e, 0] * scale, d_qk_g[e, 1], None]
        offs = _offsets(sizes)
        chunk_cls = _chunk_classes(sizes, classes)
        gain = _gain_vector(sizes, gains)

        p = _inproj_call(x, g1n, vec(0, 0), vec(0, 1), w_in_i, gain, chunk_cls, tabs, tm, f"inproj{i}")
        pc = _inproj_call(xc, g1n, vec(1, 0), vec(1, 1), w_in_i, gain, chunk_cls, None, t, f"inproj_ctx{i}")

        if i % 2 == 0:
            lam_init = _lambda_init(i)
            fa = functools.partial(_flash_call, "diff", offs=offs[0:3], n_heads=n_heads, lam_p=a_lambda[e],
                                   subln=a_subln_g[e], lam_init=lam_init, tq=256)
            fb = functools.partial(_flash_call, "gqa", offs=offs[3:6], n_heads=n_kv, lam_p=None,
                                   subln=None, lam_init=None)
            bound_a = (A_SUB * LOGIT_BOUND_SLACK) * jnp.max(jnp.abs(gains[0])) * jnp.max(jnp.abs(gains[1]))
            bound_b = (HEAD_DIM * LOGIT_BOUND_SLACK) * jnp.max(jnp.abs(gains[3])) * jnp.max(jnp.abs(gains[4]))
            ya = lax.cond(bound_a <= MAX_LOGIT_BOUND,
                          lambda: fa(p, p, pc, name=f"diff_attn{i}", bound=bound_a.reshape(1)),
                          lambda: fa(p, p, pc, name=f"diff_attn_online{i}"))
            yb = lax.cond(bound_b <= MAX_LOGIT_BOUND,
                          lambda: fb(p, p, pc, tq=128, name=f"gqa_attn{i}", bound=bound_b.reshape(1)),
                          lambda: fb(p, p, pc, tq=128, name=f"gqa_attn_online{i}"))
            if need_ctx:
                ya_c = fa(pc, None, pc, name=f"diff_attn_ctx{i}")
                yb_c = fb(pc, None, pc, tq=t, name=f"gqa_attn_ctx{i}")
        else:
            ya = _win_call(p, pc, offs[0:3], n_kv, c_sink[e], 256)
            yb = _nbr_call(p, pc, offs[3:6], n_heads, d_rpb[e])
            if need_ctx:
                raise NotImplementedError("context stream after an odd layer")

        wo_a, wo_b = w_out_i[:hw], w_out_i[hw:]
        x = _outproj_call(x, ya, yb, wo_a, wo_b, vec(0, 2), tm, f"outproj{i}")
        x = _ffn_call(x, g2n, vec(0, 3), vec(0, 4), vec(0, 5), w_up_i, conv_w[i], conv_b[i].reshape(1, -1),
                      w_down_i, tm, tf, f"ffn{i}")
        if need_ctx:
            xc = _outproj_call(xc, ya_c, yb_c, wo_a, wo_b, vec(1, 2), t, f"outproj_ctx{i}")
            xc = _ffn_call(xc, g2n, vec(1, 3), vec(1, 4), vec(1, 5), w_up_i, conv_w[i], conv_b[i].reshape(1, -1),
                           w_down_i, t, tf, f"ffn_ctx{i}")
    return x


def kernel(x, c, ctx, c_ctx, w_ada, b_ada, norm1_g, w_in, w_out, a_qk_g, a_lambda, a_subln_g, b_qk_g, c_qk_g,
           c_sink, d_qk_g, d_rpb, norm2_g, w_up, conv_w, conv_b, w_down):
    outs = [_forward(x[b], c[b], ctx[b], c_ctx, w_ada, b_ada, norm1_g, w_in, w_out, a_qk_g, a_lambda, a_subln_g,
                     b_qk_g, c_qk_g, c_sink, d_qk_g, d_rpb, norm2_g, w_up, conv_w, conv_b, w_down)
            for b in range(x.shape[0])]
    return jnp.stack(outs)
```

```python
import functools
import math

import jax
import jax.numpy as jnp
from jax import lax
from jax.experimental import pallas as pl
from jax.experimental.pallas import tpu as pltpu

GRID_W = 64
HEAD_DIM = 128
A_SUB = HEAD_DIM // 2
ROPE_THETA = 10000.0
EPS = 1e-6
NEG_INF = -1e30
N_MOD = 6
C_WINDOW = 128
NA_KH = 8
NA_KW = 16
CONV_W = 3
GQA_GROUP = 4
LOG2E = math.log2(math.e)
LOGIT_BOUND_SLACK = 1.02
MAX_LOGIT_BOUND = 50.0

LANES = 128
BF16_SUBLANES = 16
VMEM_LIMIT = 56 * 1024 * 1024

F32 = jnp.float32
BF16 = jnp.bfloat16


def _params(sem, vmem=VMEM_LIMIT):
    return pltpu.CompilerParams(dimension_semantics=sem, vmem_limit_bytes=vmem)


def _dot(a, b):
    return jnp.dot(a, b, preferred_element_type=F32)


def _dot_nt(a, b):
    return lax.dot_general(a, b, (((1,), (1,)), ((), ())), preferred_element_type=F32)


def _dot_tn(a, b):
    return lax.dot_general(a, b, (((0,), (0,)), ((), ())), preferred_element_type=F32)


def _norm_mod(x, g, sh, sc):
    inv = lax.rsqrt(jnp.mean(x * x, axis=-1, keepdims=True) + EPS)
    return (x * inv * g) * (1.0 + sc) + sh


def _mod_kernel(cb_ref, w_ref, b_ref, o_ref):
    tn = w_ref.shape[2]
    for r in range(2):
        cv = cb_ref[r]
        a = cv * (1.0 / (1.0 + jnp.exp(-cv)))
        for ch in range(tn // LANES):
            sl = slice(ch * LANES, (ch + 1) * LANES)
            o_ref[0, r:r + 1, sl] = jnp.sum(w_ref[0, :, sl] * a, axis=0, keepdims=True) + b_ref[0, :, sl]


def _mod_call(c, c_ctx, w_ada, b_ada):
    depth, d, n = w_ada.shape
    tn = 1024
    cb = jnp.broadcast_to(jnp.stack([c, c_ctx])[:, :, None], (2, d, LANES))
    return pl.pallas_call(
        _mod_kernel,
        grid=(depth, n // tn),
        in_specs=[pl.BlockSpec((2, d, LANES), lambda i, j: (0, 0, 0)),
                  pl.BlockSpec((1, d, tn), lambda i, j: (i, 0, j)),
                  pl.BlockSpec((1, 1, tn), lambda i, j: (i, 0, j))],
        out_specs=pl.BlockSpec((1, 2, tn), lambda i, j: (i, 0, j)),
        out_shape=jax.ShapeDtypeStruct((depth, 2, n), F32),
        compiler_params=_params(("parallel", "parallel")),
        name="adaln_mod",
    )(cb, w_ada, b_ada.reshape(depth, 1, n))


EPI_CHUNK = 256


def _head_norm(p, gain, width):
    ss = p * p
    hi = ss.astype(BF16)
    lo = (ss - hi.astype(F32)).astype(BF16)
    shift = int(math.log2(width))
    r = lax.broadcasted_iota(jnp.int32, (EPI_CHUNK, EPI_CHUNK), 0) >> shift
    c = lax.broadcasted_iota(jnp.int32, (EPI_CHUNK, EPI_CHUNK), 1) >> shift
    bd = jnp.where(r == c, 1.0, 0.0).astype(BF16)
    seg = _dot(hi, bd) + _dot(lo, bd)
    return p * lax.rsqrt(seg * (1.0 / width) + EPS) * gain


def _rope(y, half, tabs):
    c_ref, s1_ref, s2_ref = tabs
    c, s1, s2 = c_ref[...], s1_ref[...], s2_ref[...]
    outs = []
    for k in range(y.shape[1] // LANES):
        yk = y[:, k * LANES:(k + 1) * LANES]
        outs.append(yk * c + pltpu.roll(yk, LANES - half, 1) * s1 + pltpu.roll(yk, half, 1) * s2)
    return jnp.concatenate(outs, axis=1)


def _inproj_kernel(*refs, classes, rope, tn):
    x_ref, g_ref, sh_ref, sc_ref, w_ref, gain_ref = refs[:6]
    tabs = {"A": refs[6:9], "B": refs[9:12]} if rope else None
    o_ref = refs[-1]
    h = _norm_mod(x_ref[...], g_ref[...], sh_ref[...], sc_ref[...]).astype(BF16)
    per = tn // EPI_CHUNK
    for t in range(len(classes) // per):
        p = _dot(h, w_ref[:, t * tn:(t + 1) * tn])
        for ci, (width, kind) in enumerate(classes[t * per:(t + 1) * per]):
            sl = slice(t * tn + ci * EPI_CHUNK, t * tn + (ci + 1) * EPI_CHUNK)
            y = p[:, ci * EPI_CHUNK:(ci + 1) * EPI_CHUNK]
            if width:
                y = _head_norm(y, gain_ref[:, sl], width)
            if kind and rope:
                y = _rope(y, 16 if kind == "A" else 32, tabs[kind])
            o_ref[:, sl] = y.astype(BF16)


def _inproj_call(x, g, sh, sc, w, layer, gain, classes, tabs, tm, name):
    m, d = x.shape
    n = w.shape[2]
    rope = tabs is not None
    const = lambda i: (0, 0)
    in_specs = [pl.BlockSpec((tm, d), lambda i: (i, 0)),
                pl.BlockSpec((1, d), const), pl.BlockSpec((1, d), const), pl.BlockSpec((1, d), const),
                pl.BlockSpec((None, d, n), lambda i: (layer, 0, 0), pipeline_mode=pl.Buffered(1)),
                pl.BlockSpec((1, n), const)]
    args = [x, g, sh, sc, w, gain]
    if rope:
        in_specs += [pl.BlockSpec((tm, LANES), lambda i: (i, 0))] * 6
        args += list(tabs)
    return pl.pallas_call(
        functools.partial(_inproj_kernel, classes=tuple(classes), rope=rope, tn=2 * EPI_CHUNK),
        grid=(m // tm,),
        in_specs=in_specs,
        out_specs=pl.BlockSpec((tm, n), lambda i: (i, 0)),
        out_shape=jax.ShapeDtypeStruct((m, n), BF16),
        compiler_params=_params(("parallel",)),
        name=name,
    )(*args)


def _rope_tables(s, half):
    n_rows = s // GRID_W
    freqs = ROPE_THETA ** (-jnp.arange(half, dtype=F32) / half)
    zeros = jnp.zeros((s, half), F32)
    cs, s1s, s2s = [], [], []
    for count, expand in ((n_rows, lambda a: jnp.repeat(a, GRID_W, axis=0)),
                          (GRID_W, lambda a: jnp.tile(a, (n_rows, 1)))):
        ang = jnp.arange(count, dtype=jnp.int32).astype(F32)[:, None] * freqs[None, :]
        cos, sin = expand(jnp.cos(ang)), expand(jnp.sin(ang))
        cs += [cos, cos]
        s1s += [-sin, zeros]
        s2s += [zeros, sin]
    reps = LANES // (4 * half)
    return tuple(jnp.tile(jnp.concatenate(t, axis=1), (1, reps)) for t in (cs, s1s, s2s))


def _flash_kernel(*refs, mode, tq, tk, n_lat, lam_init, bounded):
    if bounded:
        bound = refs[0][0]
        refs = refs[1:]
    q_ref, kc_ref, vc_ref = refs[:3]
    pos = 3
    if n_lat:
        k_ref, v_ref = refs[3:5]
        pos = 5
    if mode == "diff":
        lam_ref, sg_ref = refs[pos:pos + 2]
        pos += 2
    o_ref = refs[pos]

    if mode == "diff":
        q = q_ref[...].astype(F32)
        lane = lax.broadcasted_iota(jnp.int32, q.shape, 1)
        qs = jnp.concatenate([jnp.where(lane < A_SUB, q, 0.0), jnp.where(lane >= A_SUB, q, 0.0)],
                             axis=0).astype(BF16)
    else:
        qs = jnp.concatenate([q_ref[:, g * HEAD_DIM:(g + 1) * HEAD_DIM] for g in range(GQA_GROUP)], axis=0)
    rows = qs.shape[0]

    def pv_dot(v, p):
        return lax.dot_general(v, p.astype(BF16), (((0,), (0,)), ((), ())), preferred_element_type=F32)

    if bounded:
        def step(k, v, carry):
            _, l, acc = carry
            p = jnp.exp2(_dot_nt(k, qs) - bound)
            return carry[0], l + jnp.sum(p, axis=0, keepdims=True), acc + pv_dot(v, p)
    else:
        def step(k, v, carry):
            m, l, acc = carry
            s = _dot_nt(k, qs)
            m_new = jnp.maximum(m, jnp.max(s, axis=0, keepdims=True))
            p = jnp.exp2(s - m_new)
            alpha = jnp.exp2(m - m_new)
            return m_new, alpha * l + jnp.sum(p, axis=0, keepdims=True), alpha * acc + pv_dot(v, p)

    carry = (jnp.full((1, rows), NEG_INF, F32), jnp.zeros((1, rows), F32), jnp.zeros((HEAD_DIM, rows), F32))
    carry = step(kc_ref[...], vc_ref[...], carry)
    if n_lat:
        def body(c, carry):
            off = pl.multiple_of(c * tk, tk)
            return step(k_ref[pl.ds(off, tk), :], v_ref[pl.ds(off, tk), :], carry)
        carry = lax.fori_loop(0, n_lat, body, carry, unroll=4)
    _, l, acc = carry
    o = (acc * (1.0 / l)).T
    if mode == "diff":
        lp = lam_ref[...]
        lam = (jnp.exp(jnp.sum(lp[0:1] * lp[1:2], axis=1, keepdims=True))
               - jnp.exp(jnp.sum(lp[2:3] * lp[3:4], axis=1, keepdims=True)) + lam_init)
        dlt = o[:tq] - lam * o[tq:]
        y = dlt * lax.rsqrt(jnp.mean(dlt * dlt, axis=-1, keepdims=True) + EPS) * sg_ref[...]
        o_ref[...] = (y * (1.0 - lam_init)).astype(BF16)
    else:
        for g in range(GQA_GROUP):
            o_ref[:, g * HEAD_DIM:(g + 1) * HEAD_DIM] = o[g * tq:(g + 1) * tq].astype(BF16)


def _flash_call(mode, qsrc, ksrc, pc, offs, n_heads, lam_p, subln, lam_init, tq, name, bound=None):
    sq = qsrc.shape[0]
    t = pc.shape[0]
    qo, ko, vo = offs
    tk = 2048
    qw = HEAD_DIM if mode == "diff" else GQA_GROUP * HEAD_DIM
    qb, kb, vb = qo // qw, ko // HEAD_DIM, vo // HEAD_DIM
    in_specs = [pl.BlockSpec((tq, qw), lambda h, i: (i, qb + h)),
                pl.BlockSpec((t, HEAD_DIM), lambda h, i: (0, kb + h)),
                pl.BlockSpec((t, HEAD_DIM), lambda h, i: (0, vb + h))]
    args = [qsrc, pc, pc]
    if bound is not None:
        in_specs = [pl.BlockSpec(memory_space=pltpu.SMEM)] + in_specs
        args = [bound] + args
    n_lat = 0
    if ksrc is not None:
        s = ksrc.shape[0]
        tk = min(tk, s)
        assert s % tk == 0
        n_lat = s // tk
        in_specs += [pl.BlockSpec((s, HEAD_DIM), lambda h, i: (0, kb + h)),
                     pl.BlockSpec((s, HEAD_DIM), lambda h, i: (0, vb + h))]
        args += [ksrc, ksrc]
    if mode == "diff":
        in_specs += [pl.BlockSpec(lam_p.shape, lambda h, i: (0, 0)),
                     pl.BlockSpec((1, HEAD_DIM), lambda h, i: (0, 0))]
        args += [lam_p, subln.reshape(1, HEAD_DIM)]
    return pl.pallas_call(
        functools.partial(_flash_kernel, mode=mode, tq=tq, tk=tk, n_lat=n_lat, lam_init=lam_init,
                          bounded=bound is not None),
        grid=(n_heads, sq // tq),
        in_specs=in_specs,
        out_specs=pl.BlockSpec((tq, qw), lambda h, i: (i, h)),
        out_shape=jax.ShapeDtypeStruct((sq, n_heads * qw), BF16),
        compiler_params=_params(("parallel", "parallel")),
        name=name,
    )(*args)


def _win_kernel(sink_ref, q_ref, k_ref, v_ref, kc_ref, vc_ref, o_ref, *, tq, s_len):
    kvh = pl.program_id(0)
    q0 = pl.program_id(1) * tq
    band = tq + 2 * C_WINDOW
    start = pl.multiple_of(jnp.clip(q0 - C_WINDOW, 0, s_len - band), C_WINDOW)
    kb = k_ref[pl.ds(start, band), :]
    vb = v_ref[pl.ds(start, band), :]
    qs = jnp.concatenate([q_ref[:, g * HEAD_DIM:(g + 1) * HEAD_DIM] for g in range(GQA_GROUP)], axis=0)
    rows = GQA_GROUP * tq
    lane = lax.broadcasted_iota(jnp.int32, (band, rows), 1)
    qpos = q0 + (lane & (tq - 1))
    kpos = start + lax.broadcasted_iota(jnp.int32, (band, rows), 0)
    s_lat = jnp.where(jnp.abs(qpos - kpos) <= C_WINDOW, _dot_nt(kb, qs), NEG_INF)
    s_ctx = _dot_nt(kc_ref[...], qs)
    group = lax.broadcasted_iota(jnp.int32, (1, rows), 1) >> (tq.bit_length() - 1)
    sink = jnp.zeros((1, rows), F32)
    for g in range(GQA_GROUP):
        sink = jnp.where(group == g, sink_ref[kvh * GQA_GROUP + g], sink)
    m = jnp.maximum(jnp.maximum(jnp.max(s_lat, axis=0, keepdims=True),
                                jnp.max(s_ctx, axis=0, keepdims=True)), sink)
    p_lat = jnp.exp(s_lat - m)
    p_ctx = jnp.exp(s_ctx - m)
    l = jnp.sum(p_lat, axis=0, keepdims=True) + jnp.sum(p_ctx, axis=0, keepdims=True) + jnp.exp(sink - m)
    o = ((_dot_tn(vc_ref[...], p_ctx.astype(BF16)) + _dot_tn(vb, p_lat.astype(BF16))) * (1.0 / l)).T
    for g in range(GQA_GROUP):
        o_ref[:, g * HEAD_DIM:(g + 1) * HEAD_DIM] = o[g * tq:(g + 1) * tq].astype(BF16)


def _win_call(p, pc, offs, n_kv, sink, tq):
    s = p.shape[0]
    t = pc.shape[0]
    qo, ko, vo = offs
    qw = GQA_GROUP * HEAD_DIM
    qb, kb, vb = qo // qw, ko // HEAD_DIM, vo // HEAD_DIM
    return pl.pallas_call(
        functools.partial(_win_kernel, tq=tq, s_len=s),
        grid=(n_kv, s // tq),
        in_specs=[pl.BlockSpec(memory_space=pltpu.SMEM),
                  pl.BlockSpec((tq, qw), lambda h, i: (i, qb + h)),
                  pl.BlockSpec((s, HEAD_DIM), lambda h, i: (0, kb + h)),
                  pl.BlockSpec((s, HEAD_DIM), lambda h, i: (0, vb + h)),
                  pl.BlockSpec((t, HEAD_DIM), lambda h, i: (0, kb + h)),
                  pl.BlockSpec((t, HEAD_DIM), lambda h, i: (0, vb + h))],
        out_specs=pl.BlockSpec((tq, qw), lambda h, i: (i, h)),
        out_shape=jax.ShapeDtypeStruct((s, n_kv * qw), BF16),
        compiler_params=_params(("parallel", "parallel")),
        name="window_attn",
    )(sink, p, p, p, pc, pc)


NBR_Q_ROWS = 4
NBR_WIN_ROWS = NBR_Q_ROWS + NA_KH


def _nbr_build_bias(rpb_ref, head, mats_scr, tab_scr, n_rows):
    na, nb = 2 * NA_KH - 1, 2 * NA_KW - 1
    shape = (GRID_W, 2 * GRID_W)
    wk = lax.broadcasted_iota(jnp.int32, shape, 0)
    lane = lax.broadcasted_iota(jnp.int32, shape, 1)
    wq = lane & (GRID_W - 1)
    dc = jnp.clip(wk - wq, -(NA_KW - 1), NA_KW - 1) + (NA_KW - 1)
    c_start = jnp.clip(wq - NA_KW // 2, 0, GRID_W - NA_KW)
    col_ok = jnp.logical_and(wk >= c_start, wk < c_start + NA_KW)

    def build(a, carry):
        base = (head * na + a) * nb
        mat = jnp.zeros(shape, F32)
        for b in range(nb):
            mat = jnp.where(dc == b, rpb_ref[base + b], mat)
        mats_scr[a] = jnp.where(col_ok, mat, NEG_INF)
        return carry

    lax.fori_loop(0, na, build, 0)
    neg = jnp.full(shape, NEG_INF, F32)
    first_half = lane < GRID_W
    for cl, r0 in enumerate((0, NBR_Q_ROWS, n_rows - NBR_Q_ROWS)):
        ws = min(max(r0 - NA_KH // 2, 0), n_rows - NBR_WIN_ROWS)
        for kr in range(NBR_WIN_ROWS):
            for qp in range(NBR_Q_ROWS // 2):
                halves = []
                for r in (r0 + 2 * qp, r0 + 2 * qp + 1):
                    r_start = min(max(r - NA_KH // 2, 0), n_rows - NA_KH)
                    inside = r_start <= ws + kr < r_start + NA_KH
                    halves.append(mats_scr[ws + kr - r + NA_KH - 1] if inside else neg)
                tab_scr[cl, kr * GRID_W:(kr + 1) * GRID_W, qp * 2 * GRID_W:(qp + 1) * 2 * GRID_W] = (
                    jnp.where(first_half, halves[0], halves[1]))


NBR_HEADS_PER_STEP = 2


def _nbr_kernel(rpb_ref, q_ref, k_ref, v_ref, kc_ref, vc_ref, o_ref, mats_scr, tab_scr, *, tq, win, s_len):
    i = pl.program_id(1)
    nq = pl.num_programs(1)

    @pl.when(i == 0)
    def _():
        for hh in range(NBR_HEADS_PER_STEP):
            _nbr_build_bias(rpb_ref, pl.program_id(0) * NBR_HEADS_PER_STEP + hh, mats_scr, tab_scr.at[hh],
                            s_len // GRID_W)

    q0 = i * tq
    lead = (NA_KH // 2) * GRID_W
    start = pl.multiple_of(jnp.clip(q0 - lead, 0, s_len - win), lead)
    cls = jnp.where(i == 0, 0, jnp.where(i == nq - 1, 2, 1))
    for hh in range(NBR_HEADS_PER_STEP):
        sl = slice(hh * HEAD_DIM, (hh + 1) * HEAD_DIM)
        q = q_ref[:, sl]
        s_win = _dot_nt(k_ref[pl.ds(start, win), sl], q) + tab_scr[hh, cls]
        s_ctx = _dot_nt(kc_ref[:, sl], q)
        m = jnp.maximum(jnp.max(s_win, axis=0, keepdims=True), jnp.max(s_ctx, axis=0, keepdims=True))
        p_win = jnp.exp(s_win - m)
        p_ctx = jnp.exp(s_ctx - m)
        l = jnp.sum(p_win, axis=0, keepdims=True) + jnp.sum(p_ctx, axis=0, keepdims=True)
        o = (_dot_tn(vc_ref[:, sl], p_ctx.astype(BF16))
             + _dot_tn(v_ref[pl.ds(start, win), sl], p_win.astype(BF16)))
        o_ref[:, sl] = ((o * (1.0 / l)).T).astype(BF16)


def _nbr_call(p, pc, offs, n_heads, rpb):
    s = p.shape[0]
    t = pc.shape[0]
    tq = NBR_Q_ROWS * GRID_W
    win = NBR_WIN_ROWS * GRID_W
    bw = NBR_HEADS_PER_STEP * HEAD_DIM
    qb, kb, vb = (o // bw for o in offs)
    assert s // GRID_W >= NBR_WIN_ROWS and rpb.shape[1:] == (2 * NA_KH - 1, 2 * NA_KW - 1)
    assert n_heads % NBR_HEADS_PER_STEP == 0 and all(o % bw == 0 for o in offs)
    return pl.pallas_call(
        functools.partial(_nbr_kernel, tq=tq, win=win, s_len=s),
        grid=(n_heads // NBR_HEADS_PER_STEP, s // tq),
        in_specs=[pl.BlockSpec(memory_space=pltpu.SMEM),
                  pl.BlockSpec((tq, bw), lambda h, i: (i, qb + h)),
                  pl.BlockSpec((s, bw), lambda h, i: (0, kb + h)),
                  pl.BlockSpec((s, bw), lambda h, i: (0, vb + h)),
                  pl.BlockSpec((t, bw), lambda h, i: (0, kb + h)),
                  pl.BlockSpec((t, bw), lambda h, i: (0, vb + h))],
        out_specs=pl.BlockSpec((tq, bw), lambda h, i: (i, h)),
        out_shape=jax.ShapeDtypeStruct((s, n_heads * HEAD_DIM), BF16),
        scratch_shapes=[pltpu.VMEM((2 * NA_KH - 1, GRID_W, 2 * GRID_W), F32),
                        pltpu.VMEM((NBR_HEADS_PER_STEP, 3, win, tq), F32)],
        compiler_params=_params(("parallel", "arbitrary")),
        name="nbr_attn",
    )(rpb.astype(F32).reshape(-1), p, p, p, pc, pc)


def _outproj_kernel(x_ref, ya_ref, yb_ref, wa_ref, wb_ref, g_ref, o_ref):
    acc = _dot(ya_ref[...], wa_ref[...]) + _dot(yb_ref[...], wb_ref[...])
    o_ref[...] = x_ref[...] + g_ref[...] * acc


def _outproj_call(x, ya, yb, w, layer, gate, tm, name):
    m, d = x.shape
    k = ya.shape[1]
    assert yb.shape[1] == k and w.shape[1] == 2 * k
    return pl.pallas_call(
        _outproj_kernel,
        grid=(m // tm,),
        in_specs=[pl.BlockSpec((tm, d), lambda i: (i, 0)),
                  pl.BlockSpec((tm, k), lambda i: (i, 0)),
                  pl.BlockSpec((tm, k), lambda i: (i, 0)),
                  pl.BlockSpec((None, k, d), lambda i: (layer, 0, 0)),
                  pl.BlockSpec((None, k, d), lambda i: (layer, 1, 0)),
                  pl.BlockSpec((1, d), lambda i: (0, 0))],
        out_specs=pl.BlockSpec((tm, d), lambda i: (i, 0)),
        out_shape=jax.ShapeDtypeStruct((m, d), F32),
        compiler_params=_params(("parallel",)),
        name=name,
    )(x, ya, yb, w, w, gate)


HALO = BF16_SUBLANES


def _ffn_kernel(xm_ref, xp_ref, xn_ref, gn_ref, sh_ref, sc_ref, gate_ref, wa_ref, wg_ref,
                cwa_ref, cwg_ref, cba_ref, cbg_ref, wd_ref, o_ref, h_scr, acc_scr, ua_scr, ug_scr):
    i, j = pl.program_id(0), pl.program_id(1)
    ni, nj = pl.num_programs(0), pl.num_programs(1)
    tm = xm_ref.shape[0]

    @pl.when(j == 0)
    def _():
        nm = lambda x: _norm_mod(x, gn_ref[...], sh_ref[...], sc_ref[...])
        h_scr[0:HALO, :] = jnp.where(i > 0, nm(xp_ref[...]), 0.0).astype(BF16)
        h_scr[HALO:HALO + tm, :] = nm(xm_ref[...]).astype(BF16)
        h_scr[HALO + tm:, :] = jnp.where(i < ni - 1, nm(xn_ref[...]), 0.0).astype(BF16)
        acc_scr[...] = jnp.zeros_like(acc_scr)

    h = h_scr[...]
    ua_scr[...] = _dot(h, wa_ref[...])
    ug_scr[...] = _dot(h, wg_ref[...])

    def conv(u_scr, cw_ref, cb_ref):
        out = cb_ref[...] + u_scr[pl.ds(HALO - 1, tm), :] * cw_ref[0:1, :]
        for t in range(1, CONV_W):
            out = out + u_scr[pl.ds(HALO - 1 + t, tm), :] * cw_ref[t:t + 1, :]
        return out

    a = conv(ua_scr, cwa_ref, cba_ref)
    g = conv(ug_scr, cwg_ref, cbg_ref)
    z = (g * (1.0 / (1.0 + jnp.exp(-g)))) * a
    acc_scr[...] += _dot(z.astype(BF16), wd_ref[...])

    @pl.when(j == nj - 1)
    def _():
        o_ref[...] = xm_ref[...] + gate_ref[...] * acc_scr[...]


def _ffn_call(x, gn, sh, sc, gate, w_up, conv_w, conv_b, w_down, layer, tm, tf, name):
    m, d = x.shape
    f = w_down.shape[1]
    nfb = f // tf
    hb = tm // HALO
    last = m // HALO - 1
    row = lambda i, j: (0, 0)
    return pl.pallas_call(
        _ffn_kernel,
        grid=(m // tm, nfb),
        in_specs=[pl.BlockSpec((tm, d), lambda i, j: (i, 0)),
                  pl.BlockSpec((HALO, d), lambda i, j: (jnp.maximum(i * hb - 1, 0), 0)),
                  pl.BlockSpec((HALO, d), lambda i, j: (jnp.minimum((i + 1) * hb, last), 0)),
                  pl.BlockSpec((1, d), row), pl.BlockSpec((1, d), row), pl.BlockSpec((1, d), row),
                  pl.BlockSpec((1, d), row),
                  pl.BlockSpec((None, d, tf), lambda i, j: (layer, 0, j)),
                  pl.BlockSpec((None, d, tf), lambda i, j: (layer, 0, nfb + j)),
                  pl.BlockSpec((None, CONV_W, tf), lambda i, j: (layer, 0, j)),
                  pl.BlockSpec((None, CONV_W, tf), lambda i, j: (layer, 0, nfb + j)),
                  pl.BlockSpec((None, 1, tf), lambda i, j: (layer, 0, j)),
                  pl.BlockSpec((None, 1, tf), lambda i, j: (layer, 0, nfb + j)),
                  pl.BlockSpec((None, tf, d), lambda i, j: (layer, j, 0))],
        out_specs=pl.BlockSpec((tm, d), lambda i, j: (i, 0)),
        out_shape=jax.ShapeDtypeStruct((m, d), F32),
        scratch_shapes=[pltpu.VMEM((tm + 2 * HALO, d), BF16),
                        pltpu.VMEM((tm, d), F32),
                        pltpu.VMEM((tm + 2 * HALO, tf), F32),
                        pltpu.VMEM((tm + 2 * HALO, tf), F32)],
        compiler_params=_params(("parallel", "arbitrary")),
        name=name,
    )(x, x, x, gn, sh, sc, gate, w_up, w_up, conv_w, conv_w, conv_b, conv_b, w_down)


def _offsets(sizes):
    offs, o = [], 0
    for s in sizes:
        offs.append(o)
        o += s
    return offs


def _chunk_classes(sizes, classes):
    out = []
    for s, c in zip(sizes, classes):
        out += [c] * (s // EPI_CHUNK)
    return out


def _gain_vector(sizes, gains):
    parts = []
    for s, g in zip(sizes, gains):
        parts.append(jnp.ones((s,), F32) if g is None else jnp.tile(g.astype(F32), s // g.shape[0]))
    return jnp.concatenate(parts).reshape(1, -1)


def _lambda_init(layer_idx):
    return 0.8 - 0.6 * math.exp(-0.3 * layer_idx)


def _forward(x, c, ctx, c_ctx, w_ada, b_ada, norm1_g, w_in, w_out, a_qk_g, a_lambda, a_subln_g,
             b_qk_g, c_qk_g, c_sink, d_qk_g, d_rpb, norm2_g, w_up, conv_w, conv_b, w_down):
    s, d = x.shape
    t = ctx.shape[0]
    depth = w_ada.shape[0]
    n_heads = d // (2 * HEAD_DIM)
    n_kv = n_heads // GQA_GROUP
    hw = n_heads * HEAD_DIM
    kvw = n_kv * HEAD_DIM
    even_sizes = (hw, hw, hw, hw, kvw, kvw)
    odd_sizes = (hw, kvw, kvw, hw, hw, hw)
    tm = 512
    tf = 512

    mod = _mod_call(c, c_ctx, w_ada, b_ada)
    rope_a = _rope_tables(s, 16)
    rope_b = _rope_tables(s, 32)
    tabs = rope_a + rope_b
    scale_a = A_SUB ** -0.5
    scale = HEAD_DIM ** -0.5

    w_in_b, w_out_b, w_up_b, w_down_b = (w.astype(BF16) for w in (w_in, w_out, w_up, w_down))
    conv_b3 = conv_b.reshape(depth, 1, -1)

    xc = ctx
    for i in range(depth):
        need_ctx = i < depth - 1
        e = i // 2
        vec = lambda r, k: mod[i, r:r + 1, k * d:(k + 1) * d]
        g1n = norm1_g[i].reshape(1, d)
        g2n = norm2_g[i].reshape(1, d)
        if i % 2 == 0:
            sizes = even_sizes
            classes = [(A_SUB, "A"), (A_SUB, "A"), (0, None), (HEAD_DIM, "B"), (HEAD_DIM, "B"), (0, None)]
            gains = [a_qk_g[e, 0] * (scale_a * LOG2E), a_qk_g[e, 1], None,
                     b_qk_g[e, 0] * (scale * LOG2E), b_qk_g[e, 1], None]
        else:
            sizes = odd_sizes
            classes = [(HEAD_DIM, "B"), (HEAD_DIM, "B"), (0, None), (HEAD_DIM, None), (HEAD_DIM, None), (0, None)]
            gains = [c_qk_g[e, 0] * scale, c_qk_g[e, 1], None, d_qk_g[e, 0] * scale, d_qk_g[e, 1], None]
        offs = _offsets(sizes)
        chunk_cls = _chunk_classes(sizes, classes)
        gain = _gain_vector(sizes, gains)

        p = _inproj_call(x, g1n, vec(0, 0), vec(0, 1), w_in_b, i, gain, chunk_cls, tabs, tm, f"inproj{i}")
        pc = _inproj_call(xc, g1n, vec(1, 0), vec(1, 1), w_in_b, i, gain, chunk_cls, None, t, f"inproj_ctx{i}")

        if i % 2 == 0:
            lam_init = _lambda_init(i)
            fa = functools.partial(_flash_call, "diff", offs=offs[0:3], n_heads=n_heads, lam_p=a_lambda[e],
                                   subln=a_subln_g[e], lam_init=lam_init, tq=256)
            fb = functools.partial(_flash_call, "gqa", offs=offs[3:6], n_heads=n_kv, lam_p=None,
                                   subln=None, lam_init=None)
            bound_a = (A_SUB * LOGIT_BOUND_SLACK) * jnp.max(jnp.abs(gains[0])) * jnp.max(jnp.abs(gains[1]))
            bound_b = (HEAD_DIM * LOGIT_BOUND_SLACK) * jnp.max(jnp.abs(gains[3])) * jnp.max(jnp.abs(gains[4]))
            ya = lax.cond(bound_a <= MAX_LOGIT_BOUND,
                          lambda: fa(p, p, pc, name=f"diff_attn{i}", bound=bound_a.reshape(1)),
                          lambda: fa(p, p, pc, name=f"diff_attn_online{i}"))
            yb = lax.cond(bound_b <= MAX_LOGIT_BOUND,
                          lambda: fb(p, p, pc, tq=128, name=f"gqa_attn{i}", bound=bound_b.reshape(1)),
                          lambda: fb(p, p, pc, tq=128, name=f"gqa_attn_online{i}"))
            if need_ctx:
                ya_c = fa(pc, None, pc, name=f"diff_attn_ctx{i}")
                yb_c = fb(pc, None, pc, tq=t, name=f"gqa_attn_ctx{i}")
        else:
            ya = _win_call(p, pc, offs[0:3], n_kv, c_sink[e], 256)
            yb = _nbr_call(p, pc, offs[3:6], n_heads, d_rpb[e])
            if need_ctx:
                raise NotImplementedError("context stream after an odd layer")

        x = _outproj_call(x, ya, yb, w_out_b, i, vec(0, 2), tm, f"outproj{i}")
        x = _ffn_call(x, g2n, vec(0, 3), vec(0, 4), vec(0, 5), w_up_b, conv_w, conv_b3, w_down_b, i, tm, tf,
                      f"ffn{i}")
        if need_ctx:
            xc = _outproj_call(xc, ya_c, yb_c, w_out_b, i, vec(1, 2), t, f"outproj_ctx{i}")
            xc = _ffn_call(xc, g2n, vec(1, 3), vec(1, 4), vec(1, 5), w_up_b, conv_w, conv_b3, w_down_b, i, t, tf,
                           f"ffn_ctx{i}")
    return x


def kernel(x, c, ctx, c_ctx, w_ada, b_ada, norm1_g, w_in, w_out, a_qk_g, a_lambda, a_subln_g, b_qk_g, c_qk_g,
           c_sink, d_qk_g, d_rpb, norm2_g, w_up, conv_w, conv_b, w_down):
    outs = [_forward(x[b], c[b], ctx[b], c_ctx, w_ada, b_ada, norm1_g, w_in, w_out, a_qk_g, a_lambda, a_subln_g,
                     b_qk_g, c_qk_g, c_sink, d_qk_g, d_rpb, norm2_g, w_up, conv_w, conv_b, w_down)
            for b in range(x.shape[0])]
    return jnp.stack(outs)
```

```python
import functools
import math

import jax
import jax.numpy as jnp
from jax import lax
from jax.experimental import pallas as pl
from jax.experimental.pallas import tpu as pltpu

GRID_W = 64
HEAD_DIM = 128
A_SUB = HEAD_DIM // 2
ROPE_THETA = 10000.0
EPS = 1e-6
NEG_INF = -1e30
N_MOD = 6
C_WINDOW = 128
NA_KH = 8
NA_KW = 16
CONV_W = 3
GQA_GROUP = 4
LOG2E = math.log2(math.e)
LOGIT_BOUND_SLACK = 1.02
MAX_LOGIT_BOUND = 50.0

LANES = 128
BF16_SUBLANES = 16
VMEM_LIMIT = 56 * 1024 * 1024

F32 = jnp.float32
BF16 = jnp.bfloat16


def _params(sem, vmem=VMEM_LIMIT):
    return pltpu.CompilerParams(dimension_semantics=sem, vmem_limit_bytes=vmem)


def _dot(a, b):
    return jnp.dot(a, b, preferred_element_type=F32)


def _dot_nt(a, b):
    return lax.dot_general(a, b, (((1,), (1,)), ((), ())), preferred_element_type=F32)


def _dot_tn(a, b):
    return lax.dot_general(a, b, (((0,), (0,)), ((), ())), preferred_element_type=F32)


def _norm_mod(x, g, sh, sc):
    inv = lax.rsqrt(jnp.mean(x * x, axis=-1, keepdims=True) + EPS)
    return (x * inv * g) * (1.0 + sc) + sh


def _mod_kernel(cb_ref, w_ref, b_ref, o_ref):
    tn = w_ref.shape[2]
    for r in range(2):
        cv = cb_ref[r]
        a = cv * (1.0 / (1.0 + jnp.exp(-cv)))
        for ch in range(tn // LANES):
            sl = slice(ch * LANES, (ch + 1) * LANES)
            o_ref[0, r:r + 1, sl] = jnp.sum(w_ref[0, :, sl] * a, axis=0, keepdims=True) + b_ref[0, :, sl]


def _mod_call(c, c_ctx, w_ada, b_ada):
    depth, d, n = w_ada.shape
    tn = 1024
    cb = jnp.broadcast_to(jnp.stack([c, c_ctx])[:, :, None], (2, d, LANES))
    return pl.pallas_call(
        _mod_kernel,
        grid=(depth, n // tn),
        in_specs=[pl.BlockSpec((2, d, LANES), lambda i, j: (0, 0, 0)),
                  pl.BlockSpec((1, d, tn), lambda i, j: (i, 0, j)),
                  pl.BlockSpec((1, 1, tn), lambda i, j: (i, 0, j))],
        out_specs=pl.BlockSpec((1, 2, tn), lambda i, j: (i, 0, j)),
        out_shape=jax.ShapeDtypeStruct((depth, 2, n), F32),
        compiler_params=_params(("parallel", "parallel")),
        name="adaln_mod",
    )(cb, w_ada, b_ada.reshape(depth, 1, n))


EPI_CHUNK = 256


def _head_norm(p, gain, width):
    shift = int(math.log2(width))
    r = lax.broadcasted_iota(jnp.int32, (EPI_CHUNK, EPI_CHUNK), 0) >> shift
    c = lax.broadcasted_iota(jnp.int32, (EPI_CHUNK, EPI_CHUNK), 1) >> shift
    bd = jnp.where(r == c, 1.0, 0.0).astype(BF16)
    seg = _dot((p * p).astype(BF16), bd)
    return p * lax.rsqrt(seg * (1.0 / width) + EPS) * gain


def _rope(y, half, tabs):
    c_ref, s1_ref, s2_ref = tabs
    c, s1, s2 = c_ref[...], s1_ref[...], s2_ref[...]
    outs = []
    for k in range(y.shape[1] // LANES):
        yk = y[:, k * LANES:(k + 1) * LANES]
        outs.append(yk * c + pltpu.roll(yk, LANES - half, 1) * s1 + pltpu.roll(yk, half, 1) * s2)
    return jnp.concatenate(outs, axis=1)


def _inproj_kernel(*refs, classes, rope, tn):
    x_ref, g_ref, sh_ref, sc_ref, w_ref, gain_ref = refs[:6]
    tabs = {"A": refs[6:9], "B": refs[9:12]} if rope else None
    o_ref = refs[-1]
    h = _norm_mod(x_ref[...], g_ref[...], sh_ref[...], sc_ref[...]).astype(BF16)
    per = tn // EPI_CHUNK
    for t in range(len(classes) // per):
        p = _dot(h, w_ref[:, t * tn:(t + 1) * tn])
        for ci, (width, kind) in enumerate(classes[t * per:(t + 1) * per]):
            sl = slice(t * tn + ci * EPI_CHUNK, t * tn + (ci + 1) * EPI_CHUNK)
            y = p[:, ci * EPI_CHUNK:(ci + 1) * EPI_CHUNK]
            if width:
                y = _head_norm(y, gain_ref[:, sl], width)
            if kind and rope:
                y = _rope(y, 16 if kind == "A" else 32, tabs[kind])
            o_ref[:, sl] = y.astype(BF16)


def _inproj_call(x, g, sh, sc, w, layer, gain, classes, tabs, tm, name):
    m, d = x.shape
    n = w.shape[2]
    rope = tabs is not None
    const = lambda i: (0, 0)
    in_specs = [pl.BlockSpec((tm, d), lambda i: (i, 0)),
                pl.BlockSpec((1, d), const), pl.BlockSpec((1, d), const), pl.BlockSpec((1, d), const),
                pl.BlockSpec((None, d, n), lambda i: (layer, 0, 0), pipeline_mode=pl.Buffered(1)),
                pl.BlockSpec((1, n), const)]
    args = [x, g, sh, sc, w, gain]
    if rope:
        in_specs += [pl.BlockSpec((tm, LANES), lambda i: (i, 0))] * 6
        args += list(tabs)
    return pl.pallas_call(
        functools.partial(_inproj_kernel, classes=tuple(classes), rope=rope, tn=2 * EPI_CHUNK),
        grid=(m // tm,),
        in_specs=in_specs,
        out_specs=pl.BlockSpec((tm, n), lambda i: (i, 0)),
        out_shape=jax.ShapeDtypeStruct((m, n), BF16),
        compiler_params=_params(("parallel",)),
        name=name,
    )(*args)


def _rope_tables(s, half):
    n_rows = s // GRID_W
    freqs = ROPE_THETA ** (-jnp.arange(half, dtype=F32) / half)
    zeros = jnp.zeros((s, half), F32)
    cs, s1s, s2s = [], [], []
    for count, expand in ((n_rows, lambda a: jnp.repeat(a, GRID_W, axis=0)),
                          (GRID_W, lambda a: jnp.tile(a, (n_rows, 1)))):
        ang = jnp.arange(count, dtype=jnp.int32).astype(F32)[:, None] * freqs[None, :]
        cos, sin = expand(jnp.cos(ang)), expand(jnp.sin(ang))
        cs += [cos, cos]
        s1s += [-sin, zeros]
        s2s += [zeros, sin]
    reps = LANES // (4 * half)
    return tuple(jnp.tile(jnp.concatenate(t, axis=1), (1, reps)) for t in (cs, s1s, s2s))


def _flash_kernel(*refs, mode, tq, tk, n_lat, lam_init, bounded):
    if bounded:
        bound = refs[0][0]
        refs = refs[1:]
    q_ref, kc_ref, vc_ref = refs[:3]
    pos = 3
    if n_lat:
        k_ref, v_ref = refs[3:5]
        pos = 5
    if mode == "diff":
        lam_ref, sg_ref = refs[pos:pos + 2]
        pos += 2
    o_ref = refs[pos]

    if mode == "diff":
        q = q_ref[...].astype(F32)
        lane = lax.broadcasted_iota(jnp.int32, q.shape, 1)
        qs = jnp.concatenate([jnp.where(lane < A_SUB, q, 0.0), jnp.where(lane >= A_SUB, q, 0.0)],
                             axis=0).astype(BF16)
    else:
        qs = jnp.concatenate([q_ref[:, g * HEAD_DIM:(g + 1) * HEAD_DIM] for g in range(GQA_GROUP)], axis=0)
    rows = qs.shape[0]

    def pv_dot(v, p):
        return lax.dot_general(v, p.astype(BF16), (((0,), (0,)), ((), ())), preferred_element_type=F32)

    if bounded:
        def step(k, v, carry):
            _, l, acc = carry
            p = jnp.exp2(_dot_nt(k, qs) - bound)
            return carry[0], l + jnp.sum(p, axis=0, keepdims=True), acc + pv_dot(v, p)
    else:
        def step(k, v, carry):
            m, l, acc = carry
            s = _dot_nt(k, qs)
            m_new = jnp.maximum(m, jnp.max(s, axis=0, keepdims=True))
            p = jnp.exp2(s - m_new)
            alpha = jnp.exp2(m - m_new)
            return m_new, alpha * l + jnp.sum(p, axis=0, keepdims=True), alpha * acc + pv_dot(v, p)

    carry = (jnp.full((1, rows), NEG_INF, F32), jnp.zeros((1, rows), F32), jnp.zeros((HEAD_DIM, rows), F32))
    carry = step(kc_ref[...], vc_ref[...], carry)
    if n_lat:
        def body(c, carry):
            off = pl.multiple_of(c * tk, tk)
            return step(k_ref[pl.ds(off, tk), :], v_ref[pl.ds(off, tk), :], carry)
        carry = lax.fori_loop(0, n_lat, body, carry, unroll=8)
    _, l, acc = carry
    o = (acc * (1.0 / l)).T
    if mode == "diff":
        lp = lam_ref[...]
        lam = (jnp.exp(jnp.sum(lp[0:1] * lp[1:2], axis=1, keepdims=True))
               - jnp.exp(jnp.sum(lp[2:3] * lp[3:4], axis=1, keepdims=True)) + lam_init)
        dlt = o[:tq] - lam * o[tq:]
        y = dlt * lax.rsqrt(jnp.mean(dlt * dlt, axis=-1, keepdims=True) + EPS) * sg_ref[...]
        o_ref[...] = (y * (1.0 - lam_init)).astype(BF16)
    else:
        for g in range(GQA_GROUP):
            o_ref[:, g * HEAD_DIM:(g + 1) * HEAD_DIM] = o[g * tq:(g + 1) * tq].astype(BF16)


def _flash_call(mode, qsrc, ksrc, pc, offs, n_heads, lam_p, subln, lam_init, tq, name, bound=None):
    sq = qsrc.shape[0]
    t = pc.shape[0]
    qo, ko, vo = offs
    tk = 1024
    qw = HEAD_DIM if mode == "diff" else GQA_GROUP * HEAD_DIM
    qb, kb, vb = qo // qw, ko // HEAD_DIM, vo // HEAD_DIM
    in_specs = [pl.BlockSpec((tq, qw), lambda h, i: (i, qb + h)),
                pl.BlockSpec((t, HEAD_DIM), lambda h, i: (0, kb + h)),
                pl.BlockSpec((t, HEAD_DIM), lambda h, i: (0, vb + h))]
    args = [qsrc, pc, pc]
    if bound is not None:
        in_specs = [pl.BlockSpec(memory_space=pltpu.SMEM)] + in_specs
        args = [bound] + args
    n_lat = 0
    if ksrc is not None:
        s = ksrc.shape[0]
        tk = min(tk, s)
        assert s % tk == 0
        n_lat = s // tk
        in_specs += [pl.BlockSpec((s, HEAD_DIM), lambda h, i: (0, kb + h)),
                     pl.BlockSpec((s, HEAD_DIM), lambda h, i: (0, vb + h))]
        args += [ksrc, ksrc]
    if mode == "diff":
        in_specs += [pl.BlockSpec(lam_p.shape, lambda h, i: (0, 0)),
                     pl.BlockSpec((1, HEAD_DIM), lambda h, i: (0, 0))]
        args += [lam_p, subln.reshape(1, HEAD_DIM)]
    return pl.pallas_call(
        functools.partial(_flash_kernel, mode=mode, tq=tq, tk=tk, n_lat=n_lat, lam_init=lam_init,
                          bounded=bound is not None),
        grid=(n_heads, sq // tq),
        in_specs=in_specs,
        out_specs=pl.BlockSpec((tq, qw), lambda h, i: (i, h)),
        out_shape=jax.ShapeDtypeStruct((sq, n_heads * qw), BF16),
        compiler_params=_params(("parallel", "parallel")),
        name=name,
    )(*args)


def _win_kernel(sink_ref, q_ref, k_ref, v_ref, kc_ref, vc_ref, o_ref, *, tq, s_len):
    kvh = pl.program_id(0)
    q0 = pl.program_id(1) * tq
    band = tq + 2 * C_WINDOW
    start = pl.multiple_of(jnp.clip(q0 - C_WINDOW, 0, s_len - band), C_WINDOW)
    kb = k_ref[pl.ds(start, band), :]
    vb = v_ref[pl.ds(start, band), :]
    qs = jnp.concatenate([q_ref[:, g * HEAD_DIM:(g + 1) * HEAD_DIM] for g in range(GQA_GROUP)], axis=0)
    rows = GQA_GROUP * tq
    lane = lax.broadcasted_iota(jnp.int32, (band, rows), 1)
    qpos = q0 + (lane & (tq - 1))
    kpos = start + lax.broadcasted_iota(jnp.int32, (band, rows), 0)
    s_lat = jnp.where(jnp.abs(qpos - kpos) <= C_WINDOW, _dot_nt(kb, qs), NEG_INF)
    s_ctx = _dot_nt(kc_ref[...], qs)
    group = lax.broadcasted_iota(jnp.int32, (1, rows), 1) >> (tq.bit_length() - 1)
    sink = jnp.zeros((1, rows), F32)
    for g in range(GQA_GROUP):
        sink = jnp.where(group == g, sink_ref[kvh * GQA_GROUP + g], sink)
    m = jnp.maximum(jnp.maximum(jnp.max(s_lat, axis=0, keepdims=True),
                                jnp.max(s_ctx, axis=0, keepdims=True)), sink)
    p_lat = jnp.exp(s_lat - m)
    p_ctx = jnp.exp(s_ctx - m)
    l = jnp.sum(p_lat, axis=0, keepdims=True) + jnp.sum(p_ctx, axis=0, keepdims=True) + jnp.exp(sink - m)
    o = ((_dot_tn(vc_ref[...], p_ctx.astype(BF16)) + _dot_tn(vb, p_lat.astype(BF16))) * (1.0 / l)).T
    for g in range(GQA_GROUP):
        o_ref[:, g * HEAD_DIM:(g + 1) * HEAD_DIM] = o[g * tq:(g + 1) * tq].astype(BF16)


def _win_call(p, pc, offs, n_kv, sink, tq):
    s = p.shape[0]
    t = pc.shape[0]
    qo, ko, vo = offs
    qw = GQA_GROUP * HEAD_DIM
    qb, kb, vb = qo // qw, ko // HEAD_DIM, vo // HEAD_DIM
    return pl.pallas_call(
        functools.partial(_win_kernel, tq=tq, s_len=s),
        grid=(n_kv, s // tq),
        in_specs=[pl.BlockSpec(memory_space=pltpu.SMEM),
                  pl.BlockSpec((tq, qw), lambda h, i: (i, qb + h)),
                  pl.BlockSpec((s, HEAD_DIM), lambda h, i: (0, kb + h)),
                  pl.BlockSpec((s, HEAD_DIM), lambda h, i: (0, vb + h)),
                  pl.BlockSpec((t, HEAD_DIM), lambda h, i: (0, kb + h)),
                  pl.BlockSpec((t, HEAD_DIM), lambda h, i: (0, vb + h))],
        out_specs=pl.BlockSpec((tq, qw), lambda h, i: (i, h)),
        out_shape=jax.ShapeDtypeStruct((s, n_kv * qw), BF16),
        compiler_params=_params(("parallel", "parallel")),
        name="window_attn",
    )(sink, p, p, p, pc, pc)


NBR_Q_ROWS = 4
NBR_WIN_ROWS = NBR_Q_ROWS + NA_KH


def _nbr_build_bias(rpb_ref, head, mats_scr, tab_scr, n_rows):
    na, nb = 2 * NA_KH - 1, 2 * NA_KW - 1
    shape = (GRID_W, 2 * GRID_W)
    wk = lax.broadcasted_iota(jnp.int32, shape, 0)
    lane = lax.broadcasted_iota(jnp.int32, shape, 1)
    wq = lane & (GRID_W - 1)
    dc = jnp.clip(wk - wq, -(NA_KW - 1), NA_KW - 1) + (NA_KW - 1)
    c_start = jnp.clip(wq - NA_KW // 2, 0, GRID_W - NA_KW)
    col_ok = jnp.logical_and(wk >= c_start, wk < c_start + NA_KW)

    def build(a, carry):
        base = (head * na + a) * nb
        mat = jnp.zeros(shape, F32)
        for b in range(nb):
            mat = jnp.where(dc == b, rpb_ref[base + b], mat)
        mats_scr[a] = jnp.where(col_ok, mat, NEG_INF)
        return carry

    lax.fori_loop(0, na, build, 0)
    neg = jnp.full(shape, NEG_INF, F32)
    first_half = lane < GRID_W
    for cl, r0 in enumerate((0, NBR_Q_ROWS, n_rows - NBR_Q_ROWS)):
        ws = min(max(r0 - NA_KH // 2, 0), n_rows - NBR_WIN_ROWS)
        for kr in range(NBR_WIN_ROWS):
            for qp in range(NBR_Q_ROWS // 2):
                halves = []
                for r in (r0 + 2 * qp, r0 + 2 * qp + 1):
                    r_start = min(max(r - NA_KH // 2, 0), n_rows - NA_KH)
                    inside = r_start <= ws + kr < r_start + NA_KH
                    halves.append(mats_scr[ws + kr - r + NA_KH - 1] if inside else neg)
                tab_scr[cl, kr * GRID_W:(kr + 1) * GRID_W, qp * 2 * GRID_W:(qp + 1) * 2 * GRID_W] = (
                    jnp.where(first_half, halves[0], halves[1]))


NBR_HEADS_PER_STEP = 4


def _nbr_kernel(rpb_ref, q_ref, k_ref, v_ref, kc_ref, vc_ref, o_ref, mats_scr, tab_scr, *, tq, win, s_len):
    i = pl.program_id(1)
    nq = pl.num_programs(1)

    @pl.when(i == 0)
    def _():
        for hh in range(NBR_HEADS_PER_STEP):
            _nbr_build_bias(rpb_ref, pl.program_id(0) * NBR_HEADS_PER_STEP + hh, mats_scr, tab_scr.at[hh],
                            s_len // GRID_W)

    q0 = i * tq
    lead = (NA_KH // 2) * GRID_W
    start = pl.multiple_of(jnp.clip(q0 - lead, 0, s_len - win), lead)
    cls = jnp.where(i == 0, 0, jnp.where(i == nq - 1, 2, 1))
    for hh in range(NBR_HEADS_PER_STEP):
        sl = slice(hh * HEAD_DIM, (hh + 1) * HEAD_DIM)
        q = q_ref[:, sl]
        s_win = _dot_nt(k_ref[pl.ds(start, win), sl], q) + tab_scr[hh, cls]
        s_ctx = _dot_nt(kc_ref[:, sl], q)
        m = jnp.maximum(jnp.max(s_win, axis=0, keepdims=True), jnp.max(s_ctx, axis=0, keepdims=True))
        p_win = jnp.exp(s_win - m)
        p_ctx = jnp.exp(s_ctx - m)
        l = jnp.sum(p_win, axis=0, keepdims=True) + jnp.sum(p_ctx, axis=0, keepdims=True)
        o = (_dot_tn(vc_ref[:, sl], p_ctx.astype(BF16))
             + _dot_tn(v_ref[pl.ds(start, win), sl], p_win.astype(BF16)))
        o_ref[:, sl] = ((o * (1.0 / l)).T).astype(BF16)


def _nbr_call(p, pc, offs, n_heads, rpb):
    s = p.shape[0]
    t = pc.shape[0]
    tq = NBR_Q_ROWS * GRID_W
    win = NBR_WIN_ROWS * GRID_W
    bw = NBR_HEADS_PER_STEP * HEAD_DIM
    qb, kb, vb = (o // bw for o in offs)
    assert s // GRID_W >= NBR_WIN_ROWS and rpb.shape[1:] == (2 * NA_KH - 1, 2 * NA_KW - 1)
    assert n_heads % NBR_HEADS_PER_STEP == 0 and all(o % bw == 0 for o in offs)
    return pl.pallas_call(
        functools.partial(_nbr_kernel, tq=tq, win=win, s_len=s),
        grid=(n_heads // NBR_HEADS_PER_STEP, s // tq),
        in_specs=[pl.BlockSpec(memory_space=pltpu.SMEM),
                  pl.BlockSpec((tq, bw), lambda h, i: (i, qb + h)),
                  pl.BlockSpec((s, bw), lambda h, i: (0, kb + h)),
                  pl.BlockSpec((s, bw), lambda h, i: (0, vb + h)),
                  pl.BlockSpec((t, bw), lambda h, i: (0, kb + h)),
                  pl.BlockSpec((t, bw), lambda h, i: (0, vb + h))],
        out_specs=pl.BlockSpec((tq, bw), lambda h, i: (i, h)),
        out_shape=jax.ShapeDtypeStruct((s, n_heads * HEAD_DIM), BF16),
        scratch_shapes=[pltpu.VMEM((2 * NA_KH - 1, GRID_W, 2 * GRID_W), F32),
                        pltpu.VMEM((NBR_HEADS_PER_STEP, 3, win, tq), F32)],
        compiler_params=_params(("parallel", "arbitrary")),
        name="nbr_attn",
    )(rpb.astype(F32).reshape(-1), p, p, p, pc, pc)


def _outproj_kernel(x_ref, ya_ref, yb_ref, wa_ref, wb_ref, g_ref, o_ref):
    acc = _dot(ya_ref[...], wa_ref[...]) + _dot(yb_ref[...], wb_ref[...])
    o_ref[...] = x_ref[...] + g_ref[...] * acc


def _outproj_call(x, ya, yb, w, layer, gate, tm, name):
    m, d = x.shape
    k = ya.shape[1]
    assert yb.shape[1] == k and w.shape[1] == 2 * k
    return pl.pallas_call(
        _outproj_kernel,
        grid=(m // tm,),
        in_specs=[pl.BlockSpec((tm, d), lambda i: (i, 0)),
                  pl.BlockSpec((tm, k), lambda i: (i, 0)),
                  pl.BlockSpec((tm, k), lambda i: (i, 0)),
                  pl.BlockSpec((None, k, d), lambda i: (layer, 0, 0)),
                  pl.BlockSpec((None, k, d), lambda i: (layer, 1, 0)),
                  pl.BlockSpec((1, d), lambda i: (0, 0))],
        out_specs=pl.BlockSpec((tm, d), lambda i: (i, 0)),
        out_shape=jax.ShapeDtypeStruct((m, d), F32),
        compiler_params=_params(("parallel",)),
        name=name,
    )(x, ya, yb, w, w, gate)


HALO = BF16_SUBLANES


def _ffn_kernel(xm_ref, xp_ref, xn_ref, gn_ref, sh_ref, sc_ref, gate_ref, wa_ref, wg_ref,
                cwa_ref, cwg_ref, cba_ref, cbg_ref, wd_ref, o_ref, h_scr, acc_scr, ua_scr, ug_scr):
    i, j = pl.program_id(0), pl.program_id(1)
    ni, nj = pl.num_programs(0), pl.num_programs(1)
    tm = xm_ref.shape[0]

    @pl.when(j == 0)
    def _():
        nm = lambda x: _norm_mod(x, gn_ref[...], sh_ref[...], sc_ref[...])
        h_scr[0:HALO, :] = jnp.where(i > 0, nm(xp_ref[...]), 0.0).astype(BF16)
        h_scr[HALO:HALO + tm, :] = nm(xm_ref[...]).astype(BF16)
        h_scr[HALO + tm:, :] = jnp.where(i < ni - 1, nm(xn_ref[...]), 0.0).astype(BF16)
        acc_scr[...] = jnp.zeros_like(acc_scr)

    h = h_scr[...]
    ua_scr[...] = _dot(h, wa_ref[...])
    ug_scr[...] = _dot(h, wg_ref[...])

    def conv(u_scr, cw_ref, cb_ref):
        out = cb_ref[...] + u_scr[pl.ds(HALO - 1, tm), :] * cw_ref[0:1, :]
        for t in range(1, CONV_W):
            out = out + u_scr[pl.ds(HALO - 1 + t, tm), :] * cw_ref[t:t + 1, :]
        return out

    a = conv(ua_scr, cwa_ref, cba_ref)
    g = conv(ug_scr, cwg_ref, cbg_ref)
    z = (g * (1.0 / (1.0 + jnp.exp(-g)))) * a
    acc_scr[...] += _dot(z.astype(BF16), wd_ref[...])

    @pl.when(j == nj - 1)
    def _():
        o_ref[...] = xm_ref[...] + gate_ref[...] * acc_scr[...]


def _ffn_call(x, gn, sh, sc, gate, w_up, conv_w, conv_b, w_down, layer, tm, tf, name):
    m, d = x.shape
    f = w_down.shape[1]
    nfb = f // tf
    hb = tm // HALO
    last = m // HALO - 1
    row = lambda i, j: (0, 0)
    return pl.pallas_call(
        _ffn_kernel,
        grid=(m // tm, nfb),
        in_specs=[pl.BlockSpec((tm, d), lambda i, j: (i, 0)),
                  pl.BlockSpec((HALO, d), lambda i, j: (jnp.maximum(i * hb - 1, 0), 0)),
                  pl.BlockSpec((HALO, d), lambda i, j: (jnp.minimum((i + 1) * hb, last), 0)),
                  pl.BlockSpec((1, d), row), pl.BlockSpec((1, d), row), pl.BlockSpec((1, d), row),
                  pl.BlockSpec((1, d), row),
                  pl.BlockSpec((None, d, tf), lambda i, j: (layer, 0, j)),
                  pl.BlockSpec((None, d, tf), lambda i, j: (layer, 0, nfb + j)),
                  pl.BlockSpec((None, CONV_W, tf), lambda i, j: (layer, 0, j)),
                  pl.BlockSpec((None, CONV_W, tf), lambda i, j: (layer, 0, nfb + j)),
                  pl.BlockSpec((None, 1, tf), lambda i, j: (layer, 0, j)),
                  pl.BlockSpec((None, 1, tf), lambda i, j: (layer, 0, nfb + j)),
                  pl.BlockSpec((None, tf, d), lambda i, j: (layer, j, 0))],
        out_specs=pl.BlockSpec((tm, d), lambda i, j: (i, 0)),
        out_shape=jax.ShapeDtypeStruct((m, d), F32),
        scratch_shapes=[pltpu.VMEM((tm + 2 * HALO, d), BF16),
                        pltpu.VMEM((tm, d), F32),
                        pltpu.VMEM((tm + 2 * HALO, tf), F32),
                        pltpu.VMEM((tm + 2 * HALO, tf), F32)],
        compiler_params=_params(("parallel", "arbitrary")),
        name=name,
    )(x, x, x, gn, sh, sc, gate, w_up, w_up, conv_w, conv_w, conv_b, conv_b, w_down)


def _offsets(sizes):
    offs, o = [], 0
    for s in sizes:
        offs.append(o)
        o += s
    return offs


def _chunk_classes(sizes, classes):
    out = []
    for s, c in zip(sizes, classes):
        out += [c] * (s // EPI_CHUNK)
    return out


def _gain_vector(sizes, gains):
    parts = []
    for s, g in zip(sizes, gains):
        parts.append(jnp.ones((s,), F32) if g is None else jnp.tile(g.astype(F32), s // g.shape[0]))
    return jnp.concatenate(parts).reshape(1, -1)


def _lambda_init(layer_idx):
    return 0.8 - 0.6 * math.exp(-0.3 * layer_idx)


def _forward(x, c, ctx, c_ctx, w_ada, b_ada, norm1_g, w_in, w_out, a_qk_g, a_lambda, a_subln_g,
             b_qk_g, c_qk_g, c_sink, d_qk_g, d_rpb, norm2_g, w_up, conv_w, conv_b, w_down):
    s, d = x.shape
    t = ctx.shape[0]
    depth = w_ada.shape[0]
    n_heads = d // (2 * HEAD_DIM)
    n_kv = n_heads // GQA_GROUP
    hw = n_heads * HEAD_DIM
    kvw = n_kv * HEAD_DIM
    even_sizes = (hw, hw, hw, hw, kvw, kvw)
    odd_sizes = (hw, kvw, kvw, hw, hw, hw)
    tm = 512
    tf = 512

    mod = _mod_call(c, c_ctx, w_ada, b_ada)
    rope_a = _rope_tables(s, 16)
    rope_b = _rope_tables(s, 32)
    tabs = rope_a + rope_b
    scale_a = A_SUB ** -0.5
    scale = HEAD_DIM ** -0.5

    w_in_b, w_out_b, w_up_b, w_down_b = (w.astype(BF16) for w in (w_in, w_out, w_up, w_down))
    conv_b3 = conv_b.reshape(depth, 1, -1)

    xc = ctx
    for i in range(depth):
        need_ctx = i < depth - 1
        e = i // 2
        vec = lambda r, k: mod[i, r:r + 1, k * d:(k + 1) * d]
        g1n = norm1_g[i].reshape(1, d)
        g2n = norm2_g[i].reshape(1, d)
        if i % 2 == 0:
            sizes = even_sizes
            classes = [(A_SUB, "A"), (A_SUB, "A"), (0, None), (HEAD_DIM, "B"), (HEAD_DIM, "B"), (0, None)]
            gains = [a_qk_g[e, 0] * (scale_a * LOG2E), a_qk_g[e, 1], None,
                     b_qk_g[e, 0] * (scale * LOG2E), b_qk_g[e, 1], None]
        else:
            sizes = odd_sizes
            classes = [(HEAD_DIM, "B"), (HEAD_DIM, "B"), (0, None), (HEAD_DIM, None), (HEAD_DIM, None), (0, None)]
            gains = [c_qk_g[e, 0] * scale, c_qk_g[e, 1], None, d_qk_g[e, 0] * scale, d_qk_g[e, 1], None]
        offs = _offsets(sizes)
        chunk_cls = _chunk_classes(sizes, classes)
        gain = _gain_vector(sizes, gains)

        p = _inproj_call(x, g1n, vec(0, 0), vec(0, 1), w_in_b, i, gain, chunk_cls, tabs, tm, f"inproj{i}")
        pc = _inproj_call(xc, g1n, vec(1, 0), vec(1, 1), w_in_b, i, gain, chunk_cls, None, t, f"inproj_ctx{i}")

        if i % 2 == 0:
            lam_init = _lambda_init(i)
            fa = functools.partial(_flash_call, "diff", offs=offs[0:3], n_heads=n_heads, lam_p=a_lambda[e],
                                   subln=a_subln_g[e], lam_init=lam_init)
            fb = functools.partial(_flash_call, "gqa", offs=offs[3:6], n_heads=n_kv, lam_p=None,
                                   subln=None, lam_init=None)
            bound_a = (A_SUB * LOGIT_BOUND_SLACK) * jnp.max(jnp.abs(gains[0])) * jnp.max(jnp.abs(gains[1]))
            bound_b = (HEAD_DIM * LOGIT_BOUND_SLACK) * jnp.max(jnp.abs(gains[3])) * jnp.max(jnp.abs(gains[4]))
            ya = lax.cond(bound_a <= MAX_LOGIT_BOUND,
                          lambda: fa(p, p, pc, tq=512, name=f"diff_attn{i}", bound=bound_a.reshape(1)),
                          lambda: fa(p, p, pc, tq=512, name=f"diff_attn_online{i}"))
            yb = lax.cond(bound_b <= MAX_LOGIT_BOUND,
                          lambda: fb(p, p, pc, tq=256, name=f"gqa_attn{i}", bound=bound_b.reshape(1)),
                          lambda: fb(p, p, pc, tq=256, name=f"gqa_attn_online{i}"))
            if need_ctx:
                ya_c = fa(pc, None, pc, tq=t, name=f"diff_attn_ctx{i}")
                yb_c = fb(pc, None, pc, tq=t, name=f"gqa_attn_ctx{i}")
        else:
            ya = _win_call(p, pc, offs[0:3], n_kv, c_sink[e], 256)
            yb = _nbr_call(p, pc, offs[3:6], n_heads, d_rpb[e])
            if need_ctx:
                raise NotImplementedError("context stream after an odd layer")

        x = _outproj_call(x, ya, yb, w_out_b, i, vec(0, 2), tm, f"outproj{i}")
        x = _ffn_call(x, g2n, vec(0, 3), vec(0, 4), vec(0, 5), w_up_b, conv_w, conv_b3, w_down_b, i, tm, tf,
                      f"ffn{i}")
        if need_ctx:
            xc = _outproj_call(xc, ya_c, yb_c, w_out_b, i, vec(1, 2), t, f"outproj_ctx{i}")
            xc = _ffn_call(xc, g2n, vec(1, 3), vec(1, 4), vec(1, 5), w_up_b, conv_w, conv_b3, w_down_b, i, t, tf,
                           f"ffn_ctx{i}")
    return x


def kernel(x, c, ctx, c_ctx, w_ada, b_ada, norm1_g, w_in, w_out, a_qk_g, a_lambda, a_subln_g, b_qk_g, c_qk_g,
           c_sink, d_qk_g, d_rpb, norm2_g, w_up, conv_w, conv_b, w_down):
    outs = [_forward(x[b], c[b], ctx[b], c_ctx, w_ada, b_ada, norm1_g, w_in, w_out, a_qk_g, a_lambda, a_subln_g,
                     b_qk_g, c_qk_g, c_sink, d_qk_g, d_rpb, norm2_g, w_up, conv_w, conv_b, w_down)
            for b in range(x.shape[0])]
    return jnp.stack(outs)
```

```python
import functools
import math

import jax
import jax.numpy as jnp
from jax import lax
from jax.experimental import pallas as pl
from jax.experimental.pallas import tpu as pltpu

GRID_W = 64
HEAD_DIM = 128
A_SUB = HEAD_DIM // 2
ROPE_THETA = 10000.0
EPS = 1e-6
NEG_INF = -1e30
N_MOD = 6
C_WINDOW = 128
NA_KH = 8
NA_KW = 16
CONV_W = 3
GQA_GROUP = 4
LOG2E = math.log2(math.e)
LOGIT_BOUND_SLACK = 1.02
MAX_LOGIT_BOUND = 50.0

LANES = 128
BF16_SUBLANES = 16
VMEM_LIMIT = 56 * 1024 * 1024

F32 = jnp.float32
BF16 = jnp.bfloat16


def _params(sem, vmem=VMEM_LIMIT):
    return pltpu.CompilerParams(dimension_semantics=sem, vmem_limit_bytes=vmem)


def _dot(a, b):
    return jnp.dot(a, b, preferred_element_type=F32)


def _dot_nt(a, b):
    return lax.dot_general(a, b, (((1,), (1,)), ((), ())), preferred_element_type=F32)


def _dot_tn(a, b):
    return lax.dot_general(a, b, (((0,), (0,)), ((), ())), preferred_element_type=F32)


def _norm_mod(x, g, sh, sc):
    inv = lax.rsqrt(jnp.mean(x * x, axis=-1, keepdims=True) + EPS)
    return (x * inv * g) * (1.0 + sc) + sh


def _mod_kernel(cb_ref, w_ref, b_ref, o_ref):
    tn = w_ref.shape[2]
    for r in range(2):
        cv = cb_ref[r]
        a = cv * (1.0 / (1.0 + jnp.exp(-cv)))
        for ch in range(tn // LANES):
            sl = slice(ch * LANES, (ch + 1) * LANES)
            o_ref[0, r:r + 1, sl] = jnp.sum(w_ref[0, :, sl] * a, axis=0, keepdims=True) + b_ref[0, :, sl]


def _mod_call(c, c_ctx, w_ada, b_ada):
    depth, d, n = w_ada.shape
    tn = 1024
    cb = jnp.broadcast_to(jnp.stack([c, c_ctx])[:, :, None], (2, d, LANES))
    return pl.pallas_call(
        _mod_kernel,
        grid=(depth, n // tn),
        in_specs=[pl.BlockSpec((2, d, LANES), lambda i, j: (0, 0, 0)),
                  pl.BlockSpec((1, d, tn), lambda i, j: (i, 0, j)),
                  pl.BlockSpec((1, 1, tn), lambda i, j: (i, 0, j))],
        out_specs=pl.BlockSpec((1, 2, tn), lambda i, j: (i, 0, j)),
        out_shape=jax.ShapeDtypeStruct((depth, 2, n), F32),
        compiler_params=_params(("parallel", "parallel")),
        name="adaln_mod",
    )(cb, w_ada, b_ada.reshape(depth, 1, n))


EPI_CHUNK = 256


def _head_norm(p, gain, width):
    shift = int(math.log2(width))
    r = lax.broadcasted_iota(jnp.int32, (EPI_CHUNK, EPI_CHUNK), 0) >> shift
    c = lax.broadcasted_iota(jnp.int32, (EPI_CHUNK, EPI_CHUNK), 1) >> shift
    bd = jnp.where(r == c, 1.0, 0.0).astype(BF16)
    seg = _dot((p * p).astype(BF16), bd)
    return p * lax.rsqrt(seg * (1.0 / width) + EPS) * gain


def _rope(y, half, tabs):
    c_ref, s1_ref, s2_ref = tabs
    c, s1, s2 = c_ref[...], s1_ref[...], s2_ref[...]
    outs = []
    for k in range(y.shape[1] // LANES):
        yk = y[:, k * LANES:(k + 1) * LANES]
        outs.append(yk * c + pltpu.roll(yk, LANES - half, 1) * s1 + pltpu.roll(yk, half, 1) * s2)
    return jnp.concatenate(outs, axis=1)


def _inproj_kernel(*refs, classes, rope, tn):
    x_ref, g_ref, sh_ref, sc_ref, w_ref, gain_ref = refs[:6]
    tabs = {"A": refs[6:9], "B": refs[9:12]} if rope else None
    o_ref = refs[-1]
    h = _norm_mod(x_ref[...], g_ref[...], sh_ref[...], sc_ref[...]).astype(BF16)
    per = tn // EPI_CHUNK
    for t in range(len(classes) // per):
        p = _dot(h, w_ref[:, t * tn:(t + 1) * tn])
        for ci, (width, kind) in enumerate(classes[t * per:(t + 1) * per]):
            sl = slice(t * tn + ci * EPI_CHUNK, t * tn + (ci + 1) * EPI_CHUNK)
            y = p[:, ci * EPI_CHUNK:(ci + 1) * EPI_CHUNK]
            if width:
                y = _head_norm(y, gain_ref[:, sl], width)
            if kind and rope:
                y = _rope(y, 16 if kind == "A" else 32, tabs[kind])
            o_ref[:, sl] = y.astype(BF16)


def _inproj_call(x, g, sh, sc, w, layer, gain, classes, tabs, tm, name):
    m, d = x.shape
    n = w.shape[2]
    rope = tabs is not None
    const = lambda i: (0, 0)
    in_specs = [pl.BlockSpec((tm, d), lambda i: (i, 0)),
                pl.BlockSpec((1, d), const), pl.BlockSpec((1, d), const), pl.BlockSpec((1, d), const),
                pl.BlockSpec((None, d, n), lambda i: (layer, 0, 0), pipeline_mode=pl.Buffered(1)),
                pl.BlockSpec((1, n), const)]
    args = [x, g, sh, sc, w, gain]
    if rope:
        in_specs += [pl.BlockSpec((tm, LANES), lambda i: (i, 0))] * 6
        args += list(tabs)
    return pl.pallas_call(
        functools.partial(_inproj_kernel, classes=tuple(classes), rope=rope, tn=2 * EPI_CHUNK),
        grid=(m // tm,),
        in_specs=in_specs,
        out_specs=pl.BlockSpec((tm, n), lambda i: (i, 0)),
        out_shape=jax.ShapeDtypeStruct((m, n), BF16),
        compiler_params=_params(("parallel",)),
        name=name,
    )(*args)


def _rope_tables(s, half):
    n_rows = s // GRID_W
    freqs = ROPE_THETA ** (-jnp.arange(half, dtype=F32) / half)
    zeros = jnp.zeros((s, half), F32)
    cs, s1s, s2s = [], [], []
    for count, expand in ((n_rows, lambda a: jnp.repeat(a, GRID_W, axis=0)),
                          (GRID_W, lambda a: jnp.tile(a, (n_rows, 1)))):
        ang = jnp.arange(count, dtype=jnp.int32).astype(F32)[:, None] * freqs[None, :]
        cos, sin = expand(jnp.cos(ang)), expand(jnp.sin(ang))
        cs += [cos, cos]
        s1s += [-sin, zeros]
        s2s += [zeros, sin]
    reps = LANES // (4 * half)
    return tuple(jnp.tile(jnp.concatenate(t, axis=1), (1, reps)) for t in (cs, s1s, s2s))


def _flash_kernel(*refs, mode, tq, tk, n_lat, lam_init, bounded):
    if bounded:
        bound = refs[0][0]
        refs = refs[1:]
    q_ref, kc_ref, vc_ref = refs[:3]
    pos = 3
    if n_lat:
        k_ref, v_ref = refs[3:5]
        pos = 5
    if mode == "diff":
        lam_ref, sg_ref = refs[pos:pos + 2]
        pos += 2
    o_ref = refs[pos]

    if mode == "diff":
        q = q_ref[...].astype(F32)
        lane = lax.broadcasted_iota(jnp.int32, q.shape, 1)
        qs = jnp.concatenate([jnp.where(lane < A_SUB, q, 0.0), jnp.where(lane >= A_SUB, q, 0.0)],
                             axis=0).astype(BF16)
    else:
        qs = jnp.concatenate([q_ref[:, g * HEAD_DIM:(g + 1) * HEAD_DIM] for g in range(GQA_GROUP)], axis=0)
    rows = qs.shape[0]

    def pv_dot(v, p):
        return lax.dot_general(v, p.astype(BF16), (((0,), (0,)), ((), ())), preferred_element_type=F32)

    if bounded:
        def step(k, v, carry):
            _, l, acc = carry
            p = jnp.exp2(_dot_nt(k, qs) - bound)
            return carry[0], l + jnp.sum(p, axis=0, keepdims=True), acc + pv_dot(v, p)
    else:
        def step(k, v, carry):
            m, l, acc = carry
            s = _dot_nt(k, qs)
            m_new = jnp.maximum(m, jnp.max(s, axis=0, keepdims=True))
            p = jnp.exp2(s - m_new)
            alpha = jnp.exp2(m - m_new)
            return m_new, alpha * l + jnp.sum(p, axis=0, keepdims=True), alpha * acc + pv_dot(v, p)

    carry = (jnp.full((1, rows), NEG_INF, F32), jnp.zeros((1, rows), F32), jnp.zeros((HEAD_DIM, rows), F32))
    carry = step(kc_ref[...], vc_ref[...], carry)
    if n_lat:
        def body(c, carry):
            off = pl.multiple_of(c * tk, tk)
            return step(k_ref[pl.ds(off, tk), :], v_ref[pl.ds(off, tk), :], carry)
        carry = lax.fori_loop(0, n_lat, body, carry, unroll=8)
    _, l, acc = carry
    o = (acc * (1.0 / l)).T
    if mode == "diff":
        lp = lam_ref[...]
        lam = (jnp.exp(jnp.sum(lp[0:1] * lp[1:2], axis=1, keepdims=True))
               - jnp.exp(jnp.sum(lp[2:3] * lp[3:4], axis=1, keepdims=True)) + lam_init)
        dlt = o[:tq] - lam * o[tq:]
        y = dlt * lax.rsqrt(jnp.mean(dlt * dlt, axis=-1, keepdims=True) + EPS) * sg_ref[...]
        o_ref[...] = (y * (1.0 - lam_init)).astype(BF16)
    else:
        for g in range(GQA_GROUP):
            o_ref[:, g * HEAD_DIM:(g + 1) * HEAD_DIM] = o[g * tq:(g + 1) * tq].astype(BF16)


def _flash_call(mode, qsrc, ksrc, pc, offs, n_heads, lam_p, subln, lam_init, tq, name, bound=None):
    sq = qsrc.shape[0]
    t = pc.shape[0]
    qo, ko, vo = offs
    tk = 1024
    qw = HEAD_DIM if mode == "diff" else GQA_GROUP * HEAD_DIM
    qb, kb, vb = qo // qw, ko // HEAD_DIM, vo // HEAD_DIM
    in_specs = [pl.BlockSpec((tq, qw), lambda h, i: (i, qb + h)),
                pl.BlockSpec((t, HEAD_DIM), lambda h, i: (0, kb + h)),
                pl.BlockSpec((t, HEAD_DIM), lambda h, i: (0, vb + h))]
    args = [qsrc, pc, pc]
    if bound is not None:
        in_specs = [pl.BlockSpec(memory_space=pltpu.SMEM)] + in_specs
        args = [bound] + args
    n_lat = 0
    if ksrc is not None:
        s = ksrc.shape[0]
        tk = min(tk, s)
        assert s % tk == 0
        n_lat = s // tk
        in_specs += [pl.BlockSpec((s, HEAD_DIM), lambda h, i: (0, kb + h)),
                     pl.BlockSpec((s, HEAD_DIM), lambda h, i: (0, vb + h))]
        args += [ksrc, ksrc]
    if mode == "diff":
        in_specs += [pl.BlockSpec(lam_p.shape, lambda h, i: (0, 0)),
                     pl.BlockSpec((1, HEAD_DIM), lambda h, i: (0, 0))]
        args += [lam_p, subln.reshape(1, HEAD_DIM)]
    return pl.pallas_call(
        functools.partial(_flash_kernel, mode=mode, tq=tq, tk=tk, n_lat=n_lat, lam_init=lam_init,
                          bounded=bound is not None),
        grid=(n_heads, sq // tq),
        in_specs=in_specs,
        out_specs=pl.BlockSpec((tq, qw), lambda h, i: (i, h)),
        out_shape=jax.ShapeDtypeStruct((sq, n_heads * qw), BF16),
        compiler_params=_params(("parallel", "parallel")),
        name=name,
    )(*args)


def _win_kernel(sink_ref, q_ref, k_ref, v_ref, kc_ref, vc_ref, o_ref, *, tq, s_len):
    kvh = pl.program_id(0)
    q0 = pl.program_id(1) * tq
    band = tq + 2 * C_WINDOW
    start = pl.multiple_of(jnp.clip(q0 - C_WINDOW, 0, s_len - band), C_WINDOW)
    kb = k_ref[pl.ds(start, band), :]
    vb = v_ref[pl.ds(start, band), :]
    pair = 2
    rows = pair * tq
    lane = lax.broadcasted_iota(jnp.int32, (band, rows), 1)
    qpos = q0 + (lane & (tq - 1))
    kpos = start + lax.broadcasted_iota(jnp.int32, (band, rows), 0)
    valid = jnp.abs(qpos - kpos) <= C_WINDOW
    first = lax.broadcasted_iota(jnp.int32, (1, rows), 1) < tq
    passes = []
    for g0 in range(0, GQA_GROUP, pair):
        qs = jnp.concatenate([q_ref[:, g * HEAD_DIM:(g + 1) * HEAD_DIM] for g in range(g0, g0 + pair)], axis=0)
        passes.append((_dot_nt(kb, qs), _dot_nt(kc_ref[...], qs)))
    for g0, (s_lat, s_ctx) in zip(range(0, GQA_GROUP, pair), passes):
        s_lat = jnp.where(valid, s_lat, NEG_INF)
        sink = jnp.where(first, sink_ref[kvh * GQA_GROUP + g0], sink_ref[kvh * GQA_GROUP + g0 + 1])
        m = jnp.maximum(jnp.maximum(jnp.max(s_lat, axis=0, keepdims=True),
                                    jnp.max(s_ctx, axis=0, keepdims=True)), sink)
        p_lat = jnp.exp(s_lat - m)
        p_ctx = jnp.exp(s_ctx - m)
        l = jnp.sum(p_lat, axis=0, keepdims=True) + jnp.sum(p_ctx, axis=0, keepdims=True) + jnp.exp(sink - m)
        o = ((_dot_tn(vc_ref[...], p_ctx.astype(BF16)) + _dot_tn(vb, p_lat.astype(BF16))) * (1.0 / l)).T
        for g in range(pair):
            o_ref[:, (g0 + g) * HEAD_DIM:(g0 + g + 1) * HEAD_DIM] = o[g * tq:(g + 1) * tq].astype(BF16)


def _win_call(p, pc, offs, n_kv, sink, tq):
    s = p.shape[0]
    t = pc.shape[0]
    qo, ko, vo = offs
    qw = GQA_GROUP * HEAD_DIM
    qb, kb, vb = qo // qw, ko // HEAD_DIM, vo // HEAD_DIM
    return pl.pallas_call(
        functools.partial(_win_kernel, tq=tq, s_len=s),
        grid=(n_kv, s // tq),
        in_specs=[pl.BlockSpec(memory_space=pltpu.SMEM),
                  pl.BlockSpec((tq, qw), lambda h, i: (i, qb + h)),
                  pl.BlockSpec((s, HEAD_DIM), lambda h, i: (0, kb + h)),
                  pl.BlockSpec((s, HEAD_DIM), lambda h, i: (0, vb + h)),
                  pl.BlockSpec((t, HEAD_DIM), lambda h, i: (0, kb + h)),
                  pl.BlockSpec((t, HEAD_DIM), lambda h, i: (0, vb + h))],
        out_specs=pl.BlockSpec((tq, qw), lambda h, i: (i, h)),
        out_shape=jax.ShapeDtypeStruct((s, n_kv * qw), BF16),
        compiler_params=_params(("parallel", "parallel")),
        name="window_attn",
    )(sink, p, p, p, pc, pc)


NBR_Q_ROWS = 4
NBR_WIN_ROWS = NBR_Q_ROWS + NA_KH


def _nbr_build_bias(rpb_ref, head, mats_scr, tab_scr, n_rows):
    na, nb = 2 * NA_KH - 1, 2 * NA_KW - 1
    shape = (GRID_W, 2 * GRID_W)
    wk = lax.broadcasted_iota(jnp.int32, shape, 0)
    lane = lax.broadcasted_iota(jnp.int32, shape, 1)
    wq = lane & (GRID_W - 1)
    dc = jnp.clip(wk - wq, -(NA_KW - 1), NA_KW - 1) + (NA_KW - 1)
    c_start = jnp.clip(wq - NA_KW // 2, 0, GRID_W - NA_KW)
    col_ok = jnp.logical_and(wk >= c_start, wk < c_start + NA_KW)

    def build(a, carry):
        base = (head * na + a) * nb
        mat = jnp.zeros(shape, F32)
        for b in range(nb):
            mat = jnp.where(dc == b, rpb_ref[base + b], mat)
        mats_scr[a] = jnp.where(col_ok, mat, NEG_INF)
        return carry

    lax.fori_loop(0, na, build, 0)
    neg = jnp.full(shape, NEG_INF, F32)
    first_half = lane < GRID_W
    for cl, r0 in enumerate((0, NBR_Q_ROWS, n_rows - NBR_Q_ROWS)):
        ws = min(max(r0 - NA_KH // 2, 0), n_rows - NBR_WIN_ROWS)
        for kr in range(NBR_WIN_ROWS):
            for qp in range(NBR_Q_ROWS // 2):
                halves = []
                for r in (r0 + 2 * qp, r0 + 2 * qp + 1):
                    r_start = min(max(r - NA_KH // 2, 0), n_rows - NA_KH)
                    inside = r_start <= ws + kr < r_start + NA_KH
                    halves.append(mats_scr[ws + kr - r + NA_KH - 1] if inside else neg)
                tab_scr[cl, kr * GRID_W:(kr + 1) * GRID_W, qp * 2 * GRID_W:(qp + 1) * 2 * GRID_W] = (
                    jnp.where(first_half, halves[0], halves[1]))


NBR_HEADS_PER_STEP = 4


def _nbr_kernel(rpb_ref, q_ref, k_ref, v_ref, kc_ref, vc_ref, o_ref, mats_scr, tab_scr, *, tq, win, s_len):
    i = pl.program_id(1)
    nq = pl.num_programs(1)

    @pl.when(i == 0)
    def _():
        for hh in range(NBR_HEADS_PER_STEP):
            _nbr_build_bias(rpb_ref, pl.program_id(0) * NBR_HEADS_PER_STEP + hh, mats_scr, tab_scr.at[hh],
                            s_len // GRID_W)

    q0 = i * tq
    lead = (NA_KH // 2) * GRID_W
    start = pl.multiple_of(jnp.clip(q0 - lead, 0, s_len - win), lead)
    cls = jnp.where(i == 0, 0, jnp.where(i == nq - 1, 2, 1))
    for hh in range(NBR_HEADS_PER_STEP):
        sl = slice(hh * HEAD_DIM, (hh + 1) * HEAD_DIM)
        q = q_ref[:, sl]
        s_win = _dot_nt(k_ref[pl.ds(start, win), sl], q) + tab_scr[hh, cls]
        s_ctx = _dot_nt(kc_ref[:, sl], q)
        m = jnp.maximum(jnp.max(s_win, axis=0, keepdims=True), jnp.max(s_ctx, axis=0, keepdims=True))
        p_win = jnp.exp(s_win - m)
        p_ctx = jnp.exp(s_ctx - m)
        l = jnp.sum(p_win, axis=0, keepdims=True) + jnp.sum(p_ctx, axis=0, keepdims=True)
        o = (_dot_tn(vc_ref[:, sl], p_ctx.astype(BF16))
             + _dot_tn(v_ref[pl.ds(start, win), sl], p_win.astype(BF16)))
        o_ref[:, sl] = ((o * (1.0 / l)).T).astype(BF16)


def _nbr_call(p, pc, offs, n_heads, rpb):
    s = p.shape[0]
    t = pc.shape[0]
    tq = NBR_Q_ROWS * GRID_W
    win = NBR_WIN_ROWS * GRID_W
    bw = NBR_HEADS_PER_STEP * HEAD_DIM
    qb, kb, vb = (o // bw for o in offs)
    assert s // GRID_W >= NBR_WIN_ROWS and rpb.shape[1:] == (2 * NA_KH - 1, 2 * NA_KW - 1)
    assert n_heads % NBR_HEADS_PER_STEP == 0 and all(o % bw == 0 for o in offs)
    return pl.pallas_call(
        functools.partial(_nbr_kernel, tq=tq, win=win, s_len=s),
        grid=(n_heads // NBR_HEADS_PER_STEP, s // tq),
        in_specs=[pl.BlockSpec(memory_space=pltpu.SMEM),
                  pl.BlockSpec((tq, bw), lambda h, i: (i, qb + h)),
                  pl.BlockSpec((s, bw), lambda h, i: (0, kb + h)),
                  pl.BlockSpec((s, bw), lambda h, i: (0, vb + h)),
                  pl.BlockSpec((t, bw), lambda h, i: (0, kb + h)),
                  pl.BlockSpec((t, bw), lambda h, i: (0, vb + h))],
        out_specs=pl.BlockSpec((tq, bw), lambda h, i: (i, h)),
        out_shape=jax.ShapeDtypeStruct((s, n_heads * HEAD_DIM), BF16),
        scratch_shapes=[pltpu.VMEM((2 * NA_KH - 1, GRID_W, 2 * GRID_W), F32),
                        pltpu.VMEM((NBR_HEADS_PER_STEP, 3, win, tq), F32)],
        compiler_params=_params(("parallel", "arbitrary")),
        name="nbr_attn",
    )(rpb.astype(F32).reshape(-1), p, p, p, pc, pc)


def _outproj_kernel(x_ref, ya_ref, yb_ref, wa_ref, wb_ref, g_ref, o_ref):
    acc = _dot(ya_ref[...], wa_ref[...]) + _dot(yb_ref[...], wb_ref[...])
    o_ref[...] = x_ref[...] + g_ref[...] * acc


def _outproj_call(x, ya, yb, w, layer, gate, tm, name):
    m, d = x.shape
    k = ya.shape[1]
    assert yb.shape[1] == k and w.shape[1] == 2 * k
    return pl.pallas_call(
        _outproj_kernel,
        grid=(m // tm,),
        in_specs=[pl.BlockSpec((tm, d), lambda i: (i, 0)),
                  pl.BlockSpec((tm, k), lambda i: (i, 0)),
                  pl.BlockSpec((tm, k), lambda i: (i, 0)),
                  pl.BlockSpec((None, k, d), lambda i: (layer, 0, 0)),
                  pl.BlockSpec((None, k, d), lambda i: (layer, 1, 0)),
                  pl.BlockSpec((1, d), lambda i: (0, 0))],
        out_specs=pl.BlockSpec((tm, d), lambda i: (i, 0)),
        out_shape=jax.ShapeDtypeStruct((m, d), F32),
        compiler_params=_params(("parallel",)),
        name=name,
    )(x, ya, yb, w, w, gate)


HALO = BF16_SUBLANES


def _ffn_kernel(xm_ref, xp_ref, xn_ref, gn_ref, sh_ref, sc_ref, gate_ref, wa_ref, wg_ref,
                cwa_ref, cwg_ref, cba_ref, cbg_ref, wd_ref, o_ref,
                h_scr, acc_scr, ua0_scr, ua1_scr, ug0_scr, ug1_scr):
    i, j = pl.program_id(0), pl.program_id(1)
    ni, nj = pl.num_programs(0), pl.num_programs(1)
    tm = xm_ref.shape[0]

    @pl.when(j == 0)
    def _():
        nm = lambda x: _norm_mod(x, gn_ref[...], sh_ref[...], sc_ref[...])
        h_scr[0:HALO, :] = jnp.where(i > 0, nm(xp_ref[...]), 0.0).astype(BF16)
        h_scr[HALO:HALO + tm, :] = nm(xm_ref[...]).astype(BF16)
        h_scr[HALO + tm:, :] = jnp.where(i < ni - 1, nm(xn_ref[...]), 0.0).astype(BF16)
        acc_scr[...] = jnp.zeros_like(acc_scr)

    def conv(u_scr, cw_ref, cb_ref, sl):
        out = cb_ref[:, sl] + u_scr[pl.ds(HALO - 1, tm), :] * cw_ref[0:1, sl]
        for t in range(1, CONV_W):
            out = out + u_scr[pl.ds(HALO - 1 + t, tm), :] * cw_ref[t:t + 1, sl]
        return out

    def silu_conv(u_scr, sl):
        g = conv(u_scr, cwg_ref, cbg_ref, sl)
        return g * (1.0 / (1.0 + jnp.exp(-g)))

    h = h_scr[...]
    half = wa_ref.shape[1] // 2
    lo, hi = slice(0, half), slice(half, 2 * half)
    ug0_scr[...] = _dot(h, wg_ref[:, lo])
    ug1_scr[...] = _dot(h, wg_ref[:, hi])
    sg0 = silu_conv(ug0_scr, lo)
    ua0_scr[...] = _dot(h, wa_ref[:, lo])
    sg1 = silu_conv(ug1_scr, hi)
    ua1_scr[...] = _dot(h, wa_ref[:, hi])
    z0 = sg0 * conv(ua0_scr, cwa_ref, cba_ref, lo)
    acc_scr[...] += _dot(z0.astype(BF16), wd_ref[lo, :])
    z1 = sg1 * conv(ua1_scr, cwa_ref, cba_ref, hi)
    acc_scr[...] += _dot(z1.astype(BF16), wd_ref[hi, :])

    @pl.when(j == nj - 1)
    def _():
        o_ref[...] = xm_ref[...] + gate_ref[...] * acc_scr[...]


def _ffn_call(x, gn, sh, sc, gate, w_up, conv_w, conv_b, w_down, layer, tm, tf, name):
    m, d = x.shape
    f = w_down.shape[1]
    nfb = f // tf
    hb = tm // HALO
    last = m // HALO - 1
    row = lambda i, j: (0, 0)
    return pl.pallas_call(
        _ffn_kernel,
        grid=(m // tm, nfb),
        in_specs=[pl.BlockSpec((tm, d), lambda i, j: (i, 0)),
                  pl.BlockSpec((HALO, d), lambda i, j: (jnp.maximum(i * hb - 1, 0), 0)),
                  pl.BlockSpec((HALO, d), lambda i, j: (jnp.minimum((i + 1) * hb, last), 0)),
                  pl.BlockSpec((1, d), row), pl.BlockSpec((1, d), row), pl.BlockSpec((1, d), row),
                  pl.BlockSpec((1, d), row),
                  pl.BlockSpec((None, d, tf), lambda i, j: (layer, 0, j)),
                  pl.BlockSpec((None, d, tf), lambda i, j: (layer, 0, nfb + j)),
                  pl.BlockSpec((None, CONV_W, tf), lambda i, j: (layer, 0, j)),
                  pl.BlockSpec((None, CONV_W, tf), lambda i, j: (layer, 0, nfb + j)),
                  pl.BlockSpec((None, 1, tf), lambda i, j: (layer, 0, j)),
                  pl.BlockSpec((None, 1, tf), lambda i, j: (layer, 0, nfb + j)),
                  pl.BlockSpec((None, tf, d), lambda i, j: (layer, j, 0))],
        out_specs=pl.BlockSpec((tm, d), lambda i, j: (i, 0)),
        out_shape=jax.ShapeDtypeStruct((m, d), F32),
        scratch_shapes=[pltpu.VMEM((tm + 2 * HALO, d), BF16),
                        pltpu.VMEM((tm, d), F32)] + [pltpu.VMEM((tm + 2 * HALO, tf // 2), F32)] * 4,
        compiler_params=_params(("parallel", "arbitrary")),
        name=name,
    )(x, x, x, gn, sh, sc, gate, w_up, w_up, conv_w, conv_w, conv_b, conv_b, w_down)


def _offsets(sizes):
    offs, o = [], 0
    for s in sizes:
        offs.append(o)
        o += s
    return offs


def _chunk_classes(sizes, classes):
    out = []
    for s, c in zip(sizes, classes):
        out += [c] * (s // EPI_CHUNK)
    return out


def _gain_vector(sizes, gains):
    parts = []
    for s, g in zip(sizes, gains):
        parts.append(jnp.ones((s,), F32) if g is None else jnp.tile(g.astype(F32), s // g.shape[0]))
    return jnp.concatenate(parts).reshape(1, -1)


def _lambda_init(layer_idx):
    return 0.8 - 0.6 * math.exp(-0.3 * layer_idx)


def _forward(x, c, ctx, c_ctx, w_ada, b_ada, norm1_g, w_in, w_out, a_qk_g, a_lambda, a_subln_g,
             b_qk_g, c_qk_g, c_sink, d_qk_g, d_rpb, norm2_g, w_up, conv_w, conv_b, w_down):
    s, d = x.shape
    t = ctx.shape[0]
    depth = w_ada.shape[0]
    n_heads = d // (2 * HEAD_DIM)
    n_kv = n_heads // GQA_GROUP
    hw = n_heads * HEAD_DIM
    kvw = n_kv * HEAD_DIM
    even_sizes = (hw, hw, hw, hw, kvw, kvw)
    odd_sizes = (hw, kvw, kvw, hw, hw, hw)
    tm = 512
    tf = 512

    mod = _mod_call(c, c_ctx, w_ada, b_ada)
    rope_a = _rope_tables(s, 16)
    rope_b = _rope_tables(s, 32)
    tabs = rope_a + rope_b
    scale_a = A_SUB ** -0.5
    scale = HEAD_DIM ** -0.5

    w_in_b, w_out_b, w_up_b, w_down_b = (w.astype(BF16) for w in (w_in, w_out, w_up, w_down))
    conv_b3 = conv_b.reshape(depth, 1, -1)

    xc = ctx
    for i in range(depth):
        need_ctx = i < depth - 1
        e = i // 2
        vec = lambda r, k: mod[i, r:r + 1, k * d:(k + 1) * d]
        g1n = norm1_g[i].reshape(1, d)
        g2n = norm2_g[i].reshape(1, d)
        if i % 2 == 0:
            sizes = even_sizes
            classes = [(A_SUB, "A"), (A_SUB, "A"), (0, None), (HEAD_DIM, "B"), (HEAD_DIM, "B"), (0, None)]
            gains = [a_qk_g[e, 0] * (scale_a * LOG2E), a_qk_g[e, 1], None,
                     b_qk_g[e, 0] * (scale * LOG2E), b_qk_g[e, 1], None]
        else:
            sizes = odd_sizes
            classes = [(HEAD_DIM, "B"), (HEAD_DIM, "B"), (0, None), (HEAD_DIM, None), (HEAD_DIM, None), (0, None)]
            gains = [c_qk_g[e, 0] * scale, c_qk_g[e, 1], None, d_qk_g[e, 0] * scale, d_qk_g[e, 1], None]
        offs = _offsets(sizes)
        chunk_cls = _chunk_classes(sizes, classes)
        gain = _gain_vector(sizes, gains)

        p = _inproj_call(x, g1n, vec(0, 0), vec(0, 1), w_in_b, i, gain, chunk_cls, tabs, tm, f"inproj{i}")
        pc = _inproj_call(xc, g1n, vec(1, 0), vec(1, 1), w_in_b, i, gain, chunk_cls, None, t, f"inproj_ctx{i}")

        if i % 2 == 0:
            lam_init = _lambda_init(i)
            fa = functools.partial(_flash_call, "diff", offs=offs[0:3], n_heads=n_heads, lam_p=a_lambda[e],
                                   subln=a_subln_g[e], lam_init=lam_init)
            fb = functools.partial(_flash_call, "gqa", offs=offs[3:6], n_heads=n_kv, lam_p=None,
                                   subln=None, lam_init=None)
            bound_a = (A_SUB * LOGIT_BOUND_SLACK) * jnp.max(jnp.abs(gains[0])) * jnp.max(jnp.abs(gains[1]))
            bound_b = (HEAD_DIM * LOGIT_BOUND_SLACK) * jnp.max(jnp.abs(gains[3])) * jnp.max(jnp.abs(gains[4]))
            ya = lax.cond(bound_a <= MAX_LOGIT_BOUND,
                          lambda: fa(p, p, pc, tq=512, name=f"diff_attn{i}", bound=bound_a.reshape(1)),
                          lambda: fa(p, p, pc, tq=512, name=f"diff_attn_online{i}"))
            yb = lax.cond(bound_b <= MAX_LOGIT_BOUND,
                          lambda: fb(p, p, pc, tq=256, name=f"gqa_attn{i}", bound=bound_b.reshape(1)),
                          lambda: fb(p, p, pc, tq=256, name=f"gqa_attn_online{i}"))
            if need_ctx:
                ya_c = fa(pc, None, pc, tq=t, name=f"diff_attn_ctx{i}")
                yb_c = fb(pc, None, pc, tq=t, name=f"gqa_attn_ctx{i}")
        else:
            ya = _win_call(p, pc, offs[0:3], n_kv, c_sink[e], 256)
            yb = _nbr_call(p, pc, offs[3:6], n_heads, d_rpb[e])
            if need_ctx:
                raise NotImplementedError("context stream after an odd layer")

        x = _outproj_call(x, ya, yb, w_out_b, i, vec(0, 2), tm, f"outproj{i}")
        x = _ffn_call(x, g2n, vec(0, 3), vec(0, 4), vec(0, 5), w_up_b, conv_w, conv_b3, w_down_b, i, tm, tf,
                      f"ffn{i}")
        if need_ctx:
            xc = _outproj_call(xc, ya_c, yb_c, w_out_b, i, vec(1, 2), t, f"outproj_ctx{i}")
            xc = _ffn_call(xc, g2n, vec(1, 3), vec(1, 4), vec(1, 5), w_up_b, conv_w, conv_b3, w_down_b, i, t, tf,
                           f"ffn_ctx{i}")
    return x


def kernel(x, c, ctx, c_ctx, w_ada, b_ada, norm1_g, w_in, w_out, a_qk_g, a_lambda, a_subln_g, b_qk_g, c_qk_g,
           c_sink, d_qk_g, d_rpb, norm2_g, w_up, conv_w, conv_b, w_down):
    outs = [_forward(x[b], c[b], ctx[b], c_ctx, w_ada, b_ada, norm1_g, w_in, w_out, a_qk_g, a_lambda, a_subln_g,
                     b_qk_g, c_qk_g, c_sink, d_qk_g, d_rpb, norm2_g, w_up, conv_w, conv_b, w_down)
            for b in range(x.shape[0])]
    return jnp.stack(outs)
```

```python
import functools
import math

import jax
import jax.numpy as jnp
from jax import lax
from jax.experimental import pallas as pl
from jax.experimental.pallas import tpu as pltpu

GRID_W = 64
HEAD_DIM = 128
A_SUB = HEAD_DIM // 2
ROPE_THETA = 10000.0
EPS = 1e-6
NEG_INF = -1e30
N_MOD = 6
C_WINDOW = 128
NA_KH = 8
NA_KW = 16
CONV_W = 3
GQA_GROUP = 4
LOG2E = math.log2(math.e)
LOGIT_BOUND_SLACK = 1.02
MAX_LOGIT_BOUND = 50.0

LANES = 128
BF16_SUBLANES = 16
VMEM_LIMIT = 56 * 1024 * 1024

F32 = jnp.float32
BF16 = jnp.bfloat16


def _params(sem, vmem=VMEM_LIMIT):
    return pltpu.CompilerParams(dimension_semantics=sem, vmem_limit_bytes=vmem)


def _dot(a, b):
    return jnp.dot(a, b, preferred_element_type=F32)


def _dot_nt(a, b):
    return lax.dot_general(a, b, (((1,), (1,)), ((), ())), preferred_element_type=F32)


def _dot_tn(a, b):
    return lax.dot_general(a, b, (((0,), (0,)), ((), ())), preferred_element_type=F32)


def _norm_mod(x, g, sh, sc):
    inv = lax.rsqrt(jnp.mean(x * x, axis=-1, keepdims=True) + EPS)
    return (x * inv * g) * (1.0 + sc) + sh


def _mod_kernel(cb_ref, w_ref, b_ref, o_ref):
    tn = w_ref.shape[2]
    for r in range(2):
        cv = cb_ref[r]
        a = cv * (1.0 / (1.0 + jnp.exp(-cv)))
        for ch in range(tn // LANES):
            sl = slice(ch * LANES, (ch + 1) * LANES)
            o_ref[0, r:r + 1, sl] = jnp.sum(w_ref[0, :, sl] * a, axis=0, keepdims=True) + b_ref[0, :, sl]


def _mod_call(c, c_ctx, w_ada, b_ada):
    depth, d, n = w_ada.shape
    tn = 1024
    cb = jnp.broadcast_to(jnp.stack([c, c_ctx])[:, :, None], (2, d, LANES))
    return pl.pallas_call(
        _mod_kernel,
        grid=(depth, n // tn),
        in_specs=[pl.BlockSpec((2, d, LANES), lambda i, j: (0, 0, 0)),
                  pl.BlockSpec((1, d, tn), lambda i, j: (i, 0, j)),
                  pl.BlockSpec((1, 1, tn), lambda i, j: (i, 0, j))],
        out_specs=pl.BlockSpec((1, 2, tn), lambda i, j: (i, 0, j)),
        out_shape=jax.ShapeDtypeStruct((depth, 2, n), F32),
        compiler_params=_params(("parallel", "parallel")),
        name="adaln_mod",
    )(cb, w_ada, b_ada.reshape(depth, 1, n))


EPI_CHUNK = 256


def _head_norm(p, gain, width):
    shift = int(math.log2(width))
    r = lax.broadcasted_iota(jnp.int32, (EPI_CHUNK, EPI_CHUNK), 0) >> shift
    c = lax.broadcasted_iota(jnp.int32, (EPI_CHUNK, EPI_CHUNK), 1) >> shift
    bd = jnp.where(r == c, 1.0, 0.0).astype(BF16)
    seg = _dot((p * p).astype(BF16), bd)
    return p * lax.rsqrt(seg * (1.0 / width) + EPS) * gain


def _rope(y, half, tabs):
    c_ref, s1_ref, s2_ref = tabs
    c, s1, s2 = c_ref[...], s1_ref[...], s2_ref[...]
    outs = []
    for k in range(y.shape[1] // LANES):
        yk = y[:, k * LANES:(k + 1) * LANES]
        outs.append(yk * c + pltpu.roll(yk, LANES - half, 1) * s1 + pltpu.roll(yk, half, 1) * s2)
    return jnp.concatenate(outs, axis=1)


def _inproj_kernel(*refs, classes, rope, tn):
    x_ref, g_ref, sh_ref, sc_ref, w_ref, gain_ref = refs[:6]
    tabs = {"A": refs[6:9], "B": refs[9:12]} if rope else None
    o_ref = refs[-1]
    h = _norm_mod(x_ref[...], g_ref[...], sh_ref[...], sc_ref[...]).astype(BF16)
    per = tn // EPI_CHUNK
    for t in range(len(classes) // per):
        p = _dot(h, w_ref[:, t * tn:(t + 1) * tn])
        for ci, (width, kind) in enumerate(classes[t * per:(t + 1) * per]):
            sl = slice(t * tn + ci * EPI_CHUNK, t * tn + (ci + 1) * EPI_CHUNK)
            y = p[:, ci * EPI_CHUNK:(ci + 1) * EPI_CHUNK]
            if width:
                y = _head_norm(y, gain_ref[:, sl], width)
            if kind and rope:
                y = _rope(y, 16 if kind == "A" else 32, tabs[kind])
            o_ref[:, sl] = y.astype(BF16)


def _inproj_call(x, g, sh, sc, w, layer, gain, classes, tabs, tm, name):
    m, d = x.shape
    n = w.shape[2]
    rope = tabs is not None
    const = lambda i: (0, 0)
    in_specs = [pl.BlockSpec((tm, d), lambda i: (i, 0)),
                pl.BlockSpec((1, d), const), pl.BlockSpec((1, d), const), pl.BlockSpec((1, d), const),
                pl.BlockSpec((None, d, n), lambda i: (layer, 0, 0), pipeline_mode=pl.Buffered(1)),
                pl.BlockSpec((1, n), const)]
    args = [x, g, sh, sc, w, gain]
    if rope:
        in_specs += [pl.BlockSpec((tm, LANES), lambda i: (i, 0))] * 6
        args += list(tabs)
    return pl.pallas_call(
        functools.partial(_inproj_kernel, classes=tuple(classes), rope=rope, tn=2 * EPI_CHUNK),
        grid=(m // tm,),
        in_specs=in_specs,
        out_specs=pl.BlockSpec((tm, n), lambda i: (i, 0)),
        out_shape=jax.ShapeDtypeStruct((m, n), BF16),
        compiler_params=_params(("parallel",)),
        name=name,
    )(*args)


def _rope_tables(s, half):
    n_rows = s // GRID_W
    freqs = ROPE_THETA ** (-jnp.arange(half, dtype=F32) / half)
    zeros = jnp.zeros((s, half), F32)
    cs, s1s, s2s = [], [], []
    for count, expand in ((n_rows, lambda a: jnp.repeat(a, GRID_W, axis=0)),
                          (GRID_W, lambda a: jnp.tile(a, (n_rows, 1)))):
        ang = jnp.arange(count, dtype=jnp.int32).astype(F32)[:, None] * freqs[None, :]
        cos, sin = expand(jnp.cos(ang)), expand(jnp.sin(ang))
        cs += [cos, cos]
        s1s += [-sin, zeros]
        s2s += [zeros, sin]
    reps = LANES // (4 * half)
    return tuple(jnp.tile(jnp.concatenate(t, axis=1), (1, reps)) for t in (cs, s1s, s2s))


def _flash_kernel(*refs, mode, tq, tk, n_lat, lam_init, bounded):
    if bounded:
        bound = refs[0][0]
        refs = refs[1:]
    q_ref, kc_ref, vc_ref = refs[:3]
    pos = 3
    if n_lat:
        k_ref, v_ref = refs[3:5]
        pos = 5
    if mode == "diff":
        lam_ref, sg_ref = refs[pos:pos + 2]
        pos += 2
    o_ref = refs[pos]

    if mode == "diff":
        q = q_ref[...].astype(F32)
        lane = lax.broadcasted_iota(jnp.int32, q.shape, 1)
        qs = jnp.concatenate([jnp.where(lane < A_SUB, q, 0.0), jnp.where(lane >= A_SUB, q, 0.0)],
                             axis=0).astype(BF16)
    else:
        qs = jnp.concatenate([q_ref[:, g * HEAD_DIM:(g + 1) * HEAD_DIM] for g in range(GQA_GROUP)], axis=0)
    rows = qs.shape[0]

    def pv_dot(v, p):
        return lax.dot_general(v, p.astype(BF16), (((0,), (0,)), ((), ())), preferred_element_type=F32)

    if bounded:
        def step(k, v, carry):
            _, l, acc = carry
            p = jnp.exp2(_dot_nt(k, qs) - bound)
            return carry[0], l + jnp.sum(p, axis=0, keepdims=True), acc + pv_dot(v, p)
    else:
        def step(k, v, carry):
            m, l, acc = carry
            s = _dot_nt(k, qs)
            m_new = jnp.maximum(m, jnp.max(s, axis=0, keepdims=True))
            p = jnp.exp2(s - m_new)
            alpha = jnp.exp2(m - m_new)
            return m_new, alpha * l + jnp.sum(p, axis=0, keepdims=True), alpha * acc + pv_dot(v, p)

    carry = (jnp.full((1, rows), NEG_INF, F32), jnp.zeros((1, rows), F32), jnp.zeros((HEAD_DIM, rows), F32))
    carry = step(kc_ref[...], vc_ref[...], carry)
    if n_lat:
        def body(c, carry):
            off = pl.multiple_of(c * tk, tk)
            return step(k_ref[pl.ds(off, tk), :], v_ref[pl.ds(off, tk), :], carry)
        carry = lax.fori_loop(0, n_lat, body, carry, unroll=8)
    _, l, acc = carry
    o = (acc * (1.0 / l)).T
    if mode == "diff":
        lp = lam_ref[...]
        lam = (jnp.exp(jnp.sum(lp[0:1] * lp[1:2], axis=1, keepdims=True))
               - jnp.exp(jnp.sum(lp[2:3] * lp[3:4], axis=1, keepdims=True)) + lam_init)
        dlt = o[:tq] - lam * o[tq:]
        y = dlt * lax.rsqrt(jnp.mean(dlt * dlt, axis=-1, keepdims=True) + EPS) * sg_ref[...]
        o_ref[...] = (y * (1.0 - lam_init)).astype(BF16)
    else:
        for g in range(GQA_GROUP):
            o_ref[:, g * HEAD_DIM:(g + 1) * HEAD_DIM] = o[g * tq:(g + 1) * tq].astype(BF16)


def _flash_call(mode, qsrc, ksrc, pc, offs, n_heads, lam_p, subln, lam_init, tq, name, bound=None):
    sq = qsrc.shape[0]
    t = pc.shape[0]
    qo, ko, vo = offs
    tk = 1024
    qw = HEAD_DIM if mode == "diff" else GQA_GROUP * HEAD_DIM
    qb, kb, vb = qo // qw, ko // HEAD_DIM, vo // HEAD_DIM
    in_specs = [pl.BlockSpec((tq, qw), lambda h, i: (i, qb + h)),
                pl.BlockSpec((t, HEAD_DIM), lambda h, i: (0, kb + h)),
                pl.BlockSpec((t, HEAD_DIM), lambda h, i: (0, vb + h))]
    args = [qsrc, pc, pc]
    if bound is not None:
        in_specs = [pl.BlockSpec(memory_space=pltpu.SMEM)] + in_specs
        args = [bound] + args
    n_lat = 0
    if ksrc is not None:
        s = ksrc.shape[0]
        tk = min(tk, s)
        assert s % tk == 0
        n_lat = s // tk
        in_specs += [pl.BlockSpec((s, HEAD_DIM), lambda h, i: (0, kb + h)),
                     pl.BlockSpec((s, HEAD_DIM), lambda h, i: (0, vb + h))]
        args += [ksrc, ksrc]
    if mode == "diff":
        in_specs += [pl.BlockSpec(lam_p.shape, lambda h, i: (0, 0)),
                     pl.BlockSpec((1, HEAD_DIM), lambda h, i: (0, 0))]
        args += [lam_p, subln.reshape(1, HEAD_DIM)]
    return pl.pallas_call(
        functools.partial(_flash_kernel, mode=mode, tq=tq, tk=tk, n_lat=n_lat, lam_init=lam_init,
                          bounded=bound is not None),
        grid=(n_heads, sq // tq),
        in_specs=in_specs,
        out_specs=pl.BlockSpec((tq, qw), lambda h, i: (i, h)),
        out_shape=jax.ShapeDtypeStruct((sq, n_heads * qw), BF16),
        compiler_params=_params(("parallel", "parallel")),
        name=name,
    )(*args)


WIN_PAIR = 2


def _win_kernel(sink_ref, q_ref, k_ref, v_ref, kc_ref, vc_ref, o_ref, *, tq, s_len):
    kvh = pl.program_id(0)
    q0 = pl.program_id(1) * tq
    band = tq + 2 * C_WINDOW
    pair = WIN_PAIR
    rows = pair * tq
    start = pl.multiple_of(jnp.clip(q0 - C_WINDOW, 0, s_len - band), C_WINDOW)
    kb = k_ref[pl.ds(start, band), :]
    vb = v_ref[pl.ds(start, band), :]
    lane = lax.broadcasted_iota(jnp.int32, (band, rows), 1)
    qpos = q0 + (lane & (tq - 1))
    kpos = start + lax.broadcasted_iota(jnp.int32, (band, rows), 0)
    valid = jnp.abs(qpos - kpos) <= C_WINDOW
    first = lax.broadcasted_iota(jnp.int32, (1, rows), 1) < tq
    passes = []
    for g0 in range(0, GQA_GROUP, pair):
        qs = jnp.concatenate([q_ref[:, g * HEAD_DIM:(g + 1) * HEAD_DIM] for g in range(g0, g0 + pair)], axis=0)
        passes.append((_dot_nt(kb, qs), _dot_nt(kc_ref[...], qs)))
    for g0, (s_lat, s_ctx) in zip(range(0, GQA_GROUP, pair), passes):
        s_lat = jnp.where(valid, s_lat, NEG_INF)
        sink = jnp.where(first, sink_ref[kvh * GQA_GROUP + g0], sink_ref[kvh * GQA_GROUP + g0 + 1]) * LOG2E
        m = jnp.maximum(jnp.maximum(jnp.max(s_lat, axis=0, keepdims=True),
                                    jnp.max(s_ctx, axis=0, keepdims=True)), sink)
        p_lat = jnp.exp2(s_lat - m)
        p_ctx = jnp.exp2(s_ctx - m)
        l = jnp.sum(p_lat, axis=0, keepdims=True) + jnp.sum(p_ctx, axis=0, keepdims=True) + jnp.exp2(sink - m)
        o = ((_dot_tn(vc_ref[...], p_ctx.astype(BF16)) + _dot_tn(vb, p_lat.astype(BF16))) * (1.0 / l)).T
        for g in range(pair):
            o_ref[:, (g0 + g) * HEAD_DIM:(g0 + g + 1) * HEAD_DIM] = o[g * tq:(g + 1) * tq].astype(BF16)


def _win_call(p, pc, offs, n_kv, sink, tq):
    s = p.shape[0]
    t = pc.shape[0]
    qo, ko, vo = offs
    qw = GQA_GROUP * HEAD_DIM
    qb, kb, vb = qo // qw, ko // HEAD_DIM, vo // HEAD_DIM
    assert s >= tq + 2 * C_WINDOW and tq & (tq - 1) == 0
    return pl.pallas_call(
        functools.partial(_win_kernel, tq=tq, s_len=s),
        grid=(n_kv, s // tq),
        in_specs=[pl.BlockSpec(memory_space=pltpu.SMEM),
                  pl.BlockSpec((tq, qw), lambda h, i: (i, qb + h)),
                  pl.BlockSpec((s, HEAD_DIM), lambda h, i: (0, kb + h)),
                  pl.BlockSpec((s, HEAD_DIM), lambda h, i: (0, vb + h)),
                  pl.BlockSpec((t, HEAD_DIM), lambda h, i: (0, kb + h)),
                  pl.BlockSpec((t, HEAD_DIM), lambda h, i: (0, vb + h))],
        out_specs=pl.BlockSpec((tq, qw), lambda h, i: (i, h)),
        out_shape=jax.ShapeDtypeStruct((s, n_kv * qw), BF16),
        compiler_params=_params(("parallel", "parallel")),
        name="window_attn",
    )(sink, p, p, p, pc, pc)


NBR_Q_ROWS = 4
NBR_WIN_ROWS = NBR_Q_ROWS + NA_KH


def _nbr_build_bias(rpb_ref, head, mats_scr, tab_scr, n_rows):
    na, nb = 2 * NA_KH - 1, 2 * NA_KW - 1
    shape = (GRID_W, 2 * GRID_W)
    wk = lax.broadcasted_iota(jnp.int32, shape, 0)
    lane = lax.broadcasted_iota(jnp.int32, shape, 1)
    wq = lane & (GRID_W - 1)
    dc = jnp.clip(wk - wq, -(NA_KW - 1), NA_KW - 1) + (NA_KW - 1)
    c_start = jnp.clip(wq - NA_KW // 2, 0, GRID_W - NA_KW)
    col_ok = jnp.logical_and(wk >= c_start, wk < c_start + NA_KW)

    def build(a, carry):
        base = (head * na + a) * nb
        mat = jnp.zeros(shape, F32)
        for b in range(nb):
            mat = jnp.where(dc == b, rpb_ref[base + b] * LOG2E, mat)
        mats_scr[a] = jnp.where(col_ok, mat, NEG_INF)
        return carry

    lax.fori_loop(0, na, build, 0)
    neg = jnp.full(shape, NEG_INF, F32)
    first_half = lane < GRID_W
    for cl, r0 in enumerate((0, NBR_Q_ROWS, n_rows - NBR_Q_ROWS)):
        ws = min(max(r0 - NA_KH // 2, 0), n_rows - NBR_WIN_ROWS)
        for kr in range(NBR_WIN_ROWS):
            for qp in range(NBR_Q_ROWS // 2):
                halves = []
                for r in (r0 + 2 * qp, r0 + 2 * qp + 1):
                    r_start = min(max(r - NA_KH // 2, 0), n_rows - NA_KH)
                    inside = r_start <= ws + kr < r_start + NA_KH
                    halves.append(mats_scr[ws + kr - r + NA_KH - 1] if inside else neg)
                tab_scr[cl, kr * GRID_W:(kr + 1) * GRID_W, qp * 2 * GRID_W:(qp + 1) * 2 * GRID_W] = (
                    jnp.where(first_half, halves[0], halves[1]))


NBR_HEADS_PER_STEP = 4


def _nbr_kernel(rpb_ref, q_ref, k_ref, v_ref, kc_ref, vc_ref, o_ref, mats_scr, tab_scr, *, tq, win, s_len):
    i = pl.program_id(1)
    nq = pl.num_programs(1)

    @pl.when(i == 0)
    def _():
        for hh in range(NBR_HEADS_PER_STEP):
            _nbr_build_bias(rpb_ref, pl.program_id(0) * NBR_HEADS_PER_STEP + hh, mats_scr, tab_scr.at[hh],
                            s_len // GRID_W)

    q0 = i * tq
    lead = (NA_KH // 2) * GRID_W
    start = pl.multiple_of(jnp.clip(q0 - lead, 0, s_len - win), lead)
    cls = jnp.where(i == 0, 0, jnp.where(i == nq - 1, 2, 1))
    heads = [slice(hh * HEAD_DIM, (hh + 1) * HEAD_DIM) for hh in range(NBR_HEADS_PER_STEP)]

    def scores(sl):
        return _dot_nt(k_ref[pl.ds(start, win), sl], q_ref[:, sl]), _dot_nt(kc_ref[:, sl], q_ref[:, sl])

    ahead = 2
    pending = [scores(sl) for sl in heads[:ahead]]
    for hh, sl in enumerate(heads):
        s_win, s_ctx = pending.pop(0)
        s_win = s_win + tab_scr[hh, cls]
        m = jnp.maximum(jnp.max(s_win, axis=0, keepdims=True), jnp.max(s_ctx, axis=0, keepdims=True))
        p_win = jnp.exp2(s_win - m)
        p_ctx = jnp.exp2(s_ctx - m)
        l = jnp.sum(p_win, axis=0, keepdims=True) + jnp.sum(p_ctx, axis=0, keepdims=True)
        o = (_dot_tn(vc_ref[:, sl], p_ctx.astype(BF16))
             + _dot_tn(v_ref[pl.ds(start, win), sl], p_win.astype(BF16)))
        o_ref[:, sl] = ((o * (1.0 / l)).T).astype(BF16)
        if hh + ahead < len(heads):
            pending.append(scores(heads[hh + ahead]))


def _nbr_call(p, pc, offs, n_heads, rpb):
    s = p.shape[0]
    t = pc.shape[0]
    tq = NBR_Q_ROWS * GRID_W
    win = NBR_WIN_ROWS * GRID_W
    bw = NBR_HEADS_PER_STEP * HEAD_DIM
    qb, kb, vb = (o // bw for o in offs)
    assert s // GRID_W >= NBR_WIN_ROWS and rpb.shape[1:] == (2 * NA_KH - 1, 2 * NA_KW - 1)
    assert n_heads % NBR_HEADS_PER_STEP == 0 and all(o % bw == 0 for o in offs)
    return pl.pallas_call(
        functools.partial(_nbr_kernel, tq=tq, win=win, s_len=s),
        grid=(n_heads // NBR_HEADS_PER_STEP, s // tq),
        in_specs=[pl.BlockSpec(memory_space=pltpu.SMEM),
                  pl.BlockSpec((tq, bw), lambda h, i: (i, qb + h)),
                  pl.BlockSpec((s, bw), lambda h, i: (0, kb + h)),
                  pl.BlockSpec((s, bw), lambda h, i: (0, vb + h)),
                  pl.BlockSpec((t, bw), lambda h, i: (0, kb + h)),
                  pl.BlockSpec((t, bw), lambda h, i: (0, vb + h))],
        out_specs=pl.BlockSpec((tq, bw), lambda h, i: (i, h)),
        out_shape=jax.ShapeDtypeStruct((s, n_heads * HEAD_DIM), BF16),
        scratch_shapes=[pltpu.VMEM((2 * NA_KH - 1, GRID_W, 2 * GRID_W), F32),
                        pltpu.VMEM((NBR_HEADS_PER_STEP, 3, win, tq), F32)],
        compiler_params=_params(("parallel", "arbitrary")),
        name="nbr_attn",
    )(rpb.astype(F32).reshape(-1), p, p, p, pc, pc)


def _outproj_kernel(x_ref, ya_ref, yb_ref, wa_ref, wb_ref, g_ref, o_ref):
    acc = _dot(ya_ref[...], wa_ref[...]) + _dot(yb_ref[...], wb_ref[...])
    o_ref[...] = x_ref[...] + g_ref[...] * acc


def _outproj_call(x, ya, yb, w, layer, gate, tm, name):
    m, d = x.shape
    k = ya.shape[1]
    assert yb.shape[1] == k and w.shape[1] == 2 * k
    return pl.pallas_call(
        _outproj_kernel,
        grid=(m // tm,),
        in_specs=[pl.BlockSpec((tm, d), lambda i: (i, 0)),
                  pl.BlockSpec((tm, k), lambda i: (i, 0)),
                  pl.BlockSpec((tm, k), lambda i: (i, 0)),
                  pl.BlockSpec((None, k, d), lambda i: (layer, 0, 0)),
                  pl.BlockSpec((None, k, d), lambda i: (layer, 1, 0)),
                  pl.BlockSpec((1, d), lambda i: (0, 0))],
        out_specs=pl.BlockSpec((tm, d), lambda i: (i, 0)),
        out_shape=jax.ShapeDtypeStruct((m, d), F32),
        compiler_params=_params(("parallel",)),
        name=name,
    )(x, ya, yb, w, w, gate)


HALO = 8


def _ffn_kernel(xm_ref, xp_ref, xn_ref, gn_ref, sh_ref, sc_ref, gate_ref, wa_ref, wg_ref,
                cwa_ref, cwg_ref, cba_ref, cbg_ref, wd_ref, o_ref,
                h_scr, acc_scr, ua0_scr, ua1_scr, ug0_scr, ug1_scr):
    i, j = pl.program_id(0), pl.program_id(1)
    ni, nj = pl.num_programs(0), pl.num_programs(1)
    tm = xm_ref.shape[0]

    @pl.when(j == 0)
    def _():
        nm = lambda x: _norm_mod(x, gn_ref[...], sh_ref[...], sc_ref[...])
        h_scr[0:tm, :] = nm(xm_ref[...]).astype(BF16)
        h_scr[tm:, :] = jnp.concatenate([jnp.where(i < ni - 1, nm(xn_ref[...]), 0.0),
                                         jnp.where(i > 0, nm(xp_ref[...]), 0.0)], axis=0).astype(BF16)
        acc_scr[...] = jnp.zeros_like(acc_scr)

    def up(u_scr, w_ref, sl):
        res = _dot(h_scr[...], w_ref[:, sl])
        u_scr[HALO:HALO + tm, :] = res[0:tm]
        u_scr[0:HALO, :] = res[tm + HALO:]
        u_scr[HALO + tm:, :] = res[tm:tm + HALO]

    def conv(u_scr, cw_ref, cb_ref, sl):
        out = cb_ref[:, sl] + u_scr[pl.ds(HALO - 1, tm), :] * cw_ref[0:1, sl]
        for t in range(1, CONV_W):
            out = out + u_scr[pl.ds(HALO - 1 + t, tm), :] * cw_ref[t:t + 1, sl]
        return out

    def silu_conv(u_scr, sl):
        g = conv(u_scr, cwg_ref, cbg_ref, sl)
        return g * (1.0 / (1.0 + jnp.exp(-g)))

    half = wa_ref.shape[1] // 2
    lo, hi = slice(0, half), slice(half, 2 * half)
    up(ug0_scr, wg_ref, lo)
    up(ug1_scr, wg_ref, hi)
    sg0 = silu_conv(ug0_scr, lo)
    up(ua0_scr, wa_ref, lo)
    sg1 = silu_conv(ug1_scr, hi)
    up(ua1_scr, wa_ref, hi)
    z0 = sg0 * conv(ua0_scr, cwa_ref, cba_ref, lo)
    acc_scr[...] += _dot(z0.astype(BF16), wd_ref[lo, :])
    z1 = sg1 * conv(ua1_scr, cwa_ref, cba_ref, hi)
    acc_scr[...] += _dot(z1.astype(BF16), wd_ref[hi, :])

    @pl.when(j == nj - 1)
    def _():
        o_ref[...] = xm_ref[...] + gate_ref[...] * acc_scr[...]


def _ffn_call(x, gn, sh, sc, gate, w_up, conv_w, conv_b, w_down, layer, tm, tf, name):
    m, d = x.shape
    f = w_down.shape[1]
    nfb = f // tf
    hb = tm // HALO
    last = m // HALO - 1
    row = lambda i, j: (0, 0)
    return pl.pallas_call(
        _ffn_kernel,
        grid=(m // tm, nfb),
        in_specs=[pl.BlockSpec((tm, d), lambda i, j: (i, 0)),
                  pl.BlockSpec((HALO, d), lambda i, j: (jnp.maximum(i * hb - 1, 0), 0)),
                  pl.BlockSpec((HALO, d), lambda i, j: (jnp.minimum((i + 1) * hb, last), 0)),
                  pl.BlockSpec((1, d), row), pl.BlockSpec((1, d), row), pl.BlockSpec((1, d), row),
                  pl.BlockSpec((1, d), row),
                  pl.BlockSpec((None, d, tf), lambda i, j: (layer, 0, j)),
                  pl.BlockSpec((None, d, tf), lambda i, j: (layer, 0, nfb + j)),
                  pl.BlockSpec((None, CONV_W, tf), lambda i, j: (layer, 0, j)),
                  pl.BlockSpec((None, CONV_W, tf), lambda i, j: (layer, 0, nfb + j)),
                  pl.BlockSpec((None, 1, tf), lambda i, j: (layer, 0, j)),
                  pl.BlockSpec((None, 1, tf), lambda i, j: (layer, 0, nfb + j)),
                  pl.BlockSpec((None, tf, d), lambda i, j: (layer, j, 0))],
        out_specs=pl.BlockSpec((tm, d), lambda i, j: (i, 0)),
        out_shape=jax.ShapeDtypeStruct((m, d), F32),
        scratch_shapes=[pltpu.VMEM((tm + 2 * HALO, d), BF16),
                        pltpu.VMEM((tm, d), F32)] + [pltpu.VMEM((tm + 2 * HALO, tf // 2), F32)] * 4,
        compiler_params=_params(("parallel", "arbitrary")),
        name=name,
    )(x, x, x, gn, sh, sc, gate, w_up, w_up, conv_w, conv_w, conv_b, conv_b, w_down)


def _offsets(sizes):
    offs, o = [], 0
    for s in sizes:
        offs.append(o)
        o += s
    return offs


def _chunk_classes(sizes, classes):
    out = []
    for s, c in zip(sizes, classes):
        out += [c] * (s // EPI_CHUNK)
    return out


def _gain_vector(sizes, gains):
    parts = []
    for s, g in zip(sizes, gains):
        parts.append(jnp.ones((s,), F32) if g is None else jnp.tile(g.astype(F32), s // g.shape[0]))
    return jnp.concatenate(parts).reshape(1, -1)


def _lambda_init(layer_idx):
    return 0.8 - 0.6 * math.exp(-0.3 * layer_idx)


def _forward(x, c, ctx, c_ctx, w_ada, b_ada, norm1_g, w_in, w_out, a_qk_g, a_lambda, a_subln_g,
             b_qk_g, c_qk_g, c_sink, d_qk_g, d_rpb, norm2_g, w_up, conv_w, conv_b, w_down):
    s, d = x.shape
    t = ctx.shape[0]
    depth = w_ada.shape[0]
    n_heads = d // (2 * HEAD_DIM)
    n_kv = n_heads // GQA_GROUP
    hw = n_heads * HEAD_DIM
    kvw = n_kv * HEAD_DIM
    even_sizes = (hw, hw, hw, hw, kvw, kvw)
    odd_sizes = (hw, kvw, kvw, hw, hw, hw)
    tm = 512
    tf = 512

    mod = _mod_call(c, c_ctx, w_ada, b_ada)
    rope_a = _rope_tables(s, 16)
    rope_b = _rope_tables(s, 32)
    tabs = rope_a + rope_b
    scale_a = A_SUB ** -0.5
    scale = HEAD_DIM ** -0.5

    w_in_b, w_out_b, w_up_b, w_down_b = (w.astype(BF16) for w in (w_in, w_out, w_up, w_down))
    conv_b3 = conv_b.reshape(depth, 1, -1)

    xc = ctx
    for i in range(depth):
        need_ctx = i < depth - 1
        e = i // 2
        vec = lambda r, k: mod[i, r:r + 1, k * d:(k + 1) * d]
        g1n = norm1_g[i].reshape(1, d)
        g2n = norm2_g[i].reshape(1, d)
        if i % 2 == 0:
            sizes = even_sizes
            classes = [(A_SUB, "A"), (A_SUB, "A"), (0, None), (HEAD_DIM, "B"), (HEAD_DIM, "B"), (0, None)]
            gains = [a_qk_g[e, 0] * (scale_a * LOG2E), a_qk_g[e, 1], None,
                     b_qk_g[e, 0] * (scale * LOG2E), b_qk_g[e, 1], None]
        else:
            sizes = odd_sizes
            classes = [(HEAD_DIM, "B"), (HEAD_DIM, "B"), (0, None), (HEAD_DIM, None), (HEAD_DIM, None), (0, None)]
            gains = [c_qk_g[e, 0] * (scale * LOG2E), c_qk_g[e, 1], None,
                     d_qk_g[e, 0] * (scale * LOG2E), d_qk_g[e, 1], None]
        offs = _offsets(sizes)
        chunk_cls = _chunk_classes(sizes, classes)
        gain = _gain_vector(sizes, gains)

        p = _inproj_call(x, g1n, vec(0, 0), vec(0, 1), w_in_b, i, gain, chunk_cls, tabs, tm, f"inproj{i}")
        pc = _inproj_call(xc, g1n, vec(1, 0), vec(1, 1), w_in_b, i, gain, chunk_cls, None, t, f"inproj_ctx{i}")

        if i % 2 == 0:
            lam_init = _lambda_init(i)
            fa = functools.partial(_flash_call, "diff", offs=offs[0:3], n_heads=n_heads, lam_p=a_lambda[e],
                                   subln=a_subln_g[e], lam_init=lam_init)
            fb = functools.partial(_flash_call, "gqa", offs=offs[3:6], n_heads=n_kv, lam_p=None,
                                   subln=None, lam_init=None)
            bound_a = (A_SUB * LOGIT_BOUND_SLACK) * jnp.max(jnp.abs(gains[0])) * jnp.max(jnp.abs(gains[1]))
            bound_b = (HEAD_DIM * LOGIT_BOUND_SLACK) * jnp.max(jnp.abs(gains[3])) * jnp.max(jnp.abs(gains[4]))
            ya = lax.cond(bound_a <= MAX_LOGIT_BOUND,
                          lambda: fa(p, p, pc, tq=512, name=f"diff_attn{i}", bound=bound_a.reshape(1)),
                          lambda: fa(p, p, pc, tq=512, name=f"diff_attn_online{i}"))
            yb = lax.cond(bound_b <= MAX_LOGIT_BOUND,
                          lambda: fb(p, p, pc, tq=256, name=f"gqa_attn{i}", bound=bound_b.reshape(1)),
                          lambda: fb(p, p, pc, tq=256, name=f"gqa_attn_online{i}"))
            if need_ctx:
                ya_c = fa(pc, None, pc, tq=t, name=f"diff_attn_ctx{i}")
                yb_c = fb(pc, None, pc, tq=t, name=f"gqa_attn_ctx{i}")
        else:
            ya = _win_call(p, pc, offs[0:3], n_kv, c_sink[e], 256)
            yb = _nbr_call(p, pc, offs[3:6], n_heads, d_rpb[e])
            if need_ctx:
                raise NotImplementedError("context stream after an odd layer")

        x = _outproj_call(x, ya, yb, w_out_b, i, vec(0, 2), tm, f"outproj{i}")
        x = _ffn_call(x, g2n, vec(0, 3), vec(0, 4), vec(0, 5), w_up_b, conv_w, conv_b3, w_down_b, i, tm, tf,
                      f"ffn{i}")
        if need_ctx:
            xc = _outproj_call(xc, ya_c, yb_c, w_out_b, i, vec(1, 2), t, f"outproj_ctx{i}")
            xc = _ffn_call(xc, g2n, vec(1, 3), vec(1, 4), vec(1, 5), w_up_b, conv_w, conv_b3, w_down_b, i, t, tf,
                           f"ffn_ctx{i}")
    return x


def kernel(x, c, ctx, c_ctx, w_ada, b_ada, norm1_g, w_in, w_out, a_qk_g, a_lambda, a_subln_g, b_qk_g, c_qk_g,
           c_sink, d_qk_g, d_rpb, norm2_g, w_up, conv_w, conv_b, w_down):
    outs = [_forward(x[b], c[b], ctx[b], c_ctx, w_ada, b_ada, norm1_g, w_in, w_out, a_qk_g, a_lambda, a_subln_g,
                     b_qk_g, c_qk_g, c_sink, d_qk_g, d_rpb, norm2_g, w_up, conv_w, conv_b, w_down)
            for b in range(x.shape[0])]
    return jnp.stack(outs)
```

```python
import functools
import math

import jax
import jax.numpy as jnp
from jax import lax
from jax.experimental import pallas as pl
from jax.experimental.pallas import tpu as pltpu

GRID_W = 64
HEAD_DIM = 128
A_SUB = HEAD_DIM // 2
ROPE_THETA = 10000.0
EPS = 1e-6
NEG_INF = -1e30
N_MOD = 6
C_WINDOW = 128
NA_KH = 8
NA_KW = 16
CONV_W = 3
GQA_GROUP = 4
LOG2E = math.log2(math.e)
LOGIT_BOUND_SLACK = 1.02
MAX_LOGIT_BOUND = 50.0

LANES = 128
BF16_SUBLANES = 16
VMEM_LIMIT = 56 * 1024 * 1024

F32 = jnp.float32
BF16 = jnp.bfloat16


def _params(sem, vmem=VMEM_LIMIT):
    return pltpu.CompilerParams(dimension_semantics=sem, vmem_limit_bytes=vmem)


def _dot(a, b):
    return jnp.dot(a, b, preferred_element_type=F32)


def _dot_nt(a, b):
    return lax.dot_general(a, b, (((1,), (1,)), ((), ())), preferred_element_type=F32)


def _dot_tn(a, b):
    return lax.dot_general(a, b, (((0,), (0,)), ((), ())), preferred_element_type=F32)


def _norm_mod(x, g, sh, sc):
    inv = lax.rsqrt(jnp.mean(x * x, axis=-1, keepdims=True) + EPS)
    return (x * inv * g) * (1.0 + sc) + sh


def _mod_kernel(cb_ref, w_ref, b_ref, o_ref):
    tn = w_ref.shape[2]
    for r in range(2):
        cv = cb_ref[r]
        a = cv * (1.0 / (1.0 + jnp.exp(-cv)))
        for ch in range(tn // LANES):
            sl = slice(ch * LANES, (ch + 1) * LANES)
            o_ref[0, r:r + 1, sl] = jnp.sum(w_ref[0, :, sl] * a, axis=0, keepdims=True) + b_ref[0, :, sl]


def _mod_call(c, c_ctx, w_ada, b_ada):
    depth, d, n = w_ada.shape
    tn = 1024
    cb = jnp.broadcast_to(jnp.stack([c, c_ctx])[:, :, None], (2, d, LANES))
    return pl.pallas_call(
        _mod_kernel,
        grid=(depth, n // tn),
        in_specs=[pl.BlockSpec((2, d, LANES), lambda i, j: (0, 0, 0)),
                  pl.BlockSpec((1, d, tn), lambda i, j: (i, 0, j)),
                  pl.BlockSpec((1, 1, tn), lambda i, j: (i, 0, j))],
        out_specs=pl.BlockSpec((1, 2, tn), lambda i, j: (i, 0, j)),
        out_shape=jax.ShapeDtypeStruct((depth, 2, n), F32),
        compiler_params=_params(("parallel", "parallel")),
        name="adaln_mod",
    )(cb, w_ada, b_ada.reshape(depth, 1, n))


EPI_CHUNK = 256


def _head_norm(p, gain, width):
    shift = int(math.log2(width))
    r = lax.broadcasted_iota(jnp.int32, (EPI_CHUNK, EPI_CHUNK), 0) >> shift
    c = lax.broadcasted_iota(jnp.int32, (EPI_CHUNK, EPI_CHUNK), 1) >> shift
    bd = jnp.where(r == c, 1.0, 0.0).astype(BF16)
    seg = _dot((p * p).astype(BF16), bd)
    return p * lax.rsqrt(seg * (1.0 / width) + EPS) * gain


def _rope(y, half, tabs):
    c_ref, s1_ref, s2_ref = tabs
    c, s1, s2 = c_ref[...], s1_ref[...], s2_ref[...]
    outs = []
    for k in range(y.shape[1] // LANES):
        yk = y[:, k * LANES:(k + 1) * LANES]
        outs.append(yk * c + pltpu.roll(yk, LANES - half, 1) * s1 + pltpu.roll(yk, half, 1) * s2)
    return jnp.concatenate(outs, axis=1)


def _inproj_kernel(*refs, classes, rope, tn):
    x_ref, g_ref, sh_ref, sc_ref, w_ref, gain_ref = refs[:6]
    tabs = {"A": refs[6:9], "B": refs[9:12]} if rope else None
    o_ref = refs[-1]
    h = _norm_mod(x_ref[...], g_ref[...], sh_ref[...], sc_ref[...]).astype(BF16)
    per = tn // EPI_CHUNK
    for t in range(len(classes) // per):
        p = _dot(h, w_ref[:, t * tn:(t + 1) * tn])
        for ci, (width, kind) in enumerate(classes[t * per:(t + 1) * per]):
            sl = slice(t * tn + ci * EPI_CHUNK, t * tn + (ci + 1) * EPI_CHUNK)
            y = p[:, ci * EPI_CHUNK:(ci + 1) * EPI_CHUNK]
            if width:
                y = _head_norm(y, gain_ref[:, sl], width)
            if kind and rope:
                y = _rope(y, 16 if kind == "A" else 32, tabs[kind])
            o_ref[:, sl] = y.astype(BF16)


def _inproj_call(x, g, sh, sc, w, layer, gain, classes, tabs, tm, name):
    m, d = x.shape
    n = w.shape[2]
    rope = tabs is not None
    const = lambda i: (0, 0)
    in_specs = [pl.BlockSpec((tm, d), lambda i: (i, 0)),
                pl.BlockSpec((1, d), const), pl.BlockSpec((1, d), const), pl.BlockSpec((1, d), const),
                pl.BlockSpec((None, d, n), lambda i: (layer, 0, 0), pipeline_mode=pl.Buffered(1)),
                pl.BlockSpec((1, n), const)]
    args = [x, g, sh, sc, w, gain]
    if rope:
        in_specs += [pl.BlockSpec((tm, LANES), lambda i: (i, 0))] * 6
        args += list(tabs)
    return pl.pallas_call(
        functools.partial(_inproj_kernel, classes=tuple(classes), rope=rope, tn=2 * EPI_CHUNK),
        grid=(m // tm,),
        in_specs=in_specs,
        out_specs=pl.BlockSpec((tm, n), lambda i: (i, 0)),
        out_shape=jax.ShapeDtypeStruct((m, n), BF16),
        compiler_params=_params(("parallel",)),
        name=name,
    )(*args)


def _rope_tables(s, half):
    n_rows = s // GRID_W
    freqs = ROPE_THETA ** (-jnp.arange(half, dtype=F32) / half)
    zeros = jnp.zeros((s, half), F32)
    cs, s1s, s2s = [], [], []
    for count, expand in ((n_rows, lambda a: jnp.repeat(a, GRID_W, axis=0)),
                          (GRID_W, lambda a: jnp.tile(a, (n_rows, 1)))):
        ang = jnp.arange(count, dtype=jnp.int32).astype(F32)[:, None] * freqs[None, :]
        cos, sin = expand(jnp.cos(ang)), expand(jnp.sin(ang))
        cs += [cos, cos]
        s1s += [-sin, zeros]
        s2s += [zeros, sin]
    reps = LANES // (4 * half)
    return tuple(jnp.tile(jnp.concatenate(t, axis=1), (1, reps)) for t in (cs, s1s, s2s))


def _flash_kernel(*refs, mode, tq, tk, n_lat, lam_init, bounded):
    if bounded:
        bound = refs[0][0]
        refs = refs[1:]
    q_ref, kc_ref, vc_ref = refs[:3]
    pos = 3
    if n_lat:
        k_ref, v_ref = refs[3:5]
        pos = 5
    if mode == "diff":
        lam_ref, sg_ref = refs[pos:pos + 2]
        pos += 2
    o_ref = refs[pos]

    if mode == "diff":
        q = q_ref[...].astype(F32)
        lane = lax.broadcasted_iota(jnp.int32, q.shape, 1)
        qs = jnp.concatenate([jnp.where(lane < A_SUB, q, 0.0), jnp.where(lane >= A_SUB, q, 0.0)],
                             axis=0).astype(BF16)
    else:
        qs = jnp.concatenate([q_ref[:, g * HEAD_DIM:(g + 1) * HEAD_DIM] for g in range(GQA_GROUP)], axis=0)
    rows = qs.shape[0]

    def pv_dot(v, p):
        return lax.dot_general(v, p.astype(BF16), (((0,), (0,)), ((), ())), preferred_element_type=F32)

    if bounded:
        def step(k, v, carry):
            _, l, acc = carry
            p = jnp.exp2(_dot_nt(k, qs) - bound)
            return carry[0], l + jnp.sum(p, axis=0, keepdims=True), acc + pv_dot(v, p)
    else:
        def step(k, v, carry):
            m, l, acc = carry
            s = _dot_nt(k, qs)
            m_new = jnp.maximum(m, jnp.max(s, axis=0, keepdims=True))
            p = jnp.exp2(s - m_new)
            alpha = jnp.exp2(m - m_new)
            return m_new, alpha * l + jnp.sum(p, axis=0, keepdims=True), alpha * acc + pv_dot(v, p)

    carry = (jnp.full((1, rows), NEG_INF, F32), jnp.zeros((1, rows), F32), jnp.zeros((HEAD_DIM, rows), F32))
    carry = step(kc_ref[...], vc_ref[...], carry)
    if n_lat:
        def body(c, carry):
            off = pl.multiple_of(c * tk, tk)
            return step(k_ref[pl.ds(off, tk), :], v_ref[pl.ds(off, tk), :], carry)
        carry = lax.fori_loop(0, n_lat, body, carry, unroll=8)
    _, l, acc = carry
    o = (acc * (1.0 / l)).T
    if mode == "diff":
        lp = lam_ref[...]
        lam = (jnp.exp(jnp.sum(lp[0:1] * lp[1:2], axis=1, keepdims=True))
               - jnp.exp(jnp.sum(lp[2:3] * lp[3:4], axis=1, keepdims=True)) + lam_init)
        dlt = o[:tq] - lam * o[tq:]
        y = dlt * lax.rsqrt(jnp.mean(dlt * dlt, axis=-1, keepdims=True) + EPS) * sg_ref[...]
        o_ref[...] = (y * (1.0 - lam_init)).astype(BF16)
    else:
        for g in range(GQA_GROUP):
            o_ref[:, g * HEAD_DIM:(g + 1) * HEAD_DIM] = o[g * tq:(g + 1) * tq].astype(BF16)


def _flash_call(mode, qsrc, ksrc, pc, offs, n_heads, lam_p, subln, lam_init, tq, name, bound=None):
    sq = qsrc.shape[0]
    t = pc.shape[0]
    qo, ko, vo = offs
    tk = 1024
    qw = HEAD_DIM if mode == "diff" else GQA_GROUP * HEAD_DIM
    qb, kb, vb = qo // qw, ko // HEAD_DIM, vo // HEAD_DIM
    in_specs = [pl.BlockSpec((tq, qw), lambda h, i: (i, qb + h)),
                pl.BlockSpec((t, HEAD_DIM), lambda h, i: (0, kb + h)),
                pl.BlockSpec((t, HEAD_DIM), lambda h, i: (0, vb + h))]
    args = [qsrc, pc, pc]
    if bound is not None:
        in_specs = [pl.BlockSpec(memory_space=pltpu.SMEM)] + in_specs
        args = [bound] + args
    n_lat = 0
    if ksrc is not None:
        s = ksrc.shape[0]
        tk = min(tk, s)
        assert s % tk == 0
        n_lat = s // tk
        in_specs += [pl.BlockSpec((s, HEAD_DIM), lambda h, i: (0, kb + h)),
                     pl.BlockSpec((s, HEAD_DIM), lambda h, i: (0, vb + h))]
        args += [ksrc, ksrc]
    if mode == "diff":
        in_specs += [pl.BlockSpec(lam_p.shape, lambda h, i: (0, 0)),
                     pl.BlockSpec((1, HEAD_DIM), lambda h, i: (0, 0))]
        args += [lam_p, subln.reshape(1, HEAD_DIM)]
    return pl.pallas_call(
        functools.partial(_flash_kernel, mode=mode, tq=tq, tk=tk, n_lat=n_lat, lam_init=lam_init,
                          bounded=bound is not None),
        grid=(n_heads, sq // tq),
        in_specs=in_specs,
        out_specs=pl.BlockSpec((tq, qw), lambda h, i: (i, h)),
        out_shape=jax.ShapeDtypeStruct((sq, n_heads * qw), BF16),
        compiler_params=_params(("parallel", "parallel")),
        name=name,
    )(*args)


WIN_PAIR = 2


def _win_kernel(sink_ref, q_ref, k_ref, v_ref, kc_ref, vc_ref, o_ref, *, tq, s_len):
    kvh = pl.program_id(0)
    q0 = pl.program_id(1) * tq
    band = tq + 2 * C_WINDOW
    pair = WIN_PAIR
    rows = pair * tq
    start = pl.multiple_of(jnp.clip(q0 - C_WINDOW, 0, s_len - band), C_WINDOW)
    kb = k_ref[pl.ds(start, band), :]
    vb = v_ref[pl.ds(start, band), :]
    lane = lax.broadcasted_iota(jnp.int32, (band, rows), 1)
    qpos = q0 + (lane & (tq - 1))
    kpos = start + lax.broadcasted_iota(jnp.int32, (band, rows), 0)
    valid = jnp.abs(qpos - kpos) <= C_WINDOW
    first = lax.broadcasted_iota(jnp.int32, (1, rows), 1) < tq
    passes = []
    for g0 in range(0, GQA_GROUP, pair):
        qs = jnp.concatenate([q_ref[:, g * HEAD_DIM:(g + 1) * HEAD_DIM] for g in range(g0, g0 + pair)], axis=0)
        passes.append((_dot_nt(kb, qs), _dot_nt(kc_ref[...], qs)))
    for g0, (s_lat, s_ctx) in zip(range(0, GQA_GROUP, pair), passes):
        s_lat = jnp.where(valid, s_lat, NEG_INF)
        sink = jnp.where(first, sink_ref[kvh * GQA_GROUP + g0], sink_ref[kvh * GQA_GROUP + g0 + 1]) * LOG2E
        m = jnp.maximum(jnp.maximum(jnp.max(s_lat, axis=0, keepdims=True),
                                    jnp.max(s_ctx, axis=0, keepdims=True)), sink)
        p_lat = jnp.exp2(s_lat - m)
        p_ctx = jnp.exp2(s_ctx - m)
        l = jnp.sum(p_lat, axis=0, keepdims=True) + jnp.sum(p_ctx, axis=0, keepdims=True) + jnp.exp2(sink - m)
        o = ((_dot_tn(vc_ref[...], p_ctx.astype(BF16)) + _dot_tn(vb, p_lat.astype(BF16))) * (1.0 / l)).T
        for g in range(pair):
            o_ref[:, (g0 + g) * HEAD_DIM:(g0 + g + 1) * HEAD_DIM] = o[g * tq:(g + 1) * tq].astype(BF16)


def _win_call(p, pc, offs, n_kv, sink, tq):
    s = p.shape[0]
    t = pc.shape[0]
    qo, ko, vo = offs
    qw = GQA_GROUP * HEAD_DIM
    qb, kb, vb = qo // qw, ko // HEAD_DIM, vo // HEAD_DIM
    assert s >= tq + 2 * C_WINDOW and tq & (tq - 1) == 0
    return pl.pallas_call(
        functools.partial(_win_kernel, tq=tq, s_len=s),
        grid=(n_kv, s // tq),
        in_specs=[pl.BlockSpec(memory_space=pltpu.SMEM),
                  pl.BlockSpec((tq, qw), lambda h, i: (i, qb + h)),
                  pl.BlockSpec((s, HEAD_DIM), lambda h, i: (0, kb + h)),
                  pl.BlockSpec((s, HEAD_DIM), lambda h, i: (0, vb + h)),
                  pl.BlockSpec((t, HEAD_DIM), lambda h, i: (0, kb + h)),
                  pl.BlockSpec((t, HEAD_DIM), lambda h, i: (0, vb + h))],
        out_specs=pl.BlockSpec((tq, qw), lambda h, i: (i, h)),
        out_shape=jax.ShapeDtypeStruct((s, n_kv * qw), BF16),
        compiler_params=_params(("parallel", "parallel")),
        name="window_attn",
    )(sink, p, p, p, pc, pc)


NBR_Q_ROWS = 4
NBR_WIN_ROWS = NBR_Q_ROWS + NA_KH


def _nbr_build_bias(rpb_ref, head, mats_scr, tab_scr, n_rows):
    na, nb = 2 * NA_KH - 1, 2 * NA_KW - 1
    shape = (GRID_W, 2 * GRID_W)
    wk = lax.broadcasted_iota(jnp.int32, shape, 0)
    lane = lax.broadcasted_iota(jnp.int32, shape, 1)
    wq = lane & (GRID_W - 1)
    dc = jnp.clip(wk - wq, -(NA_KW - 1), NA_KW - 1) + (NA_KW - 1)
    c_start = jnp.clip(wq - NA_KW // 2, 0, GRID_W - NA_KW)
    col_ok = jnp.logical_and(wk >= c_start, wk < c_start + NA_KW)

    def build(a, carry):
        base = (head * na + a) * nb
        mat = jnp.zeros(shape, F32)
        for b in range(nb):
            mat = jnp.where(dc == b, rpb_ref[base + b] * LOG2E, mat)
        mats_scr[a] = jnp.where(col_ok, mat, NEG_INF)
        return carry

    lax.fori_loop(0, na, build, 0)
    neg = jnp.full(shape, NEG_INF, F32)
    first_half = lane < GRID_W
    for cl, r0 in enumerate((0, NBR_Q_ROWS, n_rows - NBR_Q_ROWS)):
        ws = min(max(r0 - NA_KH // 2, 0), n_rows - NBR_WIN_ROWS)
        for kr in range(NBR_WIN_ROWS):
            for qp in range(NBR_Q_ROWS // 2):
                halves = []
                for r in (r0 + 2 * qp, r0 + 2 * qp + 1):
                    r_start = min(max(r - NA_KH // 2, 0), n_rows - NA_KH)
                    inside = r_start <= ws + kr < r_start + NA_KH
                    halves.append(mats_scr[ws + kr - r + NA_KH - 1] if inside else neg)
                tab_scr[cl, kr * GRID_W:(kr + 1) * GRID_W, qp * 2 * GRID_W:(qp + 1) * 2 * GRID_W] = (
                    jnp.where(first_half, halves[0], halves[1]))


NBR_HEADS_PER_STEP = 4


def _nbr_kernel(rpb_ref, q_ref, k_ref, v_ref, kc_ref, vc_ref, o_ref, mats_scr, tab_scr, *, tq, win, s_len):
    i = pl.program_id(1)
    nq = pl.num_programs(1)

    @pl.when(i == 0)
    def _():
        for hh in range(NBR_HEADS_PER_STEP):
            _nbr_build_bias(rpb_ref, pl.program_id(0) * NBR_HEADS_PER_STEP + hh, mats_scr, tab_scr.at[hh],
                            s_len // GRID_W)

    q0 = i * tq
    lead = (NA_KH // 2) * GRID_W
    start = pl.multiple_of(jnp.clip(q0 - lead, 0, s_len - win), lead)
    cls = jnp.where(i == 0, 0, jnp.where(i == nq - 1, 2, 1))
    heads = [slice(hh * HEAD_DIM, (hh + 1) * HEAD_DIM) for hh in range(NBR_HEADS_PER_STEP)]

    def scores(sl):
        return _dot_nt(k_ref[pl.ds(start, win), sl], q_ref[:, sl]), _dot_nt(kc_ref[:, sl], q_ref[:, sl])

    ahead = 2
    pending = [scores(sl) for sl in heads[:ahead]]
    for hh, sl in enumerate(heads):
        s_win, s_ctx = pending.pop(0)
        s_win = s_win + tab_scr[hh, cls]
        m = jnp.maximum(jnp.max(s_win, axis=0, keepdims=True), jnp.max(s_ctx, axis=0, keepdims=True))
        p_win = jnp.exp2(s_win - m)
        p_ctx = jnp.exp2(s_ctx - m)
        l = jnp.sum(p_win, axis=0, keepdims=True) + jnp.sum(p_ctx, axis=0, keepdims=True)
        o = (_dot_tn(vc_ref[:, sl], p_ctx.astype(BF16))
             + _dot_tn(v_ref[pl.ds(start, win), sl], p_win.astype(BF16)))
        o_ref[:, sl] = ((o * (1.0 / l)).T).astype(BF16)
        if hh + ahead < len(heads):
            pending.append(scores(heads[hh + ahead]))


def _nbr_call(p, pc, offs, n_heads, rpb):
    s = p.shape[0]
    t = pc.shape[0]
    tq = NBR_Q_ROWS * GRID_W
    win = NBR_WIN_ROWS * GRID_W
    bw = NBR_HEADS_PER_STEP * HEAD_DIM
    qb, kb, vb = (o // bw for o in offs)
    assert s // GRID_W >= NBR_WIN_ROWS and rpb.shape[1:] == (2 * NA_KH - 1, 2 * NA_KW - 1)
    assert n_heads % NBR_HEADS_PER_STEP == 0 and all(o % bw == 0 for o in offs)
    return pl.pallas_call(
        functools.partial(_nbr_kernel, tq=tq, win=win, s_len=s),
        grid=(n_heads // NBR_HEADS_PER_STEP, s // tq),
        in_specs=[pl.BlockSpec(memory_space=pltpu.SMEM),
                  pl.BlockSpec((tq, bw), lambda h, i: (i, qb + h)),
                  pl.BlockSpec((s, bw), lambda h, i: (0, kb + h)),
                  pl.BlockSpec((s, bw), lambda h, i: (0, vb + h)),
                  pl.BlockSpec((t, bw), lambda h, i: (0, kb + h)),
                  pl.BlockSpec((t, bw), lambda h, i: (0, vb + h))],
        out_specs=pl.BlockSpec((tq, bw), lambda h, i: (i, h)),
        out_shape=jax.ShapeDtypeStruct((s, n_heads * HEAD_DIM), BF16),
        scratch_shapes=[pltpu.VMEM((2 * NA_KH - 1, GRID_W, 2 * GRID_W), F32),
                        pltpu.VMEM((NBR_HEADS_PER_STEP, 3, win, tq), F32)],
        compiler_params=_params(("parallel", "arbitrary")),
        name="nbr_attn",
    )(rpb.astype(F32).reshape(-1), p, p, p, pc, pc)


def _outproj_kernel(x_ref, ya_ref, yb_ref, wa_ref, wb_ref, g_ref, o_ref):
    acc = _dot(ya_ref[...], wa_ref[...]) + _dot(yb_ref[...], wb_ref[...])
    o_ref[...] = x_ref[...] + g_ref[...] * acc


def _outproj_call(x, ya, yb, w, layer, gate, tm, name):
    m, d = x.shape
    k = ya.shape[1]
    assert yb.shape[1] == k and w.shape[1] == 2 * k
    return pl.pallas_call(
        _outproj_kernel,
        grid=(m // tm,),
        in_specs=[pl.BlockSpec((tm, d), lambda i: (i, 0)),
                  pl.BlockSpec((tm, k), lambda i: (i, 0)),
                  pl.BlockSpec((tm, k), lambda i: (i, 0)),
                  pl.BlockSpec((None, k, d), lambda i: (layer, 0, 0)),
                  pl.BlockSpec((None, k, d), lambda i: (layer, 1, 0)),
                  pl.BlockSpec((1, d), lambda i: (0, 0))],
        out_specs=pl.BlockSpec((tm, d), lambda i: (i, 0)),
        out_shape=jax.ShapeDtypeStruct((m, d), F32),
        compiler_params=_params(("parallel",)),
        name=name,
    )(x, ya, yb, w, w, gate)


HALO = BF16_SUBLANES


def _ffn_kernel(xm_ref, xp_ref, xn_ref, gn_ref, sh_ref, sc_ref, gate_ref, wa_ref, wg_ref,
                cwa_ref, cwg_ref, cba_ref, cbg_ref, wd_ref, o_ref,
                h_scr, acc_scr, ua0_scr, ua1_scr, ug0_scr, ug1_scr):
    i, j = pl.program_id(0), pl.program_id(1)
    ni, nj = pl.num_programs(0), pl.num_programs(1)
    tm = xm_ref.shape[0]

    @pl.when(j == 0)
    def _():
        nm = lambda x: _norm_mod(x, gn_ref[...], sh_ref[...], sc_ref[...])
        h_scr[0:HALO, :] = jnp.where(i > 0, nm(xp_ref[...]), 0.0).astype(BF16)
        h_scr[HALO:HALO + tm, :] = nm(xm_ref[...]).astype(BF16)
        h_scr[HALO + tm:, :] = jnp.where(i < ni - 1, nm(xn_ref[...]), 0.0).astype(BF16)
        acc_scr[...] = jnp.zeros_like(acc_scr)

    def conv(u_scr, cw_ref, cb_ref, sl):
        out = cb_ref[:, sl] + u_scr[pl.ds(HALO - 1, tm), :] * cw_ref[0:1, sl]
        for t in range(1, CONV_W):
            out = out + u_scr[pl.ds(HALO - 1 + t, tm), :] * cw_ref[t:t + 1, sl]
        return out

    def silu_conv(u_scr, sl):
        g = conv(u_scr, cwg_ref, cbg_ref, sl)
        return g * (1.0 / (1.0 + jnp.exp(-g)))

    h = h_scr[...]
    half = wa_ref.shape[1] // 2
    lo, hi = slice(0, half), slice(half, 2 * half)
    ug0_scr[...] = _dot(h, wg_ref[:, lo])
    ug1_scr[...] = _dot(h, wg_ref[:, hi])
    sg0 = silu_conv(ug0_scr, lo)
    ua0_scr[...] = _dot(h, wa_ref[:, lo])
    sg1 = silu_conv(ug1_scr, hi)
    ua1_scr[...] = _dot(h, wa_ref[:, hi])
    z0 = sg0 * conv(ua0_scr, cwa_ref, cba_ref, lo)
    acc_scr[...] += _dot(z0.astype(BF16), wd_ref[lo, :])
    z1 = sg1 * conv(ua1_scr, cwa_ref, cba_ref, hi)
    acc_scr[...] += _dot(z1.astype(BF16), wd_ref[hi, :])

    @pl.when(j == nj - 1)
    def _():
        o_ref[...] = xm_ref[...] + gate_ref[...] * acc_scr[...]


def _ffn_call(x, gn, sh, sc, gate, w_up, conv_w, conv_b, w_down, layer, tm, tf, name):
    m, d = x.shape
    f = w_down.shape[1]
    nfb = f // tf
    hb = tm // HALO
    last = m // HALO - 1
    row = lambda i, j: (0, 0)
    return pl.pallas_call(
        _ffn_kernel,
        grid=(m // tm, nfb),
        in_specs=[pl.BlockSpec((tm, d), lambda i, j: (i, 0)),
                  pl.BlockSpec((HALO, d), lambda i, j: (jnp.maximum(i * hb - 1, 0), 0)),
                  pl.BlockSpec((HALO, d), lambda i, j: (jnp.minimum((i + 1) * hb, last), 0)),
                  pl.BlockSpec((1, d), row), pl.BlockSpec((1, d), row), pl.BlockSpec((1, d), row),
                  pl.BlockSpec((1, d), row),
                  pl.BlockSpec((None, d, tf), lambda i, j: (layer, 0, j)),
                  pl.BlockSpec((None, d, tf), lambda i, j: (layer, 0, nfb + j)),
                  pl.BlockSpec((None, CONV_W, tf), lambda i, j: (layer, 0, j)),
                  pl.BlockSpec((None, CONV_W, tf), lambda i, j: (layer, 0, nfb + j)),
                  pl.BlockSpec((None, 1, tf), lambda i, j: (layer, 0, j)),
                  pl.BlockSpec((None, 1, tf), lambda i, j: (layer, 0, nfb + j)),
                  pl.BlockSpec((None, tf, d), lambda i, j: (layer, j, 0))],
        out_specs=pl.BlockSpec((tm, d), lambda i, j: (i, 0)),
        out_shape=jax.ShapeDtypeStruct((m, d), F32),
        scratch_shapes=[pltpu.VMEM((tm + 2 * HALO, d), BF16),
                        pltpu.VMEM((tm, d), F32)] + [pltpu.VMEM((tm + 2 * HALO, tf // 2), F32)] * 4,
        compiler_params=_params(("parallel", "arbitrary")),
        name=name,
    )(x, x, x, gn, sh, sc, gate, w_up, w_up, conv_w, conv_w, conv_b, conv_b, w_down)


def _offsets(sizes):
    offs, o = [], 0
    for s in sizes:
        offs.append(o)
        o += s
    return offs


def _chunk_classes(sizes, classes):
    out = []
    for s, c in zip(sizes, classes):
        out += [c] * (s // EPI_CHUNK)
    return out


def _gain_vector(sizes, gains):
    parts = []
    for s, g in zip(sizes, gains):
        parts.append(jnp.ones((s,), F32) if g is None else jnp.tile(g.astype(F32), s // g.shape[0]))
    return jnp.concatenate(parts).reshape(1, -1)


def _lambda_init(layer_idx):
    return 0.8 - 0.6 * math.exp(-0.3 * layer_idx)


def _forward(x, c, ctx, c_ctx, w_ada, b_ada, norm1_g, w_in, w_out, a_qk_g, a_lambda, a_subln_g,
             b_qk_g, c_qk_g, c_sink, d_qk_g, d_rpb, norm2_g, w_up, conv_w, conv_b, w_down):
    s, d = x.shape
    t = ctx.shape[0]
    depth = w_ada.shape[0]
    n_heads = d // (2 * HEAD_DIM)
    n_kv = n_heads // GQA_GROUP
    hw = n_heads * HEAD_DIM
    kvw = n_kv * HEAD_DIM
    even_sizes = (hw, hw, hw, hw, kvw, kvw)
    odd_sizes = (hw, kvw, kvw, hw, hw, hw)
    tm = 512
    tf = 512

    mod = _mod_call(c, c_ctx, w_ada, b_ada)
    rope_a = _rope_tables(s, 16)
    rope_b = _rope_tables(s, 32)
    tabs = rope_a + rope_b
    scale_a = A_SUB ** -0.5
    scale = HEAD_DIM ** -0.5

    w_in_b, w_out_b, w_up_b, w_down_b = (w.astype(BF16) for w in (w_in, w_out, w_up, w_down))
    conv_b3 = conv_b.reshape(depth, 1, -1)

    xc = ctx
    for i in range(depth):
        need_ctx = i < depth - 1
        e = i // 2
        vec = lambda r, k: mod[i, r:r + 1, k * d:(k + 1) * d]
        g1n = norm1_g[i].reshape(1, d)
        g2n = norm2_g[i].reshape(1, d)
        if i % 2 == 0:
            sizes = even_sizes
            classes = [(A_SUB, "A"), (A_SUB, "A"), (0, None), (HEAD_DIM, "B"), (HEAD_DIM, "B"), (0, None)]
            gains = [a_qk_g[e, 0] * (scale_a * LOG2E), a_qk_g[e, 1], None,
                     b_qk_g[e, 0] * (scale * LOG2E), b_qk_g[e, 1], None]
        else:
            sizes = odd_sizes
            classes = [(HEAD_DIM, "B"), (HEAD_DIM, "B"), (0, None), (HEAD_DIM, None), (HEAD_DIM, None), (0, None)]
            gains = [c_qk_g[e, 0] * (scale * LOG2E), c_qk_g[e, 1], None,
                     d_qk_g[e, 0] * (scale * LOG2E), d_qk_g[e, 1], None]
        offs = _offsets(sizes)
        chunk_cls = _chunk_classes(sizes, classes)
        gain = _gain_vector(sizes, gains)

        p = _inproj_call(x, g1n, vec(0, 0), vec(0, 1), w_in_b, i, gain, chunk_cls, tabs, tm, f"inproj{i}")
        pc = _inproj_call(xc, g1n, vec(1, 0), vec(1, 1), w_in_b, i, gain, chunk_cls, None, t, f"inproj_ctx{i}")

        if i % 2 == 0:
            lam_init = _lambda_init(i)
            fa = functools.partial(_flash_call, "diff", offs=offs[0:3], n_heads=n_heads, lam_p=a_lambda[e],
                                   subln=a_subln_g[e], lam_init=lam_init)
            fb = functools.partial(_flash_call, "gqa", offs=offs[3:6], n_heads=n_kv, lam_p=None,
                                   subln=None, lam_init=None)
            bound_a = (A_SUB * LOGIT_BOUND_SLACK) * jnp.max(jnp.abs(gains[0])) * jnp.max(jnp.abs(gains[1]))
            bound_b = (HEAD_DIM * LOGIT_BOUND_SLACK) * jnp.max(jnp.abs(gains[3])) * jnp.max(jnp.abs(gains[4]))
            ya = lax.cond(bound_a <= MAX_LOGIT_BOUND,
                          lambda: fa(p, p, pc, tq=512, name=f"diff_attn{i}", bound=bound_a.reshape(1)),
                          lambda: fa(p, p, pc, tq=512, name=f"diff_attn_online{i}"))
            yb = lax.cond(bound_b <= MAX_LOGIT_BOUND,
                          lambda: fb(p, p, pc, tq=256, name=f"gqa_attn{i}", bound=bound_b.reshape(1)),
                          lambda: fb(p, p, pc, tq=256, name=f"gqa_attn_online{i}"))
            if need_ctx:
                ya_c = fa(pc, None, pc, tq=t, name=f"diff_attn_ctx{i}")
                yb_c = fb(pc, None, pc, tq=t, name=f"gqa_attn_ctx{i}")
        else:
            ya = _win_call(p, pc, offs[0:3], n_kv, c_sink[e], 256)
            yb = _nbr_call(p, pc, offs[3:6], n_heads, d_rpb[e])
            if need_ctx:
                raise NotImplementedError("context stream after an odd layer")

        x = _outproj_call(x, ya, yb, w_out_b, i, vec(0, 2), tm, f"outproj{i}")
        x = _ffn_call(x, g2n, vec(0, 3), vec(0, 4), vec(0, 5), w_up_b, conv_w, conv_b3, w_down_b, i, tm, tf,
                      f"ffn{i}")
        if need_ctx:
            xc = _outproj_call(xc, ya_c, yb_c, w_out_b, i, vec(1, 2), t, f"outproj_ctx{i}")
            xc = _ffn_call(xc, g2n, vec(1, 3), vec(1, 4), vec(1, 5), w_up_b, conv_w, conv_b3, w_down_b, i, t, tf,
                           f"ffn_ctx{i}")
    return x


def kernel(x, c, ctx, c_ctx, w_ada, b_ada, norm1_g, w_in, w_out, a_qk_g, a_lambda, a_subln_g, b_qk_g, c_qk_g,
           c_sink, d_qk_g, d_rpb, norm2_g, w_up, conv_w, conv_b, w_down):
    outs = [_forward(x[b], c[b], ctx[b], c_ctx, w_ada, b_ada, norm1_g, w_in, w_out, a_qk_g, a_lambda, a_subln_g,
                     b_qk_g, c_qk_g, c_sink, d_qk_g, d_rpb, norm2_g, w_up, conv_w, conv_b, w_down)
            for b in range(x.shape[0])]
    return jnp.stack(outs)
```

```python
import functools
import math

import jax
import jax.numpy as jnp
from jax import lax
from jax.experimental import pallas as pl
from jax.experimental.pallas import tpu as pltpu

GRID_W = 64
HEAD_DIM = 128
A_SUB = HEAD_DIM // 2
ROPE_THETA = 10000.0
EPS = 1e-6
NEG_INF = -1e30
N_MOD = 6
C_WINDOW = 128
NA_KH = 8
NA_KW = 16
CONV_W = 3
GQA_GROUP = 4
LOG2E = math.log2(math.e)
LOGIT_BOUND_SLACK = 1.02
MAX_LOGIT_BOUND = 50.0

LANES = 128
BF16_SUBLANES = 16
VMEM_LIMIT = 56 * 1024 * 1024

F32 = jnp.float32
BF16 = jnp.bfloat16


def _params(sem, vmem=VMEM_LIMIT):
    return pltpu.CompilerParams(dimension_semantics=sem, vmem_limit_bytes=vmem)


def _dot(a, b):
    return jnp.dot(a, b, preferred_element_type=F32)


def _dot_nt(a, b):
    return lax.dot_general(a, b, (((1,), (1,)), ((), ())), preferred_element_type=F32)


def _dot_tn(a, b):
    return lax.dot_general(a, b, (((0,), (0,)), ((), ())), preferred_element_type=F32)


def _norm_mod(x, g, sh, sc):
    inv = lax.rsqrt(jnp.mean(x * x, axis=-1, keepdims=True) + EPS)
    return (x * inv * g) * (1.0 + sc) + sh


def _mod_kernel(cb_ref, w_ref, b_ref, o_ref):
    tn = w_ref.shape[2]
    for r in range(2):
        cv = cb_ref[r]
        a = cv * (1.0 / (1.0 + jnp.exp(-cv)))
        for ch in range(tn // LANES):
            sl = slice(ch * LANES, (ch + 1) * LANES)
            o_ref[0, r:r + 1, sl] = jnp.sum(w_ref[0, :, sl] * a, axis=0, keepdims=True) + b_ref[0, :, sl]


def _mod_call(c, c_ctx, w_ada, b_ada):
    depth, d, n = w_ada.shape
    tn = 1024
    cb = jnp.broadcast_to(jnp.stack([c, c_ctx])[:, :, None], (2, d, LANES))
    return pl.pallas_call(
        _mod_kernel,
        grid=(depth, n // tn),
        in_specs=[pl.BlockSpec((2, d, LANES), lambda i, j: (0, 0, 0)),
                  pl.BlockSpec((1, d, tn), lambda i, j: (i, 0, j)),
                  pl.BlockSpec((1, 1, tn), lambda i, j: (i, 0, j))],
        out_specs=pl.BlockSpec((1, 2, tn), lambda i, j: (i, 0, j)),
        out_shape=jax.ShapeDtypeStruct((depth, 2, n), F32),
        compiler_params=_params(("parallel", "parallel")),
        name="adaln_mod",
    )(cb, w_ada, b_ada.reshape(depth, 1, n))


EPI_CHUNK = 256


def _head_norm(p, gain, width):
    shift = int(math.log2(width))
    r = lax.broadcasted_iota(jnp.int32, (EPI_CHUNK, EPI_CHUNK), 0) >> shift
    c = lax.broadcasted_iota(jnp.int32, (EPI_CHUNK, EPI_CHUNK), 1) >> shift
    bd = jnp.where(r == c, 1.0, 0.0).astype(BF16)
    seg = _dot((p * p).astype(BF16), bd)
    return p * lax.rsqrt(seg * (1.0 / width) + EPS) * gain


def _rope(y, half, tabs):
    c_ref, s1_ref, s2_ref = tabs
    c, s1, s2 = c_ref[...], s1_ref[...], s2_ref[...]
    outs = []
    for k in range(y.shape[1] // LANES):
        yk = y[:, k * LANES:(k + 1) * LANES]
        outs.append(yk * c + pltpu.roll(yk, LANES - half, 1) * s1 + pltpu.roll(yk, half, 1) * s2)
    return jnp.concatenate(outs, axis=1)


def _inproj_kernel(*refs, classes, rope, tn):
    x_ref, g_ref, sh_ref, sc_ref, w_ref, gain_ref = refs[:6]
    tabs = {"A": refs[6:9], "B": refs[9:12]} if rope else None
    o_ref = refs[-1]
    h = _norm_mod(x_ref[...], g_ref[...], sh_ref[...], sc_ref[...]).astype(BF16)
    per = tn // EPI_CHUNK
    for t in range(len(classes) // per):
        p = _dot(h, w_ref[:, t * tn:(t + 1) * tn])
        for ci, (width, kind) in enumerate(classes[t * per:(t + 1) * per]):
            sl = slice(t * tn + ci * EPI_CHUNK, t * tn + (ci + 1) * EPI_CHUNK)
            y = p[:, ci * EPI_CHUNK:(ci + 1) * EPI_CHUNK]
            if width:
                y = _head_norm(y, gain_ref[:, sl], width)
            if kind and rope:
                y = _rope(y, 16 if kind == "A" else 32, tabs[kind])
            o_ref[:, sl] = y.astype(BF16)


def _inproj_call(x, g, sh, sc, w, layer, gain, classes, tabs, tm, name):
    m, d = x.shape
    n = w.shape[2]
    rope = tabs is not None
    const = lambda i: (0, 0)
    in_specs = [pl.BlockSpec((tm, d), lambda i: (i, 0)),
                pl.BlockSpec((1, d), const), pl.BlockSpec((1, d), const), pl.BlockSpec((1, d), const),
                pl.BlockSpec((None, d, n), lambda i: (layer, 0, 0), pipeline_mode=pl.Buffered(1)),
                pl.BlockSpec((1, n), const)]
    args = [x, g, sh, sc, w, gain]
    if rope:
        in_specs += [pl.BlockSpec((tm, LANES), lambda i: (i, 0))] * 6
        args += list(tabs)
    return pl.pallas_call(
        functools.partial(_inproj_kernel, classes=tuple(classes), rope=rope, tn=2 * EPI_CHUNK),
        grid=(m // tm,),
        in_specs=in_specs,
        out_specs=pl.BlockSpec((tm, n), lambda i: (i, 0)),
        out_shape=jax.ShapeDtypeStruct((m, n), BF16),
        compiler_params=_params(("parallel",)),
        name=name,
    )(*args)


def _rope_tables(s, half):
    n_rows = s // GRID_W
    freqs = ROPE_THETA ** (-jnp.arange(half, dtype=F32) / half)
    zeros = jnp.zeros((s, half), F32)
    cs, s1s, s2s = [], [], []
    for count, expand in ((n_rows, lambda a: jnp.repeat(a, GRID_W, axis=0)),
                          (GRID_W, lambda a: jnp.tile(a, (n_rows, 1)))):
        ang = jnp.arange(count, dtype=jnp.int32).astype(F32)[:, None] * freqs[None, :]
        cos, sin = expand(jnp.cos(ang)), expand(jnp.sin(ang))
        cs += [cos, cos]
        s1s += [-sin, zeros]
        s2s += [zeros, sin]
    reps = LANES // (4 * half)
    return tuple(jnp.tile(jnp.concatenate(t, axis=1), (1, reps)) for t in (cs, s1s, s2s))


def _flash_kernel(*refs, mode, tq, tk, n_lat, lam_init, bounded, n_cast):
    if bounded:
        bound = refs[0][0]
        refs = refs[1:]
    q_ref, kc_ref, vc_ref = refs[:3]
    pos = 3
    if n_lat:
        k_ref, v_ref = refs[3:5]
        pos = 5
    if mode == "diff":
        lam_ref, sg_ref = refs[pos:pos + 2]
        pos += 2
    o_ref = refs[pos + n_cast]
    for src_ref, dst_ref in zip(refs[pos:pos + n_cast], refs[pos + n_cast + 1:pos + 2 * n_cast + 1]):
        dst_ref[...] = src_ref[...].astype(BF16)

    if mode == "diff":
        q = q_ref[...].astype(F32)
        lane = lax.broadcasted_iota(jnp.int32, q.shape, 1)
        qs = jnp.concatenate([jnp.where(lane < A_SUB, q, 0.0), jnp.where(lane >= A_SUB, q, 0.0)],
                             axis=0).astype(BF16)
    else:
        qs = jnp.concatenate([q_ref[:, g * HEAD_DIM:(g + 1) * HEAD_DIM] for g in range(GQA_GROUP)], axis=0)
    rows = qs.shape[0]

    def pv_dot(v, p):
        return lax.dot_general(v, p.astype(BF16), (((0,), (0,)), ((), ())), preferred_element_type=F32)

    if bounded:
        def step(k, v, carry):
            _, l, acc = carry
            p = jnp.exp2(_dot_nt(k, qs) - bound)
            return carry[0], l + jnp.sum(p, axis=0, keepdims=True), acc + pv_dot(v, p)
    else:
        def step(k, v, carry):
            m, l, acc = carry
            s = _dot_nt(k, qs)
            m_new = jnp.maximum(m, jnp.max(s, axis=0, keepdims=True))
            p = jnp.exp2(s - m_new)
            alpha = jnp.exp2(m - m_new)
            return m_new, alpha * l + jnp.sum(p, axis=0, keepdims=True), alpha * acc + pv_dot(v, p)

    carry = (jnp.full((1, rows), NEG_INF, F32), jnp.zeros((1, rows), F32), jnp.zeros((HEAD_DIM, rows), F32))
    carry = step(kc_ref[...], vc_ref[...], carry)
    if n_lat:
        def body(c, carry):
            off = pl.multiple_of(c * tk, tk)
            return step(k_ref[pl.ds(off, tk), :], v_ref[pl.ds(off, tk), :], carry)
        carry = lax.fori_loop(0, n_lat, body, carry, unroll=8)
    _, l, acc = carry
    o = (acc * (1.0 / l)).T
    if mode == "diff":
        lp = lam_ref[...]
        lam = (jnp.exp(jnp.sum(lp[0:1] * lp[1:2], axis=1, keepdims=True))
               - jnp.exp(jnp.sum(lp[2:3] * lp[3:4], axis=1, keepdims=True)) + lam_init)
        dlt = o[:tq] - lam * o[tq:]
        y = dlt * lax.rsqrt(jnp.mean(dlt * dlt, axis=-1, keepdims=True) + EPS) * sg_ref[...]
        o_ref[...] = (y * (1.0 - lam_init)).astype(BF16)
    else:
        for g in range(GQA_GROUP):
            o_ref[:, g * HEAD_DIM:(g + 1) * HEAD_DIM] = o[g * tq:(g + 1) * tq].astype(BF16)


def _flash_call(mode, qsrc, ksrc, pc, offs, n_heads, lam_p, subln, lam_init, tq, name, bound=None,
                cast_slabs=()):
    sq = qsrc.shape[0]
    t = pc.shape[0]
    qo, ko, vo = offs
    tk = 1024
    qw = HEAD_DIM if mode == "diff" else GQA_GROUP * HEAD_DIM
    qb, kb, vb = qo // qw, ko // HEAD_DIM, vo // HEAD_DIM
    in_specs = [pl.BlockSpec((tq, qw), lambda h, i: (i, qb + h)),
                pl.BlockSpec((t, HEAD_DIM), lambda h, i: (0, kb + h)),
                pl.BlockSpec((t, HEAD_DIM), lambda h, i: (0, vb + h))]
    args = [qsrc, pc, pc]
    if bound is not None:
        in_specs = [pl.BlockSpec(memory_space=pltpu.SMEM)] + in_specs
        args = [bound] + args
    n_lat = 0
    if ksrc is not None:
        s = ksrc.shape[0]
        tk = min(tk, s)
        assert s % tk == 0
        n_lat = s // tk
        in_specs += [pl.BlockSpec((s, HEAD_DIM), lambda h, i: (0, kb + h)),
                     pl.BlockSpec((s, HEAD_DIM), lambda h, i: (0, vb + h))]
        args += [ksrc, ksrc]
    if mode == "diff":
        in_specs += [pl.BlockSpec(lam_p.shape, lambda h, i: (0, 0)),
                     pl.BlockSpec((1, HEAD_DIM), lambda h, i: (0, 0))]
        args += [lam_p, subln.reshape(1, HEAD_DIM)]
    nq = sq // tq
    slab = lambda a: pl.BlockSpec((None,) + a.shape[1:], lambda h, i: (h * nq + i, 0, 0))
    in_specs += [slab(a) for a in cast_slabs]
    args += list(cast_slabs)
    outs = pl.pallas_call(
        functools.partial(_flash_kernel, mode=mode, tq=tq, tk=tk, n_lat=n_lat, lam_init=lam_init,
                          bounded=bound is not None, n_cast=len(cast_slabs)),
        grid=(n_heads, nq),
        in_specs=in_specs,
        out_specs=[pl.BlockSpec((tq, qw), lambda h, i: (i, h))] + [slab(a) for a in cast_slabs],
        out_shape=[jax.ShapeDtypeStruct((sq, n_heads * qw), BF16)]
                  + [jax.ShapeDtypeStruct(a.shape, BF16) for a in cast_slabs],
        compiler_params=_params(("parallel", "parallel")),
        name=name,
    )(*args)
    return tuple(outs) if cast_slabs else outs[0]


def _cast_slab_shape(n_elems, n_steps):
    for lanes in (8 * LANES, 4 * LANES, 2 * LANES, LANES):
        rows, rem = divmod(n_elems, n_steps * lanes)
        if rem == 0 and rows and rows % BF16_SUBLANES == 0:
            return (n_steps, rows, lanes)
    return None


WIN_PAIR = 2


def _win_kernel(sink_ref, q_ref, k_ref, v_ref, kc_ref, vc_ref, o_ref, *, tq, s_len):
    kvh = pl.program_id(0)
    q0 = pl.program_id(1) * tq
    band = tq + 2 * C_WINDOW
    pair = WIN_PAIR
    rows = pair * tq
    start = pl.multiple_of(jnp.clip(q0 - C_WINDOW, 0, s_len - band), C_WINDOW)
    kb = k_ref[pl.ds(start, band), :]
    vb = v_ref[pl.ds(start, band), :]
    lane = lax.broadcasted_iota(jnp.int32, (band, rows), 1)
    qpos = q0 + (lane & (tq - 1))
    kpos = start + lax.broadcasted_iota(jnp.int32, (band, rows), 0)
    valid = jnp.abs(qpos - kpos) <= C_WINDOW
    first = lax.broadcasted_iota(jnp.int32, (1, rows), 1) < tq
    passes = []
    for g0 in range(0, GQA_GROUP, pair):
        qs = jnp.concatenate([q_ref[:, g * HEAD_DIM:(g + 1) * HEAD_DIM] for g in range(g0, g0 + pair)], axis=0)
        passes.append((_dot_nt(kb, qs), _dot_nt(kc_ref[...], qs)))
    for g0, (s_lat, s_ctx) in zip(range(0, GQA_GROUP, pair), passes):
        s_lat = jnp.where(valid, s_lat, NEG_INF)
        sink = jnp.where(first, sink_ref[kvh * GQA_GROUP + g0], sink_ref[kvh * GQA_GROUP + g0 + 1]) * LOG2E
        m = jnp.maximum(jnp.maximum(jnp.max(s_lat, axis=0, keepdims=True),
                                    jnp.max(s_ctx, axis=0, keepdims=True)), sink)
        p_lat = jnp.exp2(s_lat - m)
        p_ctx = jnp.exp2(s_ctx - m)
        l = jnp.sum(p_lat, axis=0, keepdims=True) + jnp.sum(p_ctx, axis=0, keepdims=True) + jnp.exp2(sink - m)
        o = ((_dot_tn(vc_ref[...], p_ctx.astype(BF16)) + _dot_tn(vb, p_lat.astype(BF16))) * (1.0 / l)).T
        for g in range(pair):
            o_ref[:, (g0 + g) * HEAD_DIM:(g0 + g + 1) * HEAD_DIM] = o[g * tq:(g + 1) * tq].astype(BF16)


def _win_call(p, pc, offs, n_kv, sink, tq):
    s = p.shape[0]
    t = pc.shape[0]
    qo, ko, vo = offs
    qw = GQA_GROUP * HEAD_DIM
    qb, kb, vb = qo // qw, ko // HEAD_DIM, vo // HEAD_DIM
    assert s >= tq + 2 * C_WINDOW and tq & (tq - 1) == 0
    return pl.pallas_call(
        functools.partial(_win_kernel, tq=tq, s_len=s),
        grid=(n_kv, s // tq),
        in_specs=[pl.BlockSpec(memory_space=pltpu.SMEM),
                  pl.BlockSpec((tq, qw), lambda h, i: (i, qb + h)),
                  pl.BlockSpec((s, HEAD_DIM), lambda h, i: (0, kb + h)),
                  pl.BlockSpec((s, HEAD_DIM), lambda h, i: (0, vb + h)),
                  pl.BlockSpec((t, HEAD_DIM), lambda h, i: (0, kb + h)),
                  pl.BlockSpec((t, HEAD_DIM), lambda h, i: (0, vb + h))],
        out_specs=pl.BlockSpec((tq, qw), lambda h, i: (i, h)),
        out_shape=jax.ShapeDtypeStruct((s, n_kv * qw), BF16),
        compiler_params=_params(("parallel", "parallel")),
        name="window_attn",
    )(sink, p, p, p, pc, pc)


NBR_Q_ROWS = 4
NBR_WIN_ROWS = NBR_Q_ROWS + NA_KH


def _nbr_build_bias(rpb_ref, head, mats_scr, tab_scr, n_rows):
    na, nb = 2 * NA_KH - 1, 2 * NA_KW - 1
    shape = (GRID_W, 2 * GRID_W)
    wk = lax.broadcasted_iota(jnp.int32, shape, 0)
    lane = lax.broadcasted_iota(jnp.int32, shape, 1)
    wq = lane & (GRID_W - 1)
    dc = jnp.clip(wk - wq, -(NA_KW - 1), NA_KW - 1) + (NA_KW - 1)
    c_start = jnp.clip(wq - NA_KW // 2, 0, GRID_W - NA_KW)
    col_ok = jnp.logical_and(wk >= c_start, wk < c_start + NA_KW)

    def build(a, carry):
        base = (head * na + a) * nb
        mat = jnp.zeros(shape, F32)
        for b in range(nb):
            mat = jnp.where(dc == b, rpb_ref[base + b] * LOG2E, mat)
        mats_scr[a] = jnp.where(col_ok, mat, NEG_INF)
        return carry

    lax.fori_loop(0, na, build, 0)
    neg = jnp.full(shape, NEG_INF, F32)
    first_half = lane < GRID_W
    for cl, r0 in enumerate((0, NBR_Q_ROWS, n_rows - NBR_Q_ROWS)):
        ws = min(max(r0 - NA_KH // 2, 0), n_rows - NBR_WIN_ROWS)
        for kr in range(NBR_WIN_ROWS):
            for qp in range(NBR_Q_ROWS // 2):
                halves = []
                for r in (r0 + 2 * qp, r0 + 2 * qp + 1):
                    r_start = min(max(r - NA_KH // 2, 0), n_rows - NA_KH)
                    inside = r_start <= ws + kr < r_start + NA_KH
                    halves.append(mats_scr[ws + kr - r + NA_KH - 1] if inside else neg)
                tab_scr[cl, kr * GRID_W:(kr + 1) * GRID_W, qp * 2 * GRID_W:(qp + 1) * 2 * GRID_W] = (
                    jnp.where(first_half, halves[0], halves[1]))


NBR_HEADS_PER_STEP = 4


def _nbr_kernel(rpb_ref, q_ref, k_ref, v_ref, kc_ref, vc_ref, o_ref, mats_scr, tab_scr, *, tq, win, s_len):
    i = pl.program_id(1)
    nq = pl.num_programs(1)

    @pl.when(i == 0)
    def _():
        for hh in range(NBR_HEADS_PER_STEP):
            _nbr_build_bias(rpb_ref, pl.program_id(0) * NBR_HEADS_PER_STEP + hh, mats_scr, tab_scr.at[hh],
                            s_len // GRID_W)

    q0 = i * tq
    lead = (NA_KH // 2) * GRID_W
    start = pl.multiple_of(jnp.clip(q0 - lead, 0, s_len - win), lead)
    cls = jnp.where(i == 0, 0, jnp.where(i == nq - 1, 2, 1))
    heads = [slice(hh * HEAD_DIM, (hh + 1) * HEAD_DIM) for hh in range(NBR_HEADS_PER_STEP)]

    def scores(sl):
        return _dot_nt(k_ref[pl.ds(start, win), sl], q_ref[:, sl]), _dot_nt(kc_ref[:, sl], q_ref[:, sl])

    ahead = 2
    pending = [scores(sl) for sl in heads[:ahead]]
    for hh, sl in enumerate(heads):
        s_win, s_ctx = pending.pop(0)
        s_win = s_win + tab_scr[hh, cls]
        m = jnp.maximum(jnp.max(s_win, axis=0, keepdims=True), jnp.max(s_ctx, axis=0, keepdims=True))
        p_win = jnp.exp2(s_win - m)
        p_ctx = jnp.exp2(s_ctx - m)
        l = jnp.sum(p_win, axis=0, keepdims=True) + jnp.sum(p_ctx, axis=0, keepdims=True)
        o = (_dot_tn(vc_ref[:, sl], p_ctx.astype(BF16))
             + _dot_tn(v_ref[pl.ds(start, win), sl], p_win.astype(BF16)))
        o_ref[:, sl] = ((o * (1.0 / l)).T).astype(BF16)
        if hh + ahead < len(heads):
            pending.append(scores(heads[hh + ahead]))


def _nbr_call(p, pc, offs, n_heads, rpb):
    s = p.shape[0]
    t = pc.shape[0]
    tq = NBR_Q_ROWS * GRID_W
    win = NBR_WIN_ROWS * GRID_W
    bw = NBR_HEADS_PER_STEP * HEAD_DIM
    qb, kb, vb = (o // bw for o in offs)
    assert s // GRID_W >= NBR_WIN_ROWS and rpb.shape[1:] == (2 * NA_KH - 1, 2 * NA_KW - 1)
    assert n_heads % NBR_HEADS_PER_STEP == 0 and all(o % bw == 0 for o in offs)
    return pl.pallas_call(
        functools.partial(_nbr_kernel, tq=tq, win=win, s_len=s),
        grid=(n_heads // NBR_HEADS_PER_STEP, s // tq),
        in_specs=[pl.BlockSpec(memory_space=pltpu.SMEM),
                  pl.BlockSpec((tq, bw), lambda h, i: (i, qb + h)),
                  pl.BlockSpec((s, bw), lambda h, i: (0, kb + h)),
                  pl.BlockSpec((s, bw), lambda h, i: (0, vb + h)),
                  pl.BlockSpec((t, bw), lambda h, i: (0, kb + h)),
                  pl.BlockSpec((t, bw), lambda h, i: (0, vb + h))],
        out_specs=pl.BlockSpec((tq, bw), lambda h, i: (i, h)),
        out_shape=jax.ShapeDtypeStruct((s, n_heads * HEAD_DIM), BF16),
        scratch_shapes=[pltpu.VMEM((2 * NA_KH - 1, GRID_W, 2 * GRID_W), F32),
                        pltpu.VMEM((NBR_HEADS_PER_STEP, 3, win, tq), F32)],
        compiler_params=_params(("parallel", "arbitrary")),
        name="nbr_attn",
    )(rpb.astype(F32).reshape(-1), p, p, p, pc, pc)


def _outproj_kernel(x_ref, ya_ref, yb_ref, wa_ref, wb_ref, g_ref, o_ref):
    acc = _dot(ya_ref[...], wa_ref[...]) + _dot(yb_ref[...], wb_ref[...])
    o_ref[...] = x_ref[...] + g_ref[...] * acc


def _outproj_call(x, ya, yb, w, layer, gate, tm, name):
    m, d = x.shape
    k = ya.shape[1]
    assert yb.shape[1] == k and w.shape[1] == 2 * k
    return pl.pallas_call(
        _outproj_kernel,
        grid=(m // tm,),
        in_specs=[pl.BlockSpec((tm, d), lambda i: (i, 0)),
                  pl.BlockSpec((tm, k), lambda i: (i, 0)),
                  pl.BlockSpec((tm, k), lambda i: (i, 0)),
                  pl.BlockSpec((None, k, d), lambda i: (layer, 0, 0)),
                  pl.BlockSpec((None, k, d), lambda i: (layer, 1, 0)),
                  pl.BlockSpec((1, d), lambda i: (0, 0))],
        out_specs=pl.BlockSpec((tm, d), lambda i: (i, 0)),
        out_shape=jax.ShapeDtypeStruct((m, d), F32),
        compiler_params=_params(("parallel",)),
        name=name,
    )(x, ya, yb, w, w, gate)


HALO = BF16_SUBLANES


def _ffn_kernel(xm_ref, xp_ref, xn_ref, gn_ref, sh_ref, sc_ref, gate_ref, wa_ref, wg_ref,
                cwa_ref, cwg_ref, cba_ref, cbg_ref, wd_ref, o_ref,
                h_scr, acc_scr, ua0_scr, ua1_scr, ug0_scr, ug1_scr):
    i, j = pl.program_id(0), pl.program_id(1)
    ni, nj = pl.num_programs(0), pl.num_programs(1)
    tm = xm_ref.shape[0]

    @pl.when(j == 0)
    def _():
        nm = lambda x: _norm_mod(x, gn_ref[...], sh_ref[...], sc_ref[...])
        h_scr[0:HALO, :] = jnp.where(i > 0, nm(xp_ref[...]), 0.0).astype(BF16)
        h_scr[HALO:HALO + tm, :] = nm(xm_ref[...]).astype(BF16)
        h_scr[HALO + tm:, :] = jnp.where(i < ni - 1, nm(xn_ref[...]), 0.0).astype(BF16)
        acc_scr[...] = jnp.zeros_like(acc_scr)

    def conv(u_scr, cw_ref, cb_ref, sl):
        out = cb_ref[:, sl] + u_scr[pl.ds(HALO - 1, tm), :] * cw_ref[0:1, sl]
        for t in range(1, CONV_W):
            out = out + u_scr[pl.ds(HALO - 1 + t, tm), :] * cw_ref[t:t + 1, sl]
        return out

    def silu_conv(u_scr, sl):
        g = conv(u_scr, cwg_ref, cbg_ref, sl)
        return g * (1.0 / (1.0 + jnp.exp(-g)))

    h = h_scr[...]
    half = wa_ref.shape[1] // 2
    lo, hi = slice(0, half), slice(half, 2 * half)
    ug0_scr[...] = _dot(h, wg_ref[:, lo])
    ug1_scr[...] = _dot(h, wg_ref[:, hi])
    sg0 = silu_conv(ug0_scr, lo)
    ua0_scr[...] = _dot(h, wa_ref[:, lo])
    sg1 = silu_conv(ug1_scr, hi)
    ua1_scr[...] = _dot(h, wa_ref[:, hi])
    z0 = sg0 * conv(ua0_scr, cwa_ref, cba_ref, lo)
    acc_scr[...] += _dot(z0.astype(BF16), wd_ref[lo, :])
    z1 = sg1 * conv(ua1_scr, cwa_ref, cba_ref, hi)
    acc_scr[...] += _dot(z1.astype(BF16), wd_ref[hi, :])

    @pl.when(j == nj - 1)
    def _():
        o_ref[...] = xm_ref[...] + gate_ref[...] * acc_scr[...]


def _ffn_call(x, gn, sh, sc, gate, w_up, conv_w, conv_b, w_down, layer, tm, tf, name):
    m, d = x.shape
    f = w_down.shape[1]
    nfb = f // tf
    hb = tm // HALO
    last = m // HALO - 1
    row = lambda i, j: (0, 0)
    return pl.pallas_call(
        _ffn_kernel,
        grid=(m // tm, nfb),
        in_specs=[pl.BlockSpec((tm, d), lambda i, j: (i, 0)),
                  pl.BlockSpec((HALO, d), lambda i, j: (jnp.maximum(i * hb - 1, 0), 0)),
                  pl.BlockSpec((HALO, d), lambda i, j: (jnp.minimum((i + 1) * hb, last), 0)),
                  pl.BlockSpec((1, d), row), pl.BlockSpec((1, d), row), pl.BlockSpec((1, d), row),
                  pl.BlockSpec((1, d), row),
                  pl.BlockSpec((None, d, tf), lambda i, j: (layer, 0, j)),
                  pl.BlockSpec((None, d, tf), lambda i, j: (layer, 0, nfb + j)),
                  pl.BlockSpec((None, CONV_W, tf), lambda i, j: (layer, 0, j)),
                  pl.BlockSpec((None, CONV_W, tf), lambda i, j: (layer, 0, nfb + j)),
                  pl.BlockSpec((None, 1, tf), lambda i, j: (layer, 0, j)),
                  pl.BlockSpec((None, 1, tf), lambda i, j: (layer, 0, nfb + j)),
                  pl.BlockSpec((None, tf, d), lambda i, j: (layer, j, 0))],
        out_specs=pl.BlockSpec((tm, d), lambda i, j: (i, 0)),
        out_shape=jax.ShapeDtypeStruct((m, d), F32),
        scratch_shapes=[pltpu.VMEM((tm + 2 * HALO, d), BF16),
                        pltpu.VMEM((tm, d), F32)] + [pltpu.VMEM((tm + 2 * HALO, tf // 2), F32)] * 4,
        compiler_params=_params(("parallel", "arbitrary")),
        name=name,
    )(x, x, x, gn, sh, sc, gate, w_up, w_up, conv_w, conv_w, conv_b, conv_b, w_down)


def _offsets(sizes):
    offs, o = [], 0
    for s in sizes:
        offs.append(o)
        o += s
    return offs


def _chunk_classes(sizes, classes):
    out = []
    for s, c in zip(sizes, classes):
        out += [c] * (s // EPI_CHUNK)
    return out


def _gain_vector(sizes, gains):
    parts = []
    for s, g in zip(sizes, gains):
        parts.append(jnp.ones((s,), F32) if g is None else jnp.tile(g.astype(F32), s // g.shape[0]))
    return jnp.concatenate(parts).reshape(1, -1)


def _lambda_init(layer_idx):
    return 0.8 - 0.6 * math.exp(-0.3 * layer_idx)


def _forward(x, c, ctx, c_ctx, w_ada, b_ada, norm1_g, w_in, w_out, a_qk_g, a_lambda, a_subln_g,
             b_qk_g, c_qk_g, c_sink, d_qk_g, d_rpb, norm2_g, w_up, conv_w, conv_b, w_down):
    s, d = x.shape
    t = ctx.shape[0]
    depth = w_ada.shape[0]
    n_heads = d // (2 * HEAD_DIM)
    n_kv = n_heads // GQA_GROUP
    hw = n_heads * HEAD_DIM
    kvw = n_kv * HEAD_DIM
    even_sizes = (hw, hw, hw, hw, kvw, kvw)
    odd_sizes = (hw, kvw, kvw, hw, hw, hw)
    tm = 512
    tf = 512

    mod = _mod_call(c, c_ctx, w_ada, b_ada)
    rope_a = _rope_tables(s, 16)
    rope_b = _rope_tables(s, 32)
    tabs = rope_a + rope_b
    scale_a = A_SUB ** -0.5
    scale = HEAD_DIM ** -0.5

    w_in_first = w_in[:1].astype(BF16)
    late_f32 = {"w_in": w_in[1:], "w_out": w_out, "w_up": w_up, "w_down": w_down}
    late = None
    conv_b3 = conv_b.reshape(depth, 1, -1)

    xc = ctx
    for i in range(depth):
        need_ctx = i < depth - 1
        e = i // 2
        vec = lambda r, k: mod[i, r:r + 1, k * d:(k + 1) * d]
        g1n = norm1_g[i].reshape(1, d)
        g2n = norm2_g[i].reshape(1, d)
        if i % 2 == 0:
            sizes = even_sizes
            classes = [(A_SUB, "A"), (A_SUB, "A"), (0, None), (HEAD_DIM, "B"), (HEAD_DIM, "B"), (0, None)]
            gains = [a_qk_g[e, 0] * (scale_a * LOG2E), a_qk_g[e, 1], None,
                     b_qk_g[e, 0] * (scale * LOG2E), b_qk_g[e, 1], None]
        else:
            sizes = odd_sizes
            classes = [(HEAD_DIM, "B"), (HEAD_DIM, "B"), (0, None), (HEAD_DIM, None), (HEAD_DIM, None), (0, None)]
            gains = [c_qk_g[e, 0] * (scale * LOG2E), c_qk_g[e, 1], None,
                     d_qk_g[e, 0] * (scale * LOG2E), d_qk_g[e, 1], None]
        offs = _offsets(sizes)
        chunk_cls = _chunk_classes(sizes, classes)
        gain = _gain_vector(sizes, gains)

        w_in_b, w_in_idx = (w_in_first, 0) if i == 0 else (late["w_in"], i - 1)
        p = _inproj_call(x, g1n, vec(0, 0), vec(0, 1), w_in_b, w_in_idx, gain, chunk_cls, tabs, tm, f"inproj{i}")
        pc = _inproj_call(xc, g1n, vec(1, 0), vec(1, 1), w_in_b, w_in_idx, gain, chunk_cls, None, t,
                          f"inproj_ctx{i}")

        if i % 2 == 0:
            lam_init = _lambda_init(i)
            fa = functools.partial(_flash_call, "diff", offs=offs[0:3], n_heads=n_heads, lam_p=a_lambda[e],
                                   subln=a_subln_g[e], lam_init=lam_init)
            fb = functools.partial(_flash_call, "gqa", offs=offs[3:6], n_heads=n_kv, lam_p=None,
                                   subln=None, lam_init=None)
            bound_a = (A_SUB * LOGIT_BOUND_SLACK) * jnp.max(jnp.abs(gains[0])) * jnp.max(jnp.abs(gains[1]))
            bound_b = (HEAD_DIM * LOGIT_BOUND_SLACK) * jnp.max(jnp.abs(gains[3])) * jnp.max(jnp.abs(gains[4]))
            tq_a = 512
            slabs = {}
            if late is None:
                steps = n_heads * (s // tq_a)
                shapes = {k: _cast_slab_shape(a.size, steps) for k, a in late_f32.items()}
                slabs = {k: late_f32[k].reshape(sh) for k, sh in shapes.items() if sh is not None}
            res = lax.cond(bound_a <= MAX_LOGIT_BOUND,
                           lambda: fa(p, p, pc, tq=tq_a, name=f"diff_attn{i}", bound=bound_a.reshape(1),
                                      cast_slabs=tuple(slabs.values())),
                           lambda: fa(p, p, pc, tq=tq_a, name=f"diff_attn_online{i}",
                                      cast_slabs=tuple(slabs.values())))
            if late is None:
                ya = res[0] if slabs else res
                cast = dict(zip(slabs, res[1:])) if slabs else {}
                late = {k: (cast[k].reshape(a.shape) if k in cast else a.astype(BF16)) for k, a in late_f32.items()}
            else:
                ya = res
            yb = lax.cond(bound_b <= MAX_LOGIT_BOUND,
                          lambda: fb(p, p, pc, tq=256, name=f"gqa_attn{i}", bound=bound_b.reshape(1)),
                          lambda: fb(p, p, pc, tq=256, name=f"gqa_attn_online{i}"))
            if need_ctx:
                ya_c = fa(pc, None, pc, tq=t, name=f"diff_attn_ctx{i}")
                yb_c = fb(pc, None, pc, tq=t, name=f"gqa_attn_ctx{i}")
        else:
            ya = _win_call(p, pc, offs[0:3], n_kv, c_sink[e], 256)
            yb = _nbr_call(p, pc, offs[3:6], n_heads, d_rpb[e])
            if need_ctx:
                raise NotImplementedError("context stream after an odd layer")

        w_out_b, w_up_b, w_down_b = late["w_out"], late["w_up"], late["w_down"]
        x = _outproj_call(x, ya, yb, w_out_b, i, vec(0, 2), tm, f"outproj{i}")
        x = _ffn_call(x, g2n, vec(0, 3), vec(0, 4), vec(0, 5), w_up_b, conv_w, conv_b3, w_down_b, i, tm, tf,
                      f"ffn{i}")
        if need_ctx:
            xc = _outproj_call(xc, ya_c, yb_c, w_out_b, i, vec(1, 2), t, f"outproj_ctx{i}")
            xc = _ffn_call(xc, g2n, vec(1, 3), vec(1, 4), vec(1, 5), w_up_b, conv_w, conv_b3, w_down_b, i, t, tf,
                           f"ffn_ctx{i}")
    return x


def kernel(x, c, ctx, c_ctx, w_ada, b_ada, norm1_g, w_in, w_out, a_qk_g, a_lambda, a_subln_g, b_qk_g, c_qk_g,
           c_sink, d_qk_g, d_rpb, norm2_g, w_up, conv_w, conv_b, w_down):
    outs = [_forward(x[b], c[b], ctx[b], c_ctx, w_ada, b_ada, norm1_g, w_in, w_out, a_qk_g, a_lambda, a_subln_g,
                     b_qk_g, c_qk_g, c_sink, d_qk_g, d_rpb, norm2_g, w_up, conv_w, conv_b, w_down)
            for b in range(x.shape[0])]
    return jnp.stack(outs)
```

```python
import functools
import math

import jax
import jax.numpy as jnp
from jax import lax
from jax.experimental import pallas as pl
from jax.experimental.pallas import tpu as pltpu

GRID_W = 64
HEAD_DIM = 128
A_SUB = HEAD_DIM // 2
ROPE_THETA = 10000.0
EPS = 1e-6
NEG_INF = -1e30
N_MOD = 6
C_WINDOW = 128
NA_KH = 8
NA_KW = 16
CONV_W = 3
GQA_GROUP = 4
LOG2E = math.log2(math.e)
LOGIT_BOUND_SLACK = 1.02
MAX_LOGIT_BOUND = 50.0

LANES = 128
BF16_SUBLANES = 16
VMEM_LIMIT = 56 * 1024 * 1024

F32 = jnp.float32
BF16 = jnp.bfloat16


def _params(sem, vmem=VMEM_LIMIT):
    return pltpu.CompilerParams(dimension_semantics=sem, vmem_limit_bytes=vmem)


def _dot(a, b):
    return jnp.dot(a, b, preferred_element_type=F32)


def _dot_nt(a, b):
    return lax.dot_general(a, b, (((1,), (1,)), ((), ())), preferred_element_type=F32)


def _dot_tn(a, b):
    return lax.dot_general(a, b, (((0,), (0,)), ((), ())), preferred_element_type=F32)


def _norm_mod(x, g, sh, sc):
    inv = lax.rsqrt(jnp.mean(x * x, axis=-1, keepdims=True) + EPS)
    return (x * inv * g) * (1.0 + sc) + sh


def _mod_kernel(cb_ref, w_ref, b_ref, o_ref):
    tn = w_ref.shape[2]
    for r in range(2):
        cv = cb_ref[r]
        a = cv * (1.0 / (1.0 + jnp.exp(-cv)))
        for ch in range(tn // LANES):
            sl = slice(ch * LANES, (ch + 1) * LANES)
            o_ref[0, r:r + 1, sl] = jnp.sum(w_ref[0, :, sl] * a, axis=0, keepdims=True) + b_ref[0, :, sl]


def _mod_call(c, c_ctx, w_ada, b_ada):
    depth, d, n = w_ada.shape
    tn = 1024
    cb = jnp.broadcast_to(jnp.stack([c, c_ctx])[:, :, None], (2, d, LANES))
    return pl.pallas_call(
        _mod_kernel,
        grid=(depth, n // tn),
        in_specs=[pl.BlockSpec((2, d, LANES), lambda i, j: (0, 0, 0)),
                  pl.BlockSpec((1, d, tn), lambda i, j: (i, 0, j)),
                  pl.BlockSpec((1, 1, tn), lambda i, j: (i, 0, j))],
        out_specs=pl.BlockSpec((1, 2, tn), lambda i, j: (i, 0, j)),
        out_shape=jax.ShapeDtypeStruct((depth, 2, n), F32),
        compiler_params=_params(("parallel", "parallel")),
        name="adaln_mod",
    )(cb, w_ada, b_ada.reshape(depth, 1, n))


EPI_CHUNK = 256


def _head_norm(p, gain, width):
    shift = int(math.log2(width))
    r = lax.broadcasted_iota(jnp.int32, (EPI_CHUNK, EPI_CHUNK), 0) >> shift
    c = lax.broadcasted_iota(jnp.int32, (EPI_CHUNK, EPI_CHUNK), 1) >> shift
    bd = jnp.where(r == c, 1.0, 0.0).astype(BF16)
    seg = _dot((p * p).astype(BF16), bd)
    return p * lax.rsqrt(seg * (1.0 / width) + EPS) * gain


def _rope(y, half, tabs):
    c_ref, s1_ref, s2_ref = tabs
    c, s1, s2 = c_ref[...], s1_ref[...], s2_ref[...]
    outs = []
    for k in range(y.shape[1] // LANES):
        yk = y[:, k * LANES:(k + 1) * LANES]
        outs.append(yk * c + pltpu.roll(yk, LANES - half, 1) * s1 + pltpu.roll(yk, half, 1) * s2)
    return jnp.concatenate(outs, axis=1)


def _inproj_kernel(*refs, classes, rope, tn):
    x_ref, g_ref, sh_ref, sc_ref, w_ref, gain_ref = refs[:6]
    tabs = {"A": refs[6:9], "B": refs[9:12]} if rope else None
    o_ref = refs[-1]
    h = _norm_mod(x_ref[...], g_ref[...], sh_ref[...], sc_ref[...]).astype(BF16)
    per = tn // EPI_CHUNK
    for t in range(len(classes) // per):
        p = _dot(h, w_ref[:, t * tn:(t + 1) * tn])
        for ci, (width, kind) in enumerate(classes[t * per:(t + 1) * per]):
            sl = slice(t * tn + ci * EPI_CHUNK, t * tn + (ci + 1) * EPI_CHUNK)
            y = p[:, ci * EPI_CHUNK:(ci + 1) * EPI_CHUNK]
            if width:
                y = _head_norm(y, gain_ref[:, sl], width)
            if kind and rope:
                y = _rope(y, 16 if kind == "A" else 32, tabs[kind])
            o_ref[:, sl] = y.astype(BF16)


def _inproj_call(x, g, sh, sc, w, layer, gain, classes, tabs, tm, name):
    m, d = x.shape
    n = w.shape[2]
    rope = tabs is not None
    const = lambda i: (0, 0)
    in_specs = [pl.BlockSpec((tm, d), lambda i: (i, 0)),
                pl.BlockSpec((1, d), const), pl.BlockSpec((1, d), const), pl.BlockSpec((1, d), const),
                pl.BlockSpec((None, d, n), lambda i: (layer, 0, 0), pipeline_mode=pl.Buffered(1)),
                pl.BlockSpec((1, n), const)]
    args = [x, g, sh, sc, w, gain]
    if rope:
        in_specs += [pl.BlockSpec((tm, LANES), lambda i: (i, 0))] * 6
        args += list(tabs)
    return pl.pallas_call(
        functools.partial(_inproj_kernel, classes=tuple(classes), rope=rope, tn=2 * EPI_CHUNK),
        grid=(m // tm,),
        in_specs=in_specs,
        out_specs=pl.BlockSpec((tm, n), lambda i: (i, 0)),
        out_shape=jax.ShapeDtypeStruct((m, n), BF16),
        compiler_params=_params(("parallel",)),
        name=name,
    )(*args)


def _rope_tables(s, half):
    n_rows = s // GRID_W
    freqs = ROPE_THETA ** (-jnp.arange(half, dtype=F32) / half)
    zeros = jnp.zeros((s, half), F32)
    cs, s1s, s2s = [], [], []
    for count, expand in ((n_rows, lambda a: jnp.repeat(a, GRID_W, axis=0)),
                          (GRID_W, lambda a: jnp.tile(a, (n_rows, 1)))):
        ang = jnp.arange(count, dtype=jnp.int32).astype(F32)[:, None] * freqs[None, :]
        cos, sin = expand(jnp.cos(ang)), expand(jnp.sin(ang))
        cs += [cos, cos]
        s1s += [-sin, zeros]
        s2s += [zeros, sin]
    reps = LANES // (4 * half)
    return tuple(jnp.tile(jnp.concatenate(t, axis=1), (1, reps)) for t in (cs, s1s, s2s))


def _flash_kernel(*refs, mode, tq, tk, n_lat, lam_init, bounded, n_cast):
    if bounded:
        bound = refs[0][0]
        refs = refs[1:]
    q_ref, kc_ref, vc_ref = refs[:3]
    pos = 3
    if n_lat:
        k_ref, v_ref = refs[3:5]
        pos = 5
    if mode == "diff":
        lam_ref, sg_ref = refs[pos:pos + 2]
        pos += 2
    o_ref = refs[pos + n_cast]
    for src_ref, dst_ref in zip(refs[pos:pos + n_cast], refs[pos + n_cast + 1:pos + 2 * n_cast + 1]):
        dst_ref[...] = src_ref[...].astype(BF16)

    if mode == "diff":
        q = q_ref[...].astype(F32)
        lane = lax.broadcasted_iota(jnp.int32, q.shape, 1)
        qs = jnp.concatenate([jnp.where(lane < A_SUB, q, 0.0), jnp.where(lane >= A_SUB, q, 0.0)],
                             axis=0).astype(BF16)
    else:
        qs = jnp.concatenate([q_ref[:, g * HEAD_DIM:(g + 1) * HEAD_DIM] for g in range(GQA_GROUP)], axis=0)
    rows = qs.shape[0]

    def pv_dot(v, p):
        return lax.dot_general(v, p.astype(BF16), (((0,), (0,)), ((), ())), preferred_element_type=F32)

    if bounded:
        def step(k, v, carry):
            _, l, acc = carry
            p = jnp.exp2(_dot_nt(k, qs) - bound)
            return carry[0], l + jnp.sum(p, axis=0, keepdims=True), acc + pv_dot(v, p)
    else:
        def step(k, v, carry):
            m, l, acc = carry
            s = _dot_nt(k, qs)
            m_new = jnp.maximum(m, jnp.max(s, axis=0, keepdims=True))
            p = jnp.exp2(s - m_new)
            alpha = jnp.exp2(m - m_new)
            return m_new, alpha * l + jnp.sum(p, axis=0, keepdims=True), alpha * acc + pv_dot(v, p)

    carry = (jnp.full((1, rows), NEG_INF, F32), jnp.zeros((1, rows), F32), jnp.zeros((HEAD_DIM, rows), F32))
    carry = step(kc_ref[...], vc_ref[...], carry)
    if n_lat:
        def body(c, carry):
            off = pl.multiple_of(c * tk, tk)
            return step(k_ref[pl.ds(off, tk), :], v_ref[pl.ds(off, tk), :], carry)
        carry = lax.fori_loop(0, n_lat, body, carry, unroll=8)
    _, l, acc = carry
    o = (acc * (1.0 / l)).T
    if mode == "diff":
        lp = lam_ref[...]
        lam = (jnp.exp(jnp.sum(lp[0:1] * lp[1:2], axis=1, keepdims=True))
               - jnp.exp(jnp.sum(lp[2:3] * lp[3:4], axis=1, keepdims=True)) + lam_init)
        dlt = o[:tq] - lam * o[tq:]
        y = dlt * lax.rsqrt(jnp.mean(dlt * dlt, axis=-1, keepdims=True) + EPS) * sg_ref[...]
        o_ref[...] = (y * (1.0 - lam_init)).astype(BF16)
    else:
        for g in range(GQA_GROUP):
            o_ref[:, g * HEAD_DIM:(g + 1) * HEAD_DIM] = o[g * tq:(g + 1) * tq].astype(BF16)


def _flash_call(mode, qsrc, ksrc, pc, offs, n_heads, lam_p, subln, lam_init, tq, name, bound=None,
                cast_slabs=()):
    sq = qsrc.shape[0]
    t = pc.shape[0]
    qo, ko, vo = offs
    tk = 1024
    qw = HEAD_DIM if mode == "diff" else GQA_GROUP * HEAD_DIM
    qb, kb, vb = qo // qw, ko // HEAD_DIM, vo // HEAD_DIM
    in_specs = [pl.BlockSpec((tq, qw), lambda h, i: (i, qb + h)),
                pl.BlockSpec((t, HEAD_DIM), lambda h, i: (0, kb + h)),
                pl.BlockSpec((t, HEAD_DIM), lambda h, i: (0, vb + h))]
    args = [qsrc, pc, pc]
    if bound is not None:
        in_specs = [pl.BlockSpec(memory_space=pltpu.SMEM)] + in_specs
        args = [bound] + args
    n_lat = 0
    if ksrc is not None:
        s = ksrc.shape[0]
        tk = min(tk, s)
        assert s % tk == 0
        n_lat = s // tk
        in_specs += [pl.BlockSpec((s, HEAD_DIM), lambda h, i: (0, kb + h)),
                     pl.BlockSpec((s, HEAD_DIM), lambda h, i: (0, vb + h))]
        args += [ksrc, ksrc]
    if mode == "diff":
        in_specs += [pl.BlockSpec(lam_p.shape, lambda h, i: (0, 0)),
                     pl.BlockSpec((1, HEAD_DIM), lambda h, i: (0, 0))]
        args += [lam_p, subln.reshape(1, HEAD_DIM)]
    nq = sq // tq
    slab = lambda a: pl.BlockSpec((None,) + a.shape[1:], lambda h, i: (h * nq + i, 0, 0))
    in_specs += [slab(a) for a in cast_slabs]
    args += list(cast_slabs)
    outs = pl.pallas_call(
        functools.partial(_flash_kernel, mode=mode, tq=tq, tk=tk, n_lat=n_lat, lam_init=lam_init,
                          bounded=bound is not None, n_cast=len(cast_slabs)),
        grid=(n_heads, nq),
        in_specs=in_specs,
        out_specs=[pl.BlockSpec((tq, qw), lambda h, i: (i, h))] + [slab(a) for a in cast_slabs],
        out_shape=[jax.ShapeDtypeStruct((sq, n_heads * qw), BF16)]
                  + [jax.ShapeDtypeStruct(a.shape, BF16) for a in cast_slabs],
        compiler_params=_params(("parallel", "parallel")),
        name=name,
    )(*args)
    return tuple(outs) if cast_slabs else outs[0]


def _cast_slab_shape(shape, n_steps):
    rows, rem = divmod(shape[0] * shape[1], n_steps)
    if rem == 0 and rows and rows % BF16_SUBLANES == 0:
        return (n_steps, rows, shape[2])
    return None


WIN_PAIR = 2


def _win_kernel(sink_ref, q_ref, k_ref, v_ref, kc_ref, vc_ref, o_ref, *, tq, s_len):
    kvh = pl.program_id(0)
    q0 = pl.program_id(1) * tq
    band = tq + 2 * C_WINDOW
    pair = WIN_PAIR
    rows = pair * tq
    start = pl.multiple_of(jnp.clip(q0 - C_WINDOW, 0, s_len - band), C_WINDOW)
    kb = k_ref[pl.ds(start, band), :]
    vb = v_ref[pl.ds(start, band), :]
    lane = lax.broadcasted_iota(jnp.int32, (band, rows), 1)
    qpos = q0 + (lane & (tq - 1))
    kpos = start + lax.broadcasted_iota(jnp.int32, (band, rows), 0)
    valid = jnp.abs(qpos - kpos) <= C_WINDOW
    first = lax.broadcasted_iota(jnp.int32, (1, rows), 1) < tq
    passes = []
    for g0 in range(0, GQA_GROUP, pair):
        qs = jnp.concatenate([q_ref[:, g * HEAD_DIM:(g + 1) * HEAD_DIM] for g in range(g0, g0 + pair)], axis=0)
        passes.append((_dot_nt(kb, qs), _dot_nt(kc_ref[...], qs)))
    for g0, (s_lat, s_ctx) in zip(range(0, GQA_GROUP, pair), passes):
        s_lat = jnp.where(valid, s_lat, NEG_INF)
        sink = jnp.where(first, sink_ref[kvh * GQA_GROUP + g0], sink_ref[kvh * GQA_GROUP + g0 + 1]) * LOG2E
        m = jnp.maximum(jnp.maximum(jnp.max(s_lat, axis=0, keepdims=True),
                                    jnp.max(s_ctx, axis=0, keepdims=True)), sink)
        p_lat = jnp.exp2(s_lat - m)
        p_ctx = jnp.exp2(s_ctx - m)
        l = jnp.sum(p_lat, axis=0, keepdims=True) + jnp.sum(p_ctx, axis=0, keepdims=True) + jnp.exp2(sink - m)
        o = ((_dot_tn(vc_ref[...], p_ctx.astype(BF16)) + _dot_tn(vb, p_lat.astype(BF16))) * (1.0 / l)).T
        for g in range(pair):
            o_ref[:, (g0 + g) * HEAD_DIM:(g0 + g + 1) * HEAD_DIM] = o[g * tq:(g + 1) * tq].astype(BF16)


def _win_call(p, pc, offs, n_kv, sink, tq):
    s = p.shape[0]
    t = pc.shape[0]
    qo, ko, vo = offs
    qw = GQA_GROUP * HEAD_DIM
    qb, kb, vb = qo // qw, ko // HEAD_DIM, vo // HEAD_DIM
    assert s >= tq + 2 * C_WINDOW and tq & (tq - 1) == 0
    return pl.pallas_call(
        functools.partial(_win_kernel, tq=tq, s_len=s),
        grid=(n_kv, s // tq),
        in_specs=[pl.BlockSpec(memory_space=pltpu.SMEM),
                  pl.BlockSpec((tq, qw), lambda h, i: (i, qb + h)),
                  pl.BlockSpec((s, HEAD_DIM), lambda h, i: (0, kb + h)),
                  pl.BlockSpec((s, HEAD_DIM), lambda h, i: (0, vb + h)),
                  pl.BlockSpec((t, HEAD_DIM), lambda h, i: (0, kb + h)),
                  pl.BlockSpec((t, HEAD_DIM), lambda h, i: (0, vb + h))],
        out_specs=pl.BlockSpec((tq, qw), lambda h, i: (i, h)),
        out_shape=jax.ShapeDtypeStruct((s, n_kv * qw), BF16),
        compiler_params=_params(("parallel", "parallel")),
        name="window_attn",
    )(sink, p, p, p, pc, pc)


NBR_Q_ROWS = 4
NBR_WIN_ROWS = NBR_Q_ROWS + NA_KH


def _nbr_build_bias(rpb_ref, head, mats_scr, tab_scr, n_rows):
    na, nb = 2 * NA_KH - 1, 2 * NA_KW - 1
    shape = (GRID_W, 2 * GRID_W)
    wk = lax.broadcasted_iota(jnp.int32, shape, 0)
    lane = lax.broadcasted_iota(jnp.int32, shape, 1)
    wq = lane & (GRID_W - 1)
    dc = jnp.clip(wk - wq, -(NA_KW - 1), NA_KW - 1) + (NA_KW - 1)
    c_start = jnp.clip(wq - NA_KW // 2, 0, GRID_W - NA_KW)
    col_ok = jnp.logical_and(wk >= c_start, wk < c_start + NA_KW)

    def build(a, carry):
        base = (head * na + a) * nb
        mat = jnp.zeros(shape, F32)
        for b in range(nb):
            mat = jnp.where(dc == b, rpb_ref[base + b] * LOG2E, mat)
        mats_scr[a] = jnp.where(col_ok, mat, NEG_INF)
        return carry

    lax.fori_loop(0, na, build, 0)
    neg = jnp.full(shape, NEG_INF, F32)
    first_half = lane < GRID_W
    for cl, r0 in enumerate((0, NBR_Q_ROWS, n_rows - NBR_Q_ROWS)):
        ws = min(max(r0 - NA_KH // 2, 0), n_rows - NBR_WIN_ROWS)
        for kr in range(NBR_WIN_ROWS):
            for qp in range(NBR_Q_ROWS // 2):
                halves = []
                for r in (r0 + 2 * qp, r0 + 2 * qp + 1):
                    r_start = min(max(r - NA_KH // 2, 0), n_rows - NA_KH)
                    inside = r_start <= ws + kr < r_start + NA_KH
                    halves.append(mats_scr[ws + kr - r + NA_KH - 1] if inside else neg)
                tab_scr[cl, kr * GRID_W:(kr + 1) * GRID_W, qp * 2 * GRID_W:(qp + 1) * 2 * GRID_W] = (
                    jnp.where(first_half, halves[0], halves[1]))


NBR_HEADS_PER_STEP = 4


def _nbr_kernel(rpb_ref, q_ref, k_ref, v_ref, kc_ref, vc_ref, o_ref, mats_scr, tab_scr, *, tq, win, s_len):
    i = pl.program_id(1)
    nq = pl.num_programs(1)

    @pl.when(i == 0)
    def _():
        for hh in range(NBR_HEADS_PER_STEP):
            _nbr_build_bias(rpb_ref, pl.program_id(0) * NBR_HEADS_PER_STEP + hh, mats_scr, tab_scr.at[hh],
                            s_len // GRID_W)

    q0 = i * tq
    lead = (NA_KH // 2) * GRID_W
    start = pl.multiple_of(jnp.clip(q0 - lead, 0, s_len - win), lead)
    cls = jnp.where(i == 0, 0, jnp.where(i == nq - 1, 2, 1))
    heads = [slice(hh * HEAD_DIM, (hh + 1) * HEAD_DIM) for hh in range(NBR_HEADS_PER_STEP)]

    def scores(sl):
        return _dot_nt(k_ref[pl.ds(start, win), sl], q_ref[:, sl]), _dot_nt(kc_ref[:, sl], q_ref[:, sl])

    ahead = 2
    pending = [scores(sl) for sl in heads[:ahead]]
    for hh, sl in enumerate(heads):
        s_win, s_ctx = pending.pop(0)
        s_win = s_win + tab_scr[hh, cls]
        m = jnp.maximum(jnp.max(s_win, axis=0, keepdims=True), jnp.max(s_ctx, axis=0, keepdims=True))
        p_win = jnp.exp2(s_win - m)
        p_ctx = jnp.exp2(s_ctx - m)
        l = jnp.sum(p_win, axis=0, keepdims=True) + jnp.sum(p_ctx, axis=0, keepdims=True)
        o = (_dot_tn(vc_ref[:, sl], p_ctx.astype(BF16))
             + _dot_tn(v_ref[pl.ds(start, win), sl], p_win.astype(BF16)))
        o_ref[:, sl] = ((o * (1.0 / l)).T).astype(BF16)
        if hh + ahead < len(heads):
            pending.append(scores(heads[hh + ahead]))


def _nbr_call(p, pc, offs, n_heads, rpb):
    s = p.shape[0]
    t = pc.shape[0]
    tq = NBR_Q_ROWS * GRID_W
    win = NBR_WIN_ROWS * GRID_W
    bw = NBR_HEADS_PER_STEP * HEAD_DIM
    qb, kb, vb = (o // bw for o in offs)
    assert s // GRID_W >= NBR_WIN_ROWS and rpb.shape[1:] == (2 * NA_KH - 1, 2 * NA_KW - 1)
    assert n_heads % NBR_HEADS_PER_STEP == 0 and all(o % bw == 0 for o in offs)
    return pl.pallas_call(
        functools.partial(_nbr_kernel, tq=tq, win=win, s_len=s),
        grid=(n_heads // NBR_HEADS_PER_STEP, s // tq),
        in_specs=[pl.BlockSpec(memory_space=pltpu.SMEM),
                  pl.BlockSpec((tq, bw), lambda h, i: (i, qb + h)),
                  pl.BlockSpec((s, bw), lambda h, i: (0, kb + h)),
                  pl.BlockSpec((s, bw), lambda h, i: (0, vb + h)),
                  pl.BlockSpec((t, bw), lambda h, i: (0, kb + h)),
                  pl.BlockSpec((t, bw), lambda h, i: (0, vb + h))],
        out_specs=pl.BlockSpec((tq, bw), lambda h, i: (i, h)),
        out_shape=jax.ShapeDtypeStruct((s, n_heads * HEAD_DIM), BF16),
        scratch_shapes=[pltpu.VMEM((2 * NA_KH - 1, GRID_W, 2 * GRID_W), F32),
                        pltpu.VMEM((NBR_HEADS_PER_STEP, 3, win, tq), F32)],
        compiler_params=_params(("parallel", "arbitrary")),
        name="nbr_attn",
    )(rpb.astype(F32).reshape(-1), p, p, p, pc, pc)


def _outproj_kernel(x_ref, ya_ref, yb_ref, wa_ref, wb_ref, g_ref, o_ref):
    acc = _dot(ya_ref[...], wa_ref[...]) + _dot(yb_ref[...], wb_ref[...])
    o_ref[...] = x_ref[...] + g_ref[...] * acc


def _outproj_call(x, ya, yb, w, layer, gate, tm, name):
    m, d = x.shape
    k = ya.shape[1]
    assert yb.shape[1] == k and w.shape[1] == 2 * k
    return pl.pallas_call(
        _outproj_kernel,
        grid=(m // tm,),
        in_specs=[pl.BlockSpec((tm, d), lambda i: (i, 0)),
                  pl.BlockSpec((tm, k), lambda i: (i, 0)),
                  pl.BlockSpec((tm, k), lambda i: (i, 0)),
                  pl.BlockSpec((None, k, d), lambda i: (layer, 0, 0)),
                  pl.BlockSpec((None, k, d), lambda i: (layer, 1, 0)),
                  pl.BlockSpec((1, d), lambda i: (0, 0))],
        out_specs=pl.BlockSpec((tm, d), lambda i: (i, 0)),
        out_shape=jax.ShapeDtypeStruct((m, d), F32),
        compiler_params=_params(("parallel",)),
        name=name,
    )(x, ya, yb, w, w, gate)


HALO = BF16_SUBLANES


def _ffn_kernel(xm_ref, xp_ref, xn_ref, gn_ref, sh_ref, sc_ref, gate_ref, wa_ref, wg_ref,
                cwa_ref, cwg_ref, cba_ref, cbg_ref, wd_ref, o_ref,
                h_scr, acc_scr, ua0_scr, ua1_scr, ug0_scr, ug1_scr):
    i, j = pl.program_id(0), pl.program_id(1)
    ni, nj = pl.num_programs(0), pl.num_programs(1)
    tm = xm_ref.shape[0]

    @pl.when(j == 0)
    def _():
        nm = lambda x: _norm_mod(x, gn_ref[...], sh_ref[...], sc_ref[...])
        h_scr[0:HALO, :] = jnp.where(i > 0, nm(xp_ref[...]), 0.0).astype(BF16)
        h_scr[HALO:HALO + tm, :] = nm(xm_ref[...]).astype(BF16)
        h_scr[HALO + tm:, :] = jnp.where(i < ni - 1, nm(xn_ref[...]), 0.0).astype(BF16)
        acc_scr[...] = jnp.zeros_like(acc_scr)

    def conv(u_scr, cw_ref, cb_ref, sl):
        out = cb_ref[:, sl] + u_scr[pl.ds(HALO - 1, tm), :] * cw_ref[0:1, sl]
        for t in range(1, CONV_W):
            out = out + u_scr[pl.ds(HALO - 1 + t, tm), :] * cw_ref[t:t + 1, sl]
        return out

    def silu_conv(u_scr, sl):
        g = conv(u_scr, cwg_ref, cbg_ref, sl)
        return g * (1.0 / (1.0 + jnp.exp(-g)))

    h = h_scr[...]
    half = wa_ref.shape[1] // 2
    lo, hi = slice(0, half), slice(half, 2 * half)
    ug0_scr[...] = _dot(h, wg_ref[:, lo])
    ug1_scr[...] = _dot(h, wg_ref[:, hi])
    sg0 = silu_conv(ug0_scr, lo)
    ua0_scr[...] = _dot(h, wa_ref[:, lo])
    sg1 = silu_conv(ug1_scr, hi)
    ua1_scr[...] = _dot(h, wa_ref[:, hi])
    z0 = sg0 * conv(ua0_scr, cwa_ref, cba_ref, lo)
    acc_scr[...] += _dot(z0.astype(BF16), wd_ref[lo, :])
    z1 = sg1 * conv(ua1_scr, cwa_ref, cba_ref, hi)
    acc_scr[...] += _dot(z1.astype(BF16), wd_ref[hi, :])

    @pl.when(j == nj - 1)
    def _():
        o_ref[...] = xm_ref[...] + gate_ref[...] * acc_scr[...]


def _ffn_call(x, gn, sh, sc, gate, w_up, conv_w, conv_b, w_down, layer, tm, tf, name):
    m, d = x.shape
    f = w_down.shape[1]
    nfb = f // tf
    hb = tm // HALO
    last = m // HALO - 1
    row = lambda i, j: (0, 0)
    return pl.pallas_call(
        _ffn_kernel,
        grid=(m // tm, nfb),
        in_specs=[pl.BlockSpec((tm, d), lambda i, j: (i, 0)),
                  pl.BlockSpec((HALO, d), lambda i, j: (jnp.maximum(i * hb - 1, 0), 0)),
                  pl.BlockSpec((HALO, d), lambda i, j: (jnp.minimum((i + 1) * hb, last), 0)),
                  pl.BlockSpec((1, d), row), pl.BlockSpec((1, d), row), pl.BlockSpec((1, d), row),
                  pl.BlockSpec((1, d), row),
                  pl.BlockSpec((None, d, tf), lambda i, j: (layer, 0, j)),
                  pl.BlockSpec((None, d, tf), lambda i, j: (layer, 0, nfb + j)),
                  pl.BlockSpec((None, CONV_W, tf), lambda i, j: (layer, 0, j)),
                  pl.BlockSpec((None, CONV_W, tf), lambda i, j: (layer, 0, nfb + j)),
                  pl.BlockSpec((None, 1, tf), lambda i, j: (layer, 0, j)),
                  pl.BlockSpec((None, 1, tf), lambda i, j: (layer, 0, nfb + j)),
                  pl.BlockSpec((None, tf, d), lambda i, j: (layer, j, 0))],
        out_specs=pl.BlockSpec((tm, d), lambda i, j: (i, 0)),
        out_shape=jax.ShapeDtypeStruct((m, d), F32),
        scratch_shapes=[pltpu.VMEM((tm + 2 * HALO, d), BF16),
                        pltpu.VMEM((tm, d), F32)] + [pltpu.VMEM((tm + 2 * HALO, tf // 2), F32)] * 4,
        compiler_params=_params(("parallel", "arbitrary")),
        name=name,
    )(x, x, x, gn, sh, sc, gate, w_up, w_up, conv_w, conv_w, conv_b, conv_b, w_down)


def _offsets(sizes):
    offs, o = [], 0
    for s in sizes:
        offs.append(o)
        o += s
    return offs


def _chunk_classes(sizes, classes):
    out = []
    for s, c in zip(sizes, classes):
        out += [c] * (s // EPI_CHUNK)
    return out


def _gain_vector(sizes, gains):
    parts = []
    for s, g in zip(sizes, gains):
        parts.append(jnp.ones((s,), F32) if g is None else jnp.tile(g.astype(F32), s // g.shape[0]))
    return jnp.concatenate(parts).reshape(1, -1)


def _lambda_init(layer_idx):
    return 0.8 - 0.6 * math.exp(-0.3 * layer_idx)


def _forward(x, c, ctx, c_ctx, w_ada, b_ada, norm1_g, w_in, w_out, a_qk_g, a_lambda, a_subln_g,
             b_qk_g, c_qk_g, c_sink, d_qk_g, d_rpb, norm2_g, w_up, conv_w, conv_b, w_down):
    s, d = x.shape
    t = ctx.shape[0]
    depth = w_ada.shape[0]
    n_heads = d // (2 * HEAD_DIM)
    n_kv = n_heads // GQA_GROUP
    hw = n_heads * HEAD_DIM
    kvw = n_kv * HEAD_DIM
    even_sizes = (hw, hw, hw, hw, kvw, kvw)
    odd_sizes = (hw, kvw, kvw, hw, hw, hw)
    tm = 512
    tf = 512

    mod = _mod_call(c, c_ctx, w_ada, b_ada)
    rope_a = _rope_tables(s, 16)
    rope_b = _rope_tables(s, 32)
    tabs = rope_a + rope_b
    scale_a = A_SUB ** -0.5
    scale = HEAD_DIM ** -0.5

    w_in_first = w_in[:1].astype(BF16)
    late_f32 = {"w_in": w_in[1:], "w_out": w_out, "w_up": w_up, "w_down": w_down}
    late = None
    conv_b3 = conv_b.reshape(depth, 1, -1)

    xc = ctx
    for i in range(depth):
        need_ctx = i < depth - 1
        e = i // 2
        vec = lambda r, k: mod[i, r:r + 1, k * d:(k + 1) * d]
        g1n = norm1_g[i].reshape(1, d)
        g2n = norm2_g[i].reshape(1, d)
        if i % 2 == 0:
            sizes = even_sizes
            classes = [(A_SUB, "A"), (A_SUB, "A"), (0, None), (HEAD_DIM, "B"), (HEAD_DIM, "B"), (0, None)]
            gains = [a_qk_g[e, 0] * (scale_a * LOG2E), a_qk_g[e, 1], None,
                     b_qk_g[e, 0] * (scale * LOG2E), b_qk_g[e, 1], None]
        else:
            sizes = odd_sizes
            classes = [(HEAD_DIM, "B"), (HEAD_DIM, "B"), (0, None), (HEAD_DIM, None), (HEAD_DIM, None), (0, None)]
            gains = [c_qk_g[e, 0] * (scale * LOG2E), c_qk_g[e, 1], None,
                     d_qk_g[e, 0] * (scale * LOG2E), d_qk_g[e, 1], None]
        offs = _offsets(sizes)
        chunk_cls = _chunk_classes(sizes, classes)
        gain = _gain_vector(sizes, gains)

        w_in_b, w_in_idx = (w_in_first, 0) if i == 0 else (late["w_in"], i - 1)
        p = _inproj_call(x, g1n, vec(0, 0), vec(0, 1), w_in_b, w_in_idx, gain, chunk_cls, tabs, tm, f"inproj{i}")
        pc = _inproj_call(xc, g1n, vec(1, 0), vec(1, 1), w_in_b, w_in_idx, gain, chunk_cls, None, t,
                          f"inproj_ctx{i}")

        if i % 2 == 0:
            lam_init = _lambda_init(i)
            fa = functools.partial(_flash_call, "diff", offs=offs[0:3], n_heads=n_heads, lam_p=a_lambda[e],
                                   subln=a_subln_g[e], lam_init=lam_init)
            fb = functools.partial(_flash_call, "gqa", offs=offs[3:6], n_heads=n_kv, lam_p=None,
                                   subln=None, lam_init=None)
            bound_a = (A_SUB * LOGIT_BOUND_SLACK) * jnp.max(jnp.abs(gains[0])) * jnp.max(jnp.abs(gains[1]))
            bound_b = (HEAD_DIM * LOGIT_BOUND_SLACK) * jnp.max(jnp.abs(gains[3])) * jnp.max(jnp.abs(gains[4]))
            tq_a, tq_b = 512, 256
            slabs_a, slabs_b = {}, {}
            if late is None:
                for k, a in late_f32.items():
                    for slabs, steps in ((slabs_a, n_heads * (s // tq_a)), (slabs_b, n_kv * (s // tq_b))):
                        shape = _cast_slab_shape(a.shape, steps)
                        if shape is not None:
                            slabs[k] = a.reshape(shape)
                            break
            res_a = lax.cond(bound_a <= MAX_LOGIT_BOUND,
                             lambda: fa(p, p, pc, tq=tq_a, name=f"diff_attn{i}", bound=bound_a.reshape(1),
                                        cast_slabs=tuple(slabs_a.values())),
                             lambda: fa(p, p, pc, tq=tq_a, name=f"diff_attn_online{i}",
                                        cast_slabs=tuple(slabs_a.values())))
            res_b = lax.cond(bound_b <= MAX_LOGIT_BOUND,
                             lambda: fb(p, p, pc, tq=tq_b, name=f"gqa_attn{i}", bound=bound_b.reshape(1),
                                        cast_slabs=tuple(slabs_b.values())),
                             lambda: fb(p, p, pc, tq=tq_b, name=f"gqa_attn_online{i}",
                                        cast_slabs=tuple(slabs_b.values())))
            ya = res_a[0] if slabs_a else res_a
            yb = res_b[0] if slabs_b else res_b
            if late is None:
                cast = dict(zip(slabs_a, res_a[1:])) if slabs_a else {}
                cast.update(dict(zip(slabs_b, res_b[1:])) if slabs_b else {})
                late = {k: (cast[k].reshape(a.shape) if k in cast else a.astype(BF16)) for k, a in late_f32.items()}
            if need_ctx:
                ya_c = fa(pc, None, pc, tq=t, name=f"diff_attn_ctx{i}")
                yb_c = fb(pc, None, pc, tq=t, name=f"gqa_attn_ctx{i}")
        else:
            ya = _win_call(p, pc, offs[0:3], n_kv, c_sink[e], 256)
            yb = _nbr_call(p, pc, offs[3:6], n_heads, d_rpb[e])
            if need_ctx:
                raise NotImplementedError("context stream after an odd layer")

        w_out_b, w_up_b, w_down_b = late["w_out"], late["w_up"], late["w_down"]
        x = _outproj_call(x, ya, yb, w_out_b, i, vec(0, 2), tm, f"outproj{i}")
        x = _ffn_call(x, g2n, vec(0, 3), vec(0, 4), vec(0, 5), w_up_b, conv_w, conv_b3, w_down_b, i, tm, tf,
                      f"ffn{i}")
        if need_ctx:
            xc = _outproj_call(xc, ya_c, yb_c, w_out_b, i, vec(1, 2), t, f"outproj_ctx{i}")
            xc = _ffn_call(xc, g2n, vec(1, 3), vec(1, 4), vec(1, 5), w_up_b, conv_w, conv_b3, w_down_b, i, t, tf,
                           f"ffn_ctx{i}")
    return x


def kernel(x, c, ctx, c_ctx, w_ada, b_ada, norm1_g, w_in, w_out, a_qk_g, a_lambda, a_subln_g, b_qk_g, c_qk_g,
           c_sink, d_qk_g, d_rpb, norm2_g, w_up, conv_w, conv_b, w_down):
    outs = [_forward(x[b], c[b], ctx[b], c_ctx, w_ada, b_ada, norm1_g, w_in, w_out, a_qk_g, a_lambda, a_subln_g,
                     b_qk_g, c_qk_g, c_sink, d_qk_g, d_rpb, norm2_g, w_up, conv_w, conv_b, w_down)
            for b in range(x.shape[0])]
    return jnp.stack(outs)
```

```python
import functools
import math

import jax
import jax.numpy as jnp
from jax import lax
from jax.experimental import pallas as pl
from jax.experimental.pallas import tpu as pltpu

GRID_W = 64
HEAD_DIM = 128
A_SUB = HEAD_DIM // 2
ROPE_THETA = 10000.0
EPS = 1e-6
NEG_INF = -1e30
N_MOD = 6
C_WINDOW = 128
NA_KH = 8
NA_KW = 16
CONV_W = 3
GQA_GROUP = 4
LOG2E = math.log2(math.e)
LOGIT_BOUND_SLACK = 1.02
MAX_LOGIT_BOUND = 50.0

LANES = 128
BF16_SUBLANES = 16
VMEM_LIMIT = 56 * 1024 * 1024

F32 = jnp.float32
BF16 = jnp.bfloat16


def _params(sem, vmem=VMEM_LIMIT):
    return pltpu.CompilerParams(dimension_semantics=sem, vmem_limit_bytes=vmem)


def _dot(a, b):
    return jnp.dot(a, b, preferred_element_type=F32)


def _dot_nt(a, b):
    return lax.dot_general(a, b, (((1,), (1,)), ((), ())), preferred_element_type=F32)


def _dot_tn(a, b):
    return lax.dot_general(a, b, (((0,), (0,)), ((), ())), preferred_element_type=F32)


def _norm_mod(x, g, sh, sc):
    inv = lax.rsqrt(jnp.mean(x * x, axis=-1, keepdims=True) + EPS)
    return (x * inv * g) * (1.0 + sc) + sh


def _mod_kernel(cb_ref, w_ref, b_ref, o_ref):
    tn = w_ref.shape[2]
    for r in range(2):
        cv = cb_ref[r]
        a = cv * (1.0 / (1.0 + jnp.exp(-cv)))
        for ch in range(tn // LANES):
            sl = slice(ch * LANES, (ch + 1) * LANES)
            o_ref[0, r:r + 1, sl] = jnp.sum(w_ref[0, :, sl] * a, axis=0, keepdims=True) + b_ref[0, :, sl]


def _mod_call(c, c_ctx, w_ada, b_ada):
    depth, d, n = w_ada.shape
    tn = 1024
    cb = jnp.broadcast_to(jnp.stack([c, c_ctx])[:, :, None], (2, d, LANES))
    return pl.pallas_call(
        _mod_kernel,
        grid=(depth, n // tn),
        in_specs=[pl.BlockSpec((2, d, LANES), lambda i, j: (0, 0, 0)),
                  pl.BlockSpec((1, d, tn), lambda i, j: (i, 0, j)),
                  pl.BlockSpec((1, 1, tn), lambda i, j: (i, 0, j))],
        out_specs=pl.BlockSpec((1, 2, tn), lambda i, j: (i, 0, j)),
        out_shape=jax.ShapeDtypeStruct((depth, 2, n), F32),
        compiler_params=_params(("parallel", "parallel")),
        name="adaln_mod",
    )(cb, w_ada, b_ada.reshape(depth, 1, n))


EPI_CHUNK = 256


def _head_norm(p, gain, width):
    shift = int(math.log2(width))
    r = lax.broadcasted_iota(jnp.int32, (EPI_CHUNK, EPI_CHUNK), 0) >> shift
    c = lax.broadcasted_iota(jnp.int32, (EPI_CHUNK, EPI_CHUNK), 1) >> shift
    bd = jnp.where(r == c, 1.0, 0.0).astype(BF16)
    seg = _dot((p * p).astype(BF16), bd)
    return p * lax.rsqrt(seg * (1.0 / width) + EPS) * gain


def _rope(y, half, tabs):
    c_ref, s1_ref, s2_ref = tabs
    c, s1, s2 = c_ref[...], s1_ref[...], s2_ref[...]
    outs = []
    for k in range(y.shape[1] // LANES):
        yk = y[:, k * LANES:(k + 1) * LANES]
        outs.append(yk * c + pltpu.roll(yk, LANES - half, 1) * s1 + pltpu.roll(yk, half, 1) * s2)
    return jnp.concatenate(outs, axis=1)


def _inproj_kernel(*refs, classes, rope, tn):
    x_ref, g_ref, sh_ref, sc_ref, w_ref, gain_ref = refs[:6]
    tabs = {"A": refs[6:9], "B": refs[9:12]} if rope else None
    o_ref = refs[-1]
    h = _norm_mod(x_ref[...], g_ref[...], sh_ref[...], sc_ref[...]).astype(BF16)
    per = tn // EPI_CHUNK
    for t in range(len(classes) // per):
        p = _dot(h, w_ref[:, t * tn:(t + 1) * tn])
        for ci, (width, kind) in enumerate(classes[t * per:(t + 1) * per]):
            sl = slice(t * tn + ci * EPI_CHUNK, t * tn + (ci + 1) * EPI_CHUNK)
            y = p[:, ci * EPI_CHUNK:(ci + 1) * EPI_CHUNK]
            if width:
                y = _head_norm(y, gain_ref[:, sl], width)
            if kind and rope:
                y = _rope(y, 16 if kind == "A" else 32, tabs[kind])
            o_ref[:, sl] = y.astype(BF16)


def _inproj_call(x, g, sh, sc, w, layer, gain, classes, tabs, tm, name):
    m, d = x.shape
    n = w.shape[2]
    rope = tabs is not None
    const = lambda i: (0, 0)
    in_specs = [pl.BlockSpec((tm, d), lambda i: (i, 0)),
                pl.BlockSpec((1, d), const), pl.BlockSpec((1, d), const), pl.BlockSpec((1, d), const),
                pl.BlockSpec((None, d, n), lambda i: (layer, 0, 0), pipeline_mode=pl.Buffered(1)),
                pl.BlockSpec((1, n), const)]
    args = [x, g, sh, sc, w, gain]
    if rope:
        in_specs += [pl.BlockSpec((tm, LANES), lambda i: (i, 0))] * 6
        args += list(tabs)
    return pl.pallas_call(
        functools.partial(_inproj_kernel, classes=tuple(classes), rope=rope, tn=2 * EPI_CHUNK),
        grid=(m // tm,),
        in_specs=in_specs,
        out_specs=pl.BlockSpec((tm, n), lambda i: (i, 0)),
        out_shape=jax.ShapeDtypeStruct((m, n), BF16),
        compiler_params=_params(("parallel",)),
        name=name,
    )(*args)


def _rope_tables(s, half):
    n_rows = s // GRID_W
    freqs = ROPE_THETA ** (-jnp.arange(half, dtype=F32) / half)
    zeros = jnp.zeros((s, half), F32)
    cs, s1s, s2s = [], [], []
    for count, expand in ((n_rows, lambda a: jnp.repeat(a, GRID_W, axis=0)),
                          (GRID_W, lambda a: jnp.tile(a, (n_rows, 1)))):
        ang = jnp.arange(count, dtype=jnp.int32).astype(F32)[:, None] * freqs[None, :]
        cos, sin = expand(jnp.cos(ang)), expand(jnp.sin(ang))
        cs += [cos, cos]
        s1s += [-sin, zeros]
        s2s += [zeros, sin]
    reps = LANES // (4 * half)
    return tuple(jnp.tile(jnp.concatenate(t, axis=1), (1, reps)) for t in (cs, s1s, s2s))


def _flash_kernel(*refs, mode, tq, tk, n_lat, lam_init, bounded, n_cast):
    if bounded:
        bound = refs[0][0]
        refs = refs[1:]
    q_ref, kc_ref, vc_ref = refs[:3]
    pos = 3
    if n_lat:
        k_ref, v_ref = refs[3:5]
        pos = 5
    if mode == "diff":
        lam_ref, sg_ref = refs[pos:pos + 2]
        pos += 2
    o_ref = refs[pos + n_cast]
    for src_ref, dst_ref in zip(refs[pos:pos + n_cast], refs[pos + n_cast + 1:pos + 2 * n_cast + 1]):
        dst_ref[...] = src_ref[...].astype(BF16)

    if mode == "diff":
        q = q_ref[...].astype(F32)
        lane = lax.broadcasted_iota(jnp.int32, q.shape, 1)
        qs = jnp.concatenate([jnp.where(lane < A_SUB, q, 0.0), jnp.where(lane >= A_SUB, q, 0.0)],
                             axis=0).astype(BF16)
    else:
        qs = jnp.concatenate([q_ref[:, g * HEAD_DIM:(g + 1) * HEAD_DIM] for g in range(GQA_GROUP)], axis=0)
    rows = qs.shape[0]

    def pv_dot(v, p):
        return lax.dot_general(v, p.astype(BF16), (((0,), (0,)), ((), ())), preferred_element_type=F32)

    if bounded:
        def step(k, v, carry):
            _, l, acc = carry
            p = jnp.exp2(_dot_nt(k, qs) - bound)
            return carry[0], l + jnp.sum(p, axis=0, keepdims=True), acc + pv_dot(v, p)
    else:
        def step(k, v, carry):
            m, l, acc = carry
            s = _dot_nt(k, qs)
            m_new = jnp.maximum(m, jnp.max(s, axis=0, keepdims=True))
            p = jnp.exp2(s - m_new)
            alpha = jnp.exp2(m - m_new)
            return m_new, alpha * l + jnp.sum(p, axis=0, keepdims=True), alpha * acc + pv_dot(v, p)

    carry = (jnp.full((1, rows), NEG_INF, F32), jnp.zeros((1, rows), F32), jnp.zeros((HEAD_DIM, rows), F32))
    carry = step(kc_ref[...], vc_ref[...], carry)
    if n_lat:
        def body(c, carry):
            off = pl.multiple_of(c * tk, tk)
            return step(k_ref[pl.ds(off, tk), :], v_ref[pl.ds(off, tk), :], carry)
        carry = lax.fori_loop(0, n_lat, body, carry, unroll=8)
    _, l, acc = carry
    o = (acc * (1.0 / l)).T
    if mode == "diff":
        lp = lam_ref[...]
        lam = (jnp.exp(jnp.sum(lp[0:1] * lp[1:2], axis=1, keepdims=True))
               - jnp.exp(jnp.sum(lp[2:3] * lp[3:4], axis=1, keepdims=True)) + lam_init)
        dlt = o[:tq] - lam * o[tq:]
        y = dlt * lax.rsqrt(jnp.mean(dlt * dlt, axis=-1, keepdims=True) + EPS) * sg_ref[...]
        o_ref[...] = (y * (1.0 - lam_init)).astype(BF16)
    else:
        for g in range(GQA_GROUP):
            o_ref[:, g * HEAD_DIM:(g + 1) * HEAD_DIM] = o[g * tq:(g + 1) * tq].astype(BF16)


def _flash_call(mode, qsrc, ksrc, pc, offs, n_heads, lam_p, subln, lam_init, tq, name, bound=None,
                cast_slabs=()):
    sq = qsrc.shape[0]
    t = pc.shape[0]
    qo, ko, vo = offs
    tk = 1024
    qw = HEAD_DIM if mode == "diff" else GQA_GROUP * HEAD_DIM
    qb, kb, vb = qo // qw, ko // HEAD_DIM, vo // HEAD_DIM
    in_specs = [pl.BlockSpec((tq, qw), lambda h, i: (i, qb + h)),
                pl.BlockSpec((t, HEAD_DIM), lambda h, i: (0, kb + h)),
                pl.BlockSpec((t, HEAD_DIM), lambda h, i: (0, vb + h))]
    args = [qsrc, pc, pc]
    if bound is not None:
        in_specs = [pl.BlockSpec(memory_space=pltpu.SMEM)] + in_specs
        args = [bound] + args
    n_lat = 0
    if ksrc is not None:
        s = ksrc.shape[0]
        tk = min(tk, s)
        assert s % tk == 0
        n_lat = s // tk
        in_specs += [pl.BlockSpec((s, HEAD_DIM), lambda h, i: (0, kb + h)),
                     pl.BlockSpec((s, HEAD_DIM), lambda h, i: (0, vb + h))]
        args += [ksrc, ksrc]
    if mode == "diff":
        in_specs += [pl.BlockSpec(lam_p.shape, lambda h, i: (0, 0)),
                     pl.BlockSpec((1, HEAD_DIM), lambda h, i: (0, 0))]
        args += [lam_p, subln.reshape(1, HEAD_DIM)]
    nq = sq // tq
    slab = lambda a: pl.BlockSpec((None,) + a.shape[1:], lambda h, i: (h * nq + i, 0, 0))
    in_specs += [slab(a) for a in cast_slabs]
    args += list(cast_slabs)
    outs = pl.pallas_call(
        functools.partial(_flash_kernel, mode=mode, tq=tq, tk=tk, n_lat=n_lat, lam_init=lam_init,
                          bounded=bound is not None, n_cast=len(cast_slabs)),
        grid=(n_heads, nq),
        in_specs=in_specs,
        out_specs=[pl.BlockSpec((tq, qw), lambda h, i: (i, h))] + [slab(a) for a in cast_slabs],
        out_shape=[jax.ShapeDtypeStruct((sq, n_heads * qw), BF16)]
                  + [jax.ShapeDtypeStruct(a.shape, BF16) for a in cast_slabs],
        compiler_params=_params(("parallel", "parallel")),
        name=name,
    )(*args)
    return tuple(outs) if cast_slabs else outs[0]


def _cast_slab_shape(shape, n_steps):
    rows, rem = divmod(shape[0] * shape[1], n_steps)
    if rem == 0 and rows and rows % BF16_SUBLANES == 0:
        return (n_steps, rows, shape[2])
    return None


WIN_PAIR = 2


def _win_kernel(sink_ref, q_ref, k_ref, v_ref, kc_ref, vc_ref, o_ref, *, tq, s_len):
    kvh = pl.program_id(0)
    q0 = pl.program_id(1) * tq
    band = tq + 2 * C_WINDOW
    pair = WIN_PAIR
    rows = pair * tq
    start = pl.multiple_of(jnp.clip(q0 - C_WINDOW, 0, s_len - band), C_WINDOW)
    kb = k_ref[pl.ds(start, band), :]
    vb = v_ref[pl.ds(start, band), :]
    lane = lax.broadcasted_iota(jnp.int32, (band, rows), 1)
    qpos = q0 + (lane & (tq - 1))
    kpos = start + lax.broadcasted_iota(jnp.int32, (band, rows), 0)
    valid = jnp.abs(qpos - kpos) <= C_WINDOW
    first = lax.broadcasted_iota(jnp.int32, (1, rows), 1) < tq
    passes = []
    for g0 in range(0, GQA_GROUP, pair):
        qs = jnp.concatenate([q_ref[:, g * HEAD_DIM:(g + 1) * HEAD_DIM] for g in range(g0, g0 + pair)], axis=0)
        passes.append((_dot_nt(kb, qs), _dot_nt(kc_ref[...], qs)))
    for g0, (s_lat, s_ctx) in zip(range(0, GQA_GROUP, pair), passes):
        s_lat = jnp.where(valid, s_lat, NEG_INF)
        sink = jnp.where(first, sink_ref[kvh * GQA_GROUP + g0], sink_ref[kvh * GQA_GROUP + g0 + 1]) * LOG2E
        m = jnp.maximum(jnp.maximum(jnp.max(s_lat, axis=0, keepdims=True),
                                    jnp.max(s_ctx, axis=0, keepdims=True)), sink)
        p_lat = jnp.exp2(s_lat - m)
        p_ctx = jnp.exp2(s_ctx - m)
        l = jnp.sum(p_lat, axis=0, keepdims=True) + jnp.sum(p_ctx, axis=0, keepdims=True) + jnp.exp2(sink - m)
        o = ((_dot_tn(vc_ref[...], p_ctx.astype(BF16)) + _dot_tn(vb, p_lat.astype(BF16))) * (1.0 / l)).T
        for g in range(pair):
            o_ref[:, (g0 + g) * HEAD_DIM:(g0 + g + 1) * HEAD_DIM] = o[g * tq:(g + 1) * tq].astype(BF16)


def _win_call(p, pc, offs, n_kv, sink, tq):
    s = p.shape[0]
    t = pc.shape[0]
    qo, ko, vo = offs
    qw = GQA_GROUP * HEAD_DIM
    qb, kb, vb = qo // qw, ko // HEAD_DIM, vo // HEAD_DIM
    assert s >= tq + 2 * C_WINDOW and tq & (tq - 1) == 0
    return pl.pallas_call(
        functools.partial(_win_kernel, tq=tq, s_len=s),
        grid=(n_kv, s // tq),
        in_specs=[pl.BlockSpec(memory_space=pltpu.SMEM),
                  pl.BlockSpec((tq, qw), lambda h, i: (i, qb + h)),
                  pl.BlockSpec((s, HEAD_DIM), lambda h, i: (0, kb + h)),
                  pl.BlockSpec((s, HEAD_DIM), lambda h, i: (0, vb + h)),
                  pl.BlockSpec((t, HEAD_DIM), lambda h, i: (0, kb + h)),
                  pl.BlockSpec((t, HEAD_DIM), lambda h, i: (0, vb + h))],
        out_specs=pl.BlockSpec((tq, qw), lambda h, i: (i, h)),
        out_shape=jax.ShapeDtypeStruct((s, n_kv * qw), BF16),
        compiler_params=_params(("parallel", "parallel")),
        name="window_attn",
    )(sink, p, p, p, pc, pc)


NBR_Q_ROWS = 4
NBR_WIN_ROWS = NBR_Q_ROWS + NA_KH


def _nbr_build_bias(rpb_ref, head, mats_scr, tab_scr, n_rows):
    na, nb = 2 * NA_KH - 1, 2 * NA_KW - 1
    shape = (GRID_W, 2 * GRID_W)
    wk = lax.broadcasted_iota(jnp.int32, shape, 0)
    lane = lax.broadcasted_iota(jnp.int32, shape, 1)
    wq = lane & (GRID_W - 1)
    dc = jnp.clip(wk - wq, -(NA_KW - 1), NA_KW - 1) + (NA_KW - 1)
    c_start = jnp.clip(wq - NA_KW // 2, 0, GRID_W - NA_KW)
    col_ok = jnp.logical_and(wk >= c_start, wk < c_start + NA_KW)

    def build(a, carry):
        base = (head * na + a) * nb
        mat = jnp.zeros(shape, F32)
        for b in range(nb):
            mat = jnp.where(dc == b, rpb_ref[base + b] * LOG2E, mat)
        mats_scr[a] = jnp.where(col_ok, mat, NEG_INF)
        return carry

    lax.fori_loop(0, na, build, 0)
    neg = jnp.full(shape, NEG_INF, F32)
    first_half = lane < GRID_W
    for cl, r0 in enumerate((0, NBR_Q_ROWS, n_rows - NBR_Q_ROWS)):
        ws = min(max(r0 - NA_KH // 2, 0), n_rows - NBR_WIN_ROWS)
        for kr in range(NBR_WIN_ROWS):
            for qp in range(NBR_Q_ROWS // 2):
                halves = []
                for r in (r0 + 2 * qp, r0 + 2 * qp + 1):
                    r_start = min(max(r - NA_KH // 2, 0), n_rows - NA_KH)
                    inside = r_start <= ws + kr < r_start + NA_KH
                    halves.append(mats_scr[ws + kr - r + NA_KH - 1] if inside else neg)
                tab_scr[cl, kr * GRID_W:(kr + 1) * GRID_W, qp * 2 * GRID_W:(qp + 1) * 2 * GRID_W] = (
                    jnp.where(first_half, halves[0], halves[1]))


NBR_HEADS_PER_STEP = 4


def _nbr_kernel(rpb_ref, q_ref, k_ref, v_ref, kc_ref, vc_ref, o_ref, mats_scr, tab_scr, *, tq, win, s_len):
    i = pl.program_id(1)
    nq = pl.num_programs(1)

    @pl.when(i == 0)
    def _():
        for hh in range(NBR_HEADS_PER_STEP):
            _nbr_build_bias(rpb_ref, pl.program_id(0) * NBR_HEADS_PER_STEP + hh, mats_scr, tab_scr.at[hh],
                            s_len // GRID_W)

    q0 = i * tq
    lead = (NA_KH // 2) * GRID_W
    start = pl.multiple_of(jnp.clip(q0 - lead, 0, s_len - win), lead)
    cls = jnp.where(i == 0, 0, jnp.where(i == nq - 1, 2, 1))
    heads = [slice(hh * HEAD_DIM, (hh + 1) * HEAD_DIM) for hh in range(NBR_HEADS_PER_STEP)]

    def scores(sl):
        return _dot_nt(k_ref[pl.ds(start, win), sl], q_ref[:, sl]), _dot_nt(kc_ref[:, sl], q_ref[:, sl])

    ahead = 2
    pending = [scores(sl) for sl in heads[:ahead]]
    for hh, sl in enumerate(heads):
        s_win, s_ctx = pending.pop(0)
        s_win = s_win + tab_scr[hh, cls]
        m = jnp.maximum(jnp.max(s_win, axis=0, keepdims=True), jnp.max(s_ctx, axis=0, keepdims=True))
        p_win = jnp.exp2(s_win - m)
        p_ctx = jnp.exp2(s_ctx - m)
        l = jnp.sum(p_win, axis=0, keepdims=True) + jnp.sum(p_ctx, axis=0, keepdims=True)
        o = (_dot_tn(vc_ref[:, sl], p_ctx.astype(BF16))
             + _dot_tn(v_ref[pl.ds(start, win), sl], p_win.astype(BF16)))
        o_ref[:, sl] = ((o * (1.0 / l)).T).astype(BF16)
        if hh + ahead < len(heads):
            pending.append(scores(heads[hh + ahead]))


def _nbr_call(p, pc, offs, n_heads, rpb):
    s = p.shape[0]
    t = pc.shape[0]
    tq = NBR_Q_ROWS * GRID_W
    win = NBR_WIN_ROWS * GRID_W
    bw = NBR_HEADS_PER_STEP * HEAD_DIM
    qb, kb, vb = (o // bw for o in offs)
    assert s // GRID_W >= NBR_WIN_ROWS and rpb.shape[1:] == (2 * NA_KH - 1, 2 * NA_KW - 1)
    assert n_heads % NBR_HEADS_PER_STEP == 0 and all(o % bw == 0 for o in offs)
    return pl.pallas_call(
        functools.partial(_nbr_kernel, tq=tq, win=win, s_len=s),
        grid=(n_heads // NBR_HEADS_PER_STEP, s // tq),
        in_specs=[pl.BlockSpec(memory_space=pltpu.SMEM),
                  pl.BlockSpec((tq, bw), lambda h, i: (i, qb + h)),
                  pl.BlockSpec((s, bw), lambda h, i: (0, kb + h)),
                  pl.BlockSpec((s, bw), lambda h, i: (0, vb + h)),
                  pl.BlockSpec((t, bw), lambda h, i: (0, kb + h)),
                  pl.BlockSpec((t, bw), lambda h, i: (0, vb + h))],
        out_specs=pl.BlockSpec((tq, bw), lambda h, i: (i, h)),
        out_shape=jax.ShapeDtypeStruct((s, n_heads * HEAD_DIM), BF16),
        scratch_shapes=[pltpu.VMEM((2 * NA_KH - 1, GRID_W, 2 * GRID_W), F32),
                        pltpu.VMEM((NBR_HEADS_PER_STEP, 3, win, tq), F32)],
        compiler_params=_params(("parallel", "arbitrary")),
        name="nbr_attn",
    )(rpb.astype(F32).reshape(-1), p, p, p, pc, pc)


def _outproj_kernel(x_ref, ya_ref, yb_ref, wa_ref, wb_ref, g_ref, o_ref):
    acc = _dot(ya_ref[...], wa_ref[...]) + _dot(yb_ref[...], wb_ref[...])
    o_ref[...] = x_ref[...] + g_ref[...] * acc


def _outproj_call(x, ya, yb, w, layer, gate, tm, name):
    m, d = x.shape
    k = ya.shape[1]
    assert yb.shape[1] == k and w.shape[1] == 2 * k
    return pl.pallas_call(
        _outproj_kernel,
        grid=(m // tm,),
        in_specs=[pl.BlockSpec((tm, d), lambda i: (i, 0)),
                  pl.BlockSpec((tm, k), lambda i: (i, 0)),
                  pl.BlockSpec((tm, k), lambda i: (i, 0)),
                  pl.BlockSpec((None, k, d), lambda i: (layer, 0, 0)),
                  pl.BlockSpec((None, k, d), lambda i: (layer, 1, 0)),
                  pl.BlockSpec((1, d), lambda i: (0, 0))],
        out_specs=pl.BlockSpec((tm, d), lambda i: (i, 0)),
        out_shape=jax.ShapeDtypeStruct((m, d), F32),
        compiler_params=_params(("parallel",)),
        name=name,
    )(x, ya, yb, w, w, gate)


HALO = BF16_SUBLANES


def _ffn_kernel(xm_ref, xp_ref, xn_ref, gn_ref, sh_ref, sc_ref, gate_ref, wa_ref, wg_ref,
                cwa_ref, cwg_ref, cba_ref, cbg_ref, wd_ref, o_ref,
                h_scr, acc_scr, ua0_scr, ua1_scr, ug0_scr, ug1_scr):
    i, j = pl.program_id(0), pl.program_id(1)
    ni, nj = pl.num_programs(0), pl.num_programs(1)
    tm = xm_ref.shape[0]

    @pl.when(j == 0)
    def _():
        nm = lambda x: _norm_mod(x, gn_ref[...], sh_ref[...], sc_ref[...])
        h_scr[0:HALO, :] = jnp.where(i > 0, nm(xp_ref[...]), 0.0).astype(BF16)
        h_scr[HALO:HALO + tm, :] = nm(xm_ref[...]).astype(BF16)
        h_scr[HALO + tm:, :] = jnp.where(i < ni - 1, nm(xn_ref[...]), 0.0).astype(BF16)
        acc_scr[...] = jnp.zeros_like(acc_scr)

    def conv(u_scr, cw_ref, cb_ref, sl):
        out = cb_ref[:, sl] + u_scr[pl.ds(HALO - 1, tm), :] * cw_ref[0:1, sl]
        for t in range(1, CONV_W):
            out = out + u_scr[pl.ds(HALO - 1 + t, tm), :] * cw_ref[t:t + 1, sl]
        return out

    def silu_conv(u_scr, sl):
        g = conv(u_scr, cwg_ref, cbg_ref, sl)
        return g * (1.0 / (1.0 + jnp.exp(-g)))

    h = h_scr[...]
    half = wa_ref.shape[1] // 2
    lo, hi = slice(0, half), slice(half, 2 * half)
    ug0_scr[...] = _dot(h, wg_ref[:, lo])
    ug1_scr[...] = _dot(h, wg_ref[:, hi])
    sg0 = silu_conv(ug0_scr, lo)
    ua0_scr[...] = _dot(h, wa_ref[:, lo])
    sg1 = silu_conv(ug1_scr, hi)
    ua1_scr[...] = _dot(h, wa_ref[:, hi])
    z0 = sg0 * conv(ua0_scr, cwa_ref, cba_ref, lo)
    acc_scr[...] += _dot(z0.astype(BF16), wd_ref[lo, :])
    z1 = sg1 * conv(ua1_scr, cwa_ref, cba_ref, hi)
    acc_scr[...] += _dot(z1.astype(BF16), wd_ref[hi, :])

    @pl.when(j == nj - 1)
    def _():
        o_ref[...] = xm_ref[...] + gate_ref[...] * acc_scr[...]


def _ffn_call(x, gn, sh, sc, gate, w_up, conv_w, conv_b, w_down, layer, tm, tf, name):
    m, d = x.shape
    f = w_down.shape[1]
    nfb = f // tf
    hb = tm // HALO
    last = m // HALO - 1
    row = lambda i, j: (0, 0)
    return pl.pallas_call(
        _ffn_kernel,
        grid=(m // tm, nfb),
        in_specs=[pl.BlockSpec((tm, d), lambda i, j: (i, 0)),
                  pl.BlockSpec((HALO, d), lambda i, j: (jnp.maximum(i * hb - 1, 0), 0)),
                  pl.BlockSpec((HALO, d), lambda i, j: (jnp.minimum((i + 1) * hb, last), 0)),
                  pl.BlockSpec((1, d), row), pl.BlockSpec((1, d), row), pl.BlockSpec((1, d), row),
                  pl.BlockSpec((1, d), row),
                  pl.BlockSpec((None, d, tf), lambda i, j: (layer, 0, j)),
                  pl.BlockSpec((None, d, tf), lambda i, j: (layer, 0, nfb + j)),
                  pl.BlockSpec((None, CONV_W, tf), lambda i, j: (layer, 0, j)),
                  pl.BlockSpec((None, CONV_W, tf), lambda i, j: (layer, 0, nfb + j)),
                  pl.BlockSpec((None, 1, tf), lambda i, j: (layer, 0, j)),
                  pl.BlockSpec((None, 1, tf), lambda i, j: (layer, 0, nfb + j)),
                  pl.BlockSpec((None, tf, d), lambda i, j: (layer, j, 0))],
        out_specs=pl.BlockSpec((tm, d), lambda i, j: (i, 0)),
        out_shape=jax.ShapeDtypeStruct((m, d), F32),
        scratch_shapes=[pltpu.VMEM((tm + 2 * HALO, d), BF16),
                        pltpu.VMEM((tm, d), F32)] + [pltpu.VMEM((tm + 2 * HALO, tf // 2), F32)] * 4,
        compiler_params=_params(("parallel", "arbitrary")),
        name=name,
    )(x, x, x, gn, sh, sc, gate, w_up, w_up, conv_w, conv_w, conv_b, conv_b, w_down)


def _offsets(sizes):
    offs, o = [], 0
    for s in sizes:
        offs.append(o)
        o += s
    return offs


def _chunk_classes(sizes, classes):
    out = []
    for s, c in zip(sizes, classes):
        out += [c] * (s // EPI_CHUNK)
    return out


def _gain_vector(sizes, gains):
    parts = []
    for s, g in zip(sizes, gains):
        parts.append(jnp.ones((s,), F32) if g is None else jnp.tile(g.astype(F32), s // g.shape[0]))
    return jnp.concatenate(parts).reshape(1, -1)


def _lambda_init(layer_idx):
    return 0.8 - 0.6 * math.exp(-0.3 * layer_idx)


def _forward(x, c, ctx, c_ctx, w_ada, b_ada, norm1_g, w_in, w_out, a_qk_g, a_lambda, a_subln_g,
             b_qk_g, c_qk_g, c_sink, d_qk_g, d_rpb, norm2_g, w_up, conv_w, conv_b, w_down):
    s, d = x.shape
    t = ctx.shape[0]
    depth = w_ada.shape[0]
    n_heads = d // (2 * HEAD_DIM)
    n_kv = n_heads // GQA_GROUP
    hw = n_heads * HEAD_DIM
    kvw = n_kv * HEAD_DIM
    even_sizes = (hw, hw, hw, hw, kvw, kvw)
    odd_sizes = (hw, kvw, kvw, hw, hw, hw)
    tm = 512
    tf = 512

    mod = _mod_call(c, c_ctx, w_ada, b_ada)
    rope_a = _rope_tables(s, 16)
    rope_b = _rope_tables(s, 32)
    tabs = rope_a + rope_b
    scale_a = A_SUB ** -0.5
    scale = HEAD_DIM ** -0.5

    w_in_first = w_in[:1].astype(BF16)
    late_f32 = {"w_in": w_in, "w_out": w_out, "w_up": w_up, "w_down": w_down}
    late = None
    conv_b3 = conv_b.reshape(depth, 1, -1)

    xc = ctx
    for i in range(depth):
        need_ctx = i < depth - 1
        e = i // 2
        vec = lambda r, k: mod[i, r:r + 1, k * d:(k + 1) * d]
        g1n = norm1_g[i].reshape(1, d)
        g2n = norm2_g[i].reshape(1, d)
        if i % 2 == 0:
            sizes = even_sizes
            classes = [(A_SUB, "A"), (A_SUB, "A"), (0, None), (HEAD_DIM, "B"), (HEAD_DIM, "B"), (0, None)]
            gains = [a_qk_g[e, 0] * (scale_a * LOG2E), a_qk_g[e, 1], None,
                     b_qk_g[e, 0] * (scale * LOG2E), b_qk_g[e, 1], None]
        else:
            sizes = odd_sizes
            classes = [(HEAD_DIM, "B"), (HEAD_DIM, "B"), (0, None), (HEAD_DIM, None), (HEAD_DIM, None), (0, None)]
            gains = [c_qk_g[e, 0] * (scale * LOG2E), c_qk_g[e, 1], None,
                     d_qk_g[e, 0] * (scale * LOG2E), d_qk_g[e, 1], None]
        offs = _offsets(sizes)
        chunk_cls = _chunk_classes(sizes, classes)
        gain = _gain_vector(sizes, gains)

        w_in_b, w_in_idx = (w_in_first, 0) if i == 0 else (late["w_in"], i)
        p = _inproj_call(x, g1n, vec(0, 0), vec(0, 1), w_in_b, w_in_idx, gain, chunk_cls, tabs, tm, f"inproj{i}")
        pc = _inproj_call(xc, g1n, vec(1, 0), vec(1, 1), w_in_b, w_in_idx, gain, chunk_cls, None, t,
                          f"inproj_ctx{i}")

        if i % 2 == 0:
            lam_init = _lambda_init(i)
            fa = functools.partial(_flash_call, "diff", offs=offs[0:3], n_heads=n_heads, lam_p=a_lambda[e],
                                   subln=a_subln_g[e], lam_init=lam_init)
            fb = functools.partial(_flash_call, "gqa", offs=offs[3:6], n_heads=n_kv, lam_p=None,
                                   subln=None, lam_init=None)
            bound_a = (A_SUB * LOGIT_BOUND_SLACK) * jnp.max(jnp.abs(gains[0])) * jnp.max(jnp.abs(gains[1]))
            bound_b = (HEAD_DIM * LOGIT_BOUND_SLACK) * jnp.max(jnp.abs(gains[3])) * jnp.max(jnp.abs(gains[4]))
            tq_a, tq_b = 512, 256
            slabs_a, slabs_b = {}, {}
            if late is None:
                for k, a in late_f32.items():
                    for slabs, steps in ((slabs_a, n_heads * (s // tq_a)), (slabs_b, n_kv * (s // tq_b))):
                        shape = _cast_slab_shape(a.shape, steps)
                        if shape is not None:
                            slabs[k] = a.reshape(shape)
                            break
            res_a = lax.cond(bound_a <= MAX_LOGIT_BOUND,
                             lambda: fa(p, p, pc, tq=tq_a, name=f"diff_attn{i}", bound=bound_a.reshape(1),
                                        cast_slabs=tuple(slabs_a.values())),
                             lambda: fa(p, p, pc, tq=tq_a, name=f"diff_attn_online{i}",
                                        cast_slabs=tuple(slabs_a.values())))
            res_b = lax.cond(bound_b <= MAX_LOGIT_BOUND,
                             lambda: fb(p, p, pc, tq=tq_b, name=f"gqa_attn{i}", bound=bound_b.reshape(1),
                                        cast_slabs=tuple(slabs_b.values())),
                             lambda: fb(p, p, pc, tq=tq_b, name=f"gqa_attn_online{i}",
                                        cast_slabs=tuple(slabs_b.values())))
            ya = res_a[0] if slabs_a else res_a
            yb = res_b[0] if slabs_b else res_b
            if late is None:
                cast = dict(zip(slabs_a, res_a[1:])) if slabs_a else {}
                cast.update(dict(zip(slabs_b, res_b[1:])) if slabs_b else {})
                late = {k: (cast[k].reshape(a.shape) if k in cast else a.astype(BF16)) for k, a in late_f32.items()}
            if need_ctx:
                ya_c = fa(pc, None, pc, tq=t, name=f"diff_attn_ctx{i}")
                yb_c = fb(pc, None, pc, tq=t, name=f"gqa_attn_ctx{i}")
        else:
            ya = _win_call(p, pc, offs[0:3], n_kv, c_sink[e], 256)
            yb = _nbr_call(p, pc, offs[3:6], n_heads, d_rpb[e])
            if need_ctx:
                raise NotImplementedError("context stream after an odd layer")

        w_out_b, w_up_b, w_down_b = late["w_out"], late["w_up"], late["w_down"]
        x = _outproj_call(x, ya, yb, w_out_b, i, vec(0, 2), tm, f"outproj{i}")
        x = _ffn_call(x, g2n, vec(0, 3), vec(0, 4), vec(0, 5), w_up_b, conv_w, conv_b3, w_down_b, i, tm, tf,
                      f"ffn{i}")
        if need_ctx:
            xc = _outproj_call(xc, ya_c, yb_c, w_out_b, i, vec(1, 2), t, f"outproj_ctx{i}")
            xc = _ffn_call(xc, g2n, vec(1, 3), vec(1, 4), vec(1, 5), w_up_b, conv_w, conv_b3, w_down_b, i, t, tf,
                           f"ffn_ctx{i}")
    return x


def kernel(x, c, ctx, c_ctx, w_ada, b_ada, norm1_g, w_in, w_out, a_qk_g, a_lambda, a_subln_g, b_qk_g, c_qk_g,
           c_sink, d_qk_g, d_rpb, norm2_g, w_up, conv_w, conv_b, w_down):
    outs = [_forward(x[b], c[b], ctx[b], c_ctx, w_ada, b_ada, norm1_g, w_in, w_out, a_qk_g, a_lambda, a_subln_g,
                     b_qk_g, c_qk_g, c_sink, d_qk_g, d_rpb, norm2_g, w_up, conv_w, conv_b, w_down)
            for b in range(x.shape[0])]
    return jnp.stack(outs)
```
